```python
import jax, jax.numpy as jnp
from jax import lax
import numpy as np

D_MODEL = 1024
BATCH = 8
SEQ = 2048
DEPTH = 4

CHUNK = 64
Q_BLOCK = 128
EPS = 1e-6

A_HEADS = 8
A_HEAD_DIM = 64
A_WIDTH = A_HEADS * A_HEAD_DIM
KV_LATENT = 128
IDX_HEADS = 8
IDX_DIM = 64
TOPK_MAX = 256

B_WIDTH = D_MODEL - A_WIDTH
B_GROUPS = 4
B_GROUP_DIM = B_WIDTH // B_GROUPS
SGU_CHUNK = 128

POOL_WINDOWS = (2, 4, 8, 16)
C_GROUPS = len(POOL_WINDOWS)
C_GROUP_DIM = D_MODEL // C_GROUPS

N_GROUPS = 4
EXPERTS_PER_GROUP = 8
N_EXPERTS = N_GROUPS * EXPERTS_PER_GROUP
TOP_K_IN_GROUP = 2
D_EXPERT = 256

IN_SIZES = (A_WIDTH, KV_LATENT, IDX_HEADS * IDX_DIM, IDX_DIM, IDX_HEADS, B_WIDTH, B_WIDTH)
D_IN = sum(IN_SIZES)
IN_OFFSETS = tuple(int(o) for o in np.cumsum(IN_SIZES)[:-1])

N_EVEN = (DEPTH + 1) // 2
N_ODD = DEPTH // 2

kernel_name = "hybrid_dsa_sgu_pool_hmoe_trunk"


def rmsnorm(x, g):
    xf = x.astype(jnp.float32)
    y = xf * lax.rsqrt(jnp.mean(xf * xf, axis=-1, keepdims=True) + EPS)
    return (y * g.astype(jnp.float32)).astype(x.dtype)


def layernorm(x, g):
    xf = x.astype(jnp.float32)
    mu = jnp.mean(xf, axis=-1, keepdims=True)
    var = jnp.mean(jnp.square(xf - mu), axis=-1, keepdims=True)
    return ((xf - mu) * lax.rsqrt(var + EPS) * g.astype(jnp.float32)).astype(x.dtype)


def modulate(h, shift, scale):
    return h * (1 + scale[:, None, :]) + shift[:, None, :]


def alibi_slopes(n):
    return jnp.asarray(np.array([2.0 ** (-8.0 * (i + 1) / n) for i in range(n)], dtype=np.float32))


def dsa_attention(q, k, v, q_idx, k_idx, w_idx):
    B, S = q.shape[0], q.shape[1]
    topk = min(TOPK_MAX, S // 4)
    nb = S // Q_BLOCK
    slopes = alibi_slopes(A_HEADS)
    key_pos = jnp.arange(S)
    kv = jnp.concatenate([k, v], axis=-1)

    def to_blocks(a):
        return jnp.moveaxis(a.reshape((B, nb, Q_BLOCK) + a.shape[2:]), 1, 0)

    def block(args):
        qb, qib, wib, qpos = args
        logits = jnp.einsum('bqhd,bsd->bqhs', qib, k_idx) * (IDX_DIM ** -0.5)
        score = jnp.einsum('bqhs,bqh->bqs', jax.nn.relu(logits), wib).astype(jnp.float32)
        visible = (key_pos[None, :] // CHUNK) <= (qpos[:, None] // CHUNK)
        score = jnp.where(visible[None], score, -jnp.inf)
        _, idx = lax.top_k(score, topk)
        valid = (idx // CHUNK) <= (qpos[None, :, None] // CHUNK)
        kv_sel = jax.vmap(lambda kk, ii: kk[ii])(kv, idx)
        k_sel, v_sel = jnp.split(kv_sel, 2, axis=-1)
        att = jnp.einsum('bqhd,bqkd->bqhk', qb, k_sel).astype(jnp.float32) * (A_HEAD_DIM ** -0.5)
        dist = jnp.abs(qpos[None, :, None] - idx).astype(jnp.float32)
        att = att - slopes[None, None, :, None] * dist[:, :, None, :]
        att = jnp.where(valid[:, :, None, :], att, -jnp.inf)
        p = jax.nn.softmax(att, axis=-1).astype(v.dtype)
        return jnp.einsum('bqhk,bqkd->bqhd', p, v_sel)

    out = lax.map(block, (to_blocks(q), to_blocks(q_idx), to_blocks(w_idx),
                          jnp.arange(S).reshape(nb, Q_BLOCK)))
    return jnp.moveaxis(out, 0, 1).reshape(B, S, A_WIDTH)


def spatial_gating(u, v, ln_g, w_s, b_s):
    B, S, _ = u.shape
    u = jax.nn.gelu(u)
    v = layernorm(jax.nn.gelu(v), ln_g)
    nc = S // SGU_CHUNK
    vg = v.reshape(B, nc, SGU_CHUNK, B_GROUPS, B_GROUP_DIM)
    pos = jnp.arange(SGU_CHUNK)
    mask = (pos[None, :] // CHUNK) <= (pos[:, None] // CHUNK)
    w = jnp.where(mask[None], w_s, jnp.zeros_like(w_s))
    mixed = jnp.einsum('gij,bnjgc->bnigc', w, vg) + b_s.T[None, None, :, :, None]
    return u * mixed.reshape(B, S, B_WIDTH)


def hybrid_ab(h, w_in, kv_norm, w_kv_up, sgu_norm, w_s, b_s, w_o):
    B, S, _ = h.shape
    proj = h @ w_in
    q, kv_lat, q_idx, k_idx, w_idx, u, v = jnp.split(proj, IN_OFFSETS, axis=-1)
    kv = rmsnorm(kv_lat, kv_norm) @ w_kv_up
    k, v_att = jnp.split(kv, 2, axis=-1)
    a_out = dsa_attention(q.reshape(B, S, A_HEADS, A_HEAD_DIM), k, v_att,
                          q_idx.reshape(B, S, IDX_HEADS, IDX_DIM), k_idx,
                          w_idx * (IDX_HEADS ** -0.5))
    b_out = spatial_gating(u, v, sgu_norm, w_s, b_s)
    return jnp.concatenate([a_out, b_out], axis=-1) @ w_o


def multiscale_pool(h, w_pool, scale):
    B, S, D = h.shape
    hf = h.astype(jnp.float32).reshape(B, S, C_GROUPS, C_GROUP_DIM)
    cs = jnp.concatenate([jnp.zeros((B, 1, C_GROUPS, C_GROUP_DIM), jnp.float32),
                          jnp.cumsum(hf, axis=1)], axis=1)
    t = jnp.arange(S)
    pooled = []
    for g, win in enumerate(POOL_WINDOWS):
        start = jnp.maximum(t + 1 - win, 0)
        cnt = (t + 1 - start).astype(jnp.float32)
        pooled.append((cs[:, 1:, g] - cs[:, start, g]) / cnt[None, :, None])
    y = (jnp.stack(pooled, axis=2) - hf).astype(h.dtype)
    y = jnp.einsum('bsgc,gce->bsge', y, w_pool).reshape(B, S, D)
    return y * scale


def hier_moe(h, w_gr, b_gr, w_er, b_er, w_gate, w_up, w_down):
    def per_seq(t):
        S = t.shape[0]
        g_logits = (t @ w_gr).astype(jnp.float32) + b_gr.astype(jnp.float32)
        g_prob = jax.nn.softmax(g_logits, axis=-1)
        g_sel = jnp.argmax(g_logits, axis=-1)
        p_g = jnp.take_along_axis(g_prob, g_sel[:, None], axis=1)
        e_logits = ((t @ w_er).astype(jnp.float32) + b_er.astype(jnp.float32)).reshape(S, N_GROUPS, EXPERTS_PER_GROUP)
        e_in_group = jnp.take_along_axis(e_logits, g_sel[:, None, None], axis=1)[:, 0]
        top_val, top_idx = lax.top_k(e_in_group, TOP_K_IN_GROUP)
        gate = p_g * jax.nn.softmax(top_val, axis=-1)
        expert_id = g_sel[:, None] * EXPERTS_PER_GROUP + top_idx
        combine = jnp.sum(jax.nn.one_hot(expert_id, N_EXPERTS, dtype=jnp.float32) * gate[..., None], axis=1)
        hg = jnp.einsum('sd,edf->sef', t, w_gate)
        hu = jnp.einsum('sd,edf->sef', t, w_up)
        act = jax.nn.silu(hg) * hu * combine[:, :, None].astype(t.dtype)
        return jnp.einsum('sef,efd->sd', act, w_down)
    return lax.map(per_seq, h)


def setup_inputs(seed: int = 0) -> dict:
    key = jax.random.key(seed)
    ks = iter(jax.random.split(key, 32))
    f32 = jnp.float32
    D = D_MODEL

    def nrm(shape, scale):
        return jax.random.normal(next(ks), shape, f32) * scale

    def gain(shape):
        return 1.0 + 0.02 * jax.random.normal(next(ks), shape, f32)

    return {
        "x": nrm((BATCH, SEQ, D), 1.0),
        "c": nrm((BATCH, D), 1.0),
        "w_mod": nrm((DEPTH, D, 6 * D), 0.5 * D ** -0.5),
        "b_mod": nrm((DEPTH, 6 * D), 0.02),
        "norm_mix": gain((DEPTH, D)),
        "norm_ffn": gain((DEPTH, D)),
        "w_in": nrm((N_EVEN, D, D_IN), D ** -0.5),
        "kv_norm": gain((N_EVEN, KV_LATENT)),
        "w_kv_up": nrm((N_EVEN, KV_LATENT, 2 * A_HEAD_DIM), KV_LATENT ** -0.5),
        "sgu_norm": gain((N_EVEN, B_WIDTH)),
        "w_s": nrm((N_EVEN, B_GROUPS, SGU_CHUNK, SGU_CHUNK), SGU_CHUNK ** -0.5),
        "b_s": gain((N_EVEN, B_GROUPS, SGU_CHUNK)),
        "w_o": nrm((N_EVEN, D, D), D ** -0.5),
        "w_pool": nrm((N_ODD, C_GROUPS, C_GROUP_DIM, C_GROUP_DIM), C_GROUP_DIM ** -0.5),
        "pool_scale": gain((N_ODD, D)),
        "w_gr": nrm((DEPTH, D, N_GROUPS), D ** -0.5),
        "b_gr": nrm((DEPTH, N_GROUPS), 0.01),
        "w_er": nrm((DEPTH, D, N_EXPERTS), D ** -0.5),
        "b_er": nrm((DEPTH, N_EXPERTS), 0.01),
        "w_gate": nrm((DEPTH, N_EXPERTS, D, D_EXPERT), D ** -0.5),
        "w_up": nrm((DEPTH, N_EXPERTS, D, D_EXPERT), D ** -0.5),
        "w_down": nrm((DEPTH, N_EXPERTS, D_EXPERT, D), D_EXPERT ** -0.5),
        "norm_out": gain((D,)),
    }


def reference(x, c, w_mod, b_mod, norm_mix, norm_ffn, w_in, kv_norm, w_kv_up, sgu_norm,
              w_s, b_s, w_o, w_pool, pool_scale, w_gr, b_gr, w_er, b_er,
              w_gate, w_up, w_down, norm_out):
    cs = jax.nn.silu(c)
    for i in range(DEPTH):
        mod = cs @ w_mod[i] + b_mod[i]
        sh_m, sc_m, g_m, sh_f, sc_f, g_f = jnp.split(mod, 6, axis=-1)
        h = modulate(rmsnorm(x, norm_mix[i]), sh_m, sc_m)
        j = i // 2
        if i % 2 == 0:
            y = hybrid_ab(h, w_in[j], kv_norm[j], w_kv_up[j], sgu_norm[j], w_s[j], b_s[j], w_o[j])
        else:
            y = multiscale_pool(h, w_pool[j], pool_scale[j])
        x = x + g_m[:, None, :] * y
        h = modulate(rmsnorm(x, norm_ffn[i]), sh_f, sc_f)
        x = x + g_f[:, None, :] * hier_moe(h, w_gr[i], b_gr[i], w_er[i], b_er[i],
                                           w_gate[i], w_up[i], w_down[i])
    return rmsnorm(x, norm_out)
```

```python
import functools

import numpy as np
import jax
import jax.numpy as jnp
from jax import lax
from jax.experimental import pallas as pl
from jax.experimental.pallas import tpu as pltpu

F32 = jnp.float32
BF16 = jnp.bfloat16
I32 = jnp.int32

EPS = 1e-6
CHUNK = 64
Q_BLOCK = 128
A_HEADS = 8
A_HEAD_DIM = 64
A_WIDTH = A_HEADS * A_HEAD_DIM
KV_LATENT = 128
IDX_HEADS = 8
IDX_DIM = 64
TOPK_MAX = 256
B_GROUPS = 4
SGU_CHUNK = 128
POOL_WINDOWS = (2, 4, 8, 16)
POOL_HALO = 16
N_GROUPS = 4
EXPERTS_PER_GROUP = 8
N_EXPERTS = N_GROUPS * EXPERTS_PER_GROUP
LANES = 128
NEG_BIG = -1e30
INT_MIN = np.int32(-2 ** 31)
KEY_NEG_INF = np.int32(np.int32(-8388608) ^ np.int32(0x7FFFFFFF))
VMEM_LIMIT = 56 * 1024 * 1024

_NT = (((1,), (1,)), ((), ()))


def _params(*sem):
    return pltpu.CompilerParams(dimension_semantics=sem, vmem_limit_bytes=VMEM_LIMIT)


def _norm_mod(x, g, shift, scale):
    y = x * lax.rsqrt(jnp.mean(x * x, axis=-1, keepdims=True) + EPS)
    return (y * g) * (1.0 + scale) + shift


def _mod_kernel(c_ref, w_ref, b_ref, o_ref):
    c = c_ref[...]
    cs = c * jax.nn.sigmoid(c)
    o_ref[0] = jnp.dot(cs.astype(BF16), w_ref[0].astype(BF16),
                       preferred_element_type=F32) + b_ref[0]


def _modulation(c, w_mod, b_mod):
    depth, d, d6 = w_mod.shape
    b = c.shape[0]
    tn = 1024
    return pl.pallas_call(
        _mod_kernel,
        grid=(depth, d6 // tn),
        in_specs=[pl.BlockSpec((b, d), lambda i, j: (0, 0)),
                  pl.BlockSpec((1, d, tn), lambda i, j: (i, 0, j)),
                  pl.BlockSpec((1, 1, tn), lambda i, j: (i, 0, j))],
        out_specs=pl.BlockSpec((1, b, tn), lambda i, j: (i, 0, j)),
        out_shape=jax.ShapeDtypeStruct((depth, b, d6), F32),
        compiler_params=_params("parallel", "parallel"),
        name="modulation",
    )(c, w_mod, b_mod.reshape(depth, 1, d6))


_C_Q = 0
_C_KV = _C_Q + A_WIDTH
_C_QI = _C_KV + KV_LATENT
_C_MISC = _C_QI + IDX_HEADS * IDX_DIM
_C_U = _C_MISC + LANES
B_WIDTH = 512
_C_V = _C_U + B_WIDTH
_C_END = _C_V + B_WIDTH


def _in_proj_kernel(x_ref, mod_ref, g_ref, w_ref, kvn_ref, wkv_ref, lng_ref, ws_ref, bs_ref,
                    q_ref, kv_ref, qi_ref, misc_ref, b_ref):
    tm = x_ref.shape[0]
    h = _norm_mod(x_ref[...], g_ref[...], mod_ref[0, 0:1, :], mod_ref[0, 1:2, :]).astype(BF16)

    def proj(lo, hi):
        return jnp.dot(h, w_ref[:, lo:hi], preferred_element_type=F32)

    q_ref[...] = proj(_C_Q, _C_KV).astype(BF16)
    qi_ref[...] = proj(_C_QI, _C_MISC).astype(BF16)

    kvl = proj(_C_KV, _C_QI)
    kvn = kvl * lax.rsqrt(jnp.mean(kvl * kvl, axis=-1, keepdims=True) + EPS) * kvn_ref[...]
    kv_ref[...] = jnp.dot(kvn.astype(BF16), wkv_ref[...], preferred_element_type=F32).astype(BF16)

    lane = lax.broadcasted_iota(I32, (1, LANES), 1)
    idx_scale = (IDX_HEADS ** -0.5) * (IDX_DIM ** -0.5)
    w_lane = (lane >= IDX_DIM) & (lane < IDX_DIM + IDX_HEADS)
    misc_ref[...] = proj(_C_MISC, _C_U) * jnp.where(w_lane, idx_scale, 1.0)

    gu = jax.nn.gelu(proj(_C_U, _C_V))
    gv = jax.nn.gelu(proj(_C_V, _C_END))
    mu = jnp.mean(gv, axis=-1, keepdims=True)
    cen = gv - mu
    var = jnp.mean(cen * cen, axis=-1, keepdims=True)
    vn = (cen * lax.rsqrt(var + EPS) * lng_ref[...]).astype(BF16)
    gd = B_WIDTH // B_GROUPS
    for r in range(tm // SGU_CHUNK):
        rows = slice(r * SGU_CHUNK, (r + 1) * SGU_CHUNK)
        for g in range(B_GROUPS):
            cols = slice(g * gd, (g + 1) * gd)
            mixed = jnp.dot(ws_ref[g], vn[rows, cols], preferred_element_type=F32) + bs_ref[g]
            b_ref[rows, cols] = (gu[rows, cols] * mixed).astype(BF16)


def _in_proj(x, mod_l, g, w_in_r, kv_norm, w_kv_up, sgu_norm, w_s_m, b_s_b, batch, seq):
    n, d = x.shape
    tm = 512
    tps = seq // tm
    row = lambda b, t: (b * tps + t, 0)
    full2 = lambda b, t: (0, 0)
    full3 = lambda b, t: (0, 0, 0)
    outs = [(A_WIDTH, BF16), (2 * A_HEAD_DIM, BF16), (IDX_HEADS * IDX_DIM, BF16), (LANES, F32),
            (B_WIDTH, BF16)]
    return pl.pallas_call(
        _in_proj_kernel,
        grid=(batch, tps),
        in_specs=[pl.BlockSpec((tm, d), row),
                  pl.BlockSpec((1, 6, d), lambda b, t: (b, 0, 0)),
                  pl.BlockSpec((1, d), full2),
                  pl.BlockSpec(w_in_r.shape, full2),
                  pl.BlockSpec((1, KV_LATENT), full2),
                  pl.BlockSpec(w_kv_up.shape, full2),
                  pl.BlockSpec((1, B_WIDTH), full2),
                  pl.BlockSpec(w_s_m.shape, full3),
                  pl.BlockSpec(b_s_b.shape, full3)],
        out_specs=[pl.BlockSpec((tm, w), row) for w, _ in outs],
        out_shape=[jax.ShapeDtypeStruct((n, w), dt) for w, dt in outs],
        compiler_params=_params("parallel", "parallel"),
        name="in_proj_sgu",
    )(x, mod_l, g, w_in_r, kv_norm, w_kv_up, sgu_norm, w_s_m, b_s_b)


def _attn_kernel(q_ref, qi_ref, mq_ref, mk_ref, kv_ref, o_ref, key_ref, pen_ref, *, topk):
    qb, seq = key_ref.shape
    t = pl.program_id(1)
    col = lax.broadcasted_iota(I32, (qb, seq), 1)
    qpos = t * qb + lax.broadcasted_iota(I32, (qb, seq), 0)
    visible = (col // CHUNK) <= (qpos // CHUNK)

    k_idx = mk_ref[:, 0:IDX_DIM].astype(BF16)
    mq = mq_ref[...]
    qi = qi_ref[...]
    score = jnp.zeros((qb, seq), F32)
    for h in range(IDX_HEADS):
        logits = lax.dot_general(qi[:, h * IDX_DIM:(h + 1) * IDX_DIM], k_idx, _NT,
                                 preferred_element_type=F32)
        score = score + mq[:, IDX_DIM + h:IDX_DIM + h + 1] * jnp.maximum(logits, 0.0)
    score = jnp.where(visible, score, -jnp.inf)

    bits = pltpu.bitcast(score, I32)
    key_ref[...] = bits ^ ((bits >> 31) & np.int32(0x7FFFFFFF))

    kf = float(topk)

    def bit_step(i, prefix):
        trial = prefix | lax.shift_left(np.int32(1), 31 - i)
        cnt = jnp.sum(jnp.where(key_ref[...] >= (trial ^ INT_MIN), 1.0, 0.0), axis=-1, keepdims=True)
        return jnp.where(cnt >= kf, trial, prefix)

    thr = lax.fori_loop(0, 32, bit_step, jnp.zeros((qb, 1), I32)) ^ INT_MIN

    key = key_ref[...]
    n_gt = jnp.sum(jnp.where(key > thr, 1.0, 0.0), axis=-1, keepdims=True)
    n_eq = jnp.sum(jnp.where(key == thr, 1.0, 0.0), axis=-1, keepdims=True)
    need = kf - n_gt
    pen_ref[...] = jnp.where((key >= thr) & visible, 0.0, NEG_BIG)

    tie_rows = jnp.where((n_eq > need) & (thr > KEY_NEG_INF), 1.0, 0.0)

    @pl.when(jnp.max(tie_rows) > 0.0)
    def _():
        eq = key_ref[...] == thr
        nbits = int(seq - 1).bit_length()

        def idx_step(i, p):
            trial = p | lax.shift_left(np.int32(1), nbits - 1 - i)
            cnt = jnp.sum(jnp.where(eq & (col < trial), 1.0, 0.0), axis=-1, keepdims=True)
            return jnp.where(cnt < need, trial, p)

        last = lax.fori_loop(0, nbits, idx_step, jnp.zeros((qb, 1), I32))
        k2 = key_ref[...]
        sel = (k2 > thr) | ((k2 == thr) & (col <= last))
        pen_ref[...] = jnp.where(sel & visible, 0.0, NEG_BIG)

    dist = jnp.abs(qpos - col).astype(F32)
    kk = kv_ref[:, 0:A_HEAD_DIM]
    vv = kv_ref[:, A_HEAD_DIM:2 * A_HEAD_DIM]
    q = q_ref[...]
    outs = []
    for h in range(A_HEADS):
        slope = float(2.0 ** (-8.0 * (h + 1) / A_HEADS))
        att = lax.dot_general(q[:, h * A_HEAD_DIM:(h + 1) * A_HEAD_DIM], kk, _NT,
                              preferred_element_type=F32)
        att = att * (A_HEAD_DIM ** -0.5) - slope * dist + pen_ref[...]
        m = jnp.max(att, axis=-1, keepdims=True)
        p = jnp.exp(att - m)
        l = jnp.sum(p, axis=-1, keepdims=True)
        o = jnp.dot(p.astype(BF16), vv, preferred_element_type=F32)
        outs.append(o / l)
    o_ref[...] = jnp.concatenate(outs, axis=-1).astype(BF16)


def _attention(q, qi, misc, kv, batch, seq):
    n = q.shape[0]
    qb = Q_BLOCK
    nb = seq // qb
    topk = min(TOPK_MAX, seq // 4)
    rowq = lambda b, t: (b * nb + t, 0)
    rowk = lambda b, t: (b, 0)
    return pl.pallas_call(
        functools.partial(_attn_kernel, topk=topk),
        grid=(batch, nb),
        in_specs=[pl.BlockSpec((qb, A_WIDTH), rowq),
                  pl.BlockSpec((qb, IDX_HEADS * IDX_DIM), rowq),
                  pl.BlockSpec((qb, LANES), rowq),
                  pl.BlockSpec((seq, LANES), rowk),
                  pl.BlockSpec((seq, 2 * A_HEAD_DIM), rowk)],
        out_specs=pl.BlockSpec((qb, A_WIDTH), rowq),
        out_shape=jax.ShapeDtypeStruct((n, A_WIDTH), BF16),
        scratch_shapes=[pltpu.VMEM((qb, seq), I32), pltpu.VMEM((qb, seq), F32)],
        compiler_params=_params("parallel", "parallel"),
        name="dsa_attention",
    )(q, qi, misc, misc, kv)


def _out_proj_kernel(x_ref, a_ref, b_ref, w_ref, mod_ref, o_ref):
    y = jnp.dot(a_ref[...], w_ref[0:A_WIDTH, :], preferred_element_type=F32)
    y = y + jnp.dot(b_ref[...], w_ref[A_WIDTH:, :], preferred_element_type=F32)
    o_ref[...] = x_ref[...] + mod_ref[0, 2:3, :] * y


def _out_proj(x, a, bo, w_o, mod_l, batch, seq):
    n, d = x.shape
    tm = 512
    tps = seq // tm
    row = lambda b, t: (b * tps + t, 0)
    return pl.pallas_call(
        _out_proj_kernel,
        grid=(batch, tps),
        in_specs=[pl.BlockSpec((tm, d), row),
                  pl.BlockSpec((tm, A_WIDTH), row),
                  pl.BlockSpec((tm, B_WIDTH), row),
                  pl.BlockSpec(w_o.shape, lambda b, t: (0, 0)),
                  pl.BlockSpec((1, 6, d), lambda b, t: (b, 0, 0))],
        out_specs=pl.BlockSpec((tm, d), row),
        out_shape=jax.ShapeDtypeStruct((n, d), F32),
        compiler_params=_params("parallel", "parallel"),
        name="out_proj",
    )(x, a, bo, w_o, mod_l)


def _pool_kernel(x_ref, halo_ref, mod_ref, g_ref, w_ref, sc_ref, o_ref):
    tm, d = x_ref.shape
    t = pl.program_id(1)
    g = g_ref[...]
    shift = mod_ref[0, 0:1, :]
    scale = mod_ref[0, 1:2, :]
    x = x_ref[...]
    h = _norm_mod(x, g, shift, scale)
    hh = _norm_mod(halo_ref[...], g, shift, scale)
    hh = jnp.where(t > 0, hh, 0.0)
    ext = jnp.concatenate([hh, h], axis=0)
    pos = t * tm + lax.broadcasted_iota(I32, (tm, 1), 0)
    gd = d // len(POOL_WINDOWS)
    ys = []
    for gi, win in enumerate(POOL_WINDOWS):
        cols = slice(gi * gd, (gi + 1) * gd)
        cur = ext[:, cols]
        k = 1
        while k < win:
            cur = cur + pltpu.roll(cur, k, axis=0)
            k *= 2
        cnt = jnp.minimum(pos + 1, win).astype(F32)
        y = (cur[POOL_HALO:, :] / cnt - h[:, cols]).astype(BF16)
        ys.append(jnp.dot(y, w_ref[gi], preferred_element_type=F32))
    y = jnp.concatenate(ys, axis=-1) * sc_ref[...]
    o_ref[...] = x + mod_ref[0, 2:3, :] * y


def _pool_mixer(x, mod_l, g, w_pool, scale, batch, seq):
    n, d = x.shape
    tm = 512
    tps = seq // tm
    row = lambda b, t: (b * tps + t, 0)
    hpt = tm // POOL_HALO
    halo = lambda b, t: (jnp.maximum((b * tps + t) * hpt - 1, 0), 0)
    return pl.pallas_call(
        _pool_kernel,
        grid=(batch, tps),
        in_specs=[pl.BlockSpec((tm, d), row),
                  pl.BlockSpec((POOL_HALO, d), halo),
                  pl.BlockSpec((1, 6, d), lambda b, t: (b, 0, 0)),
                  pl.BlockSpec((1, d), lambda b, t: (0, 0)),
                  pl.BlockSpec(w_pool.shape, lambda b, t: (0, 0, 0)),
                  pl.BlockSpec((1, d), lambda b, t: (0, 0))],
        out_specs=pl.BlockSpec((tm, d), row),
        out_shape=jax.ShapeDtypeStruct((n, d), F32),
        compiler_params=_params("parallel", "parallel"),
        name="pool_mixer",
    )(x, x, mod_l, g, w_pool, scale)


_R_EXP0 = N_GROUPS


def _router_kernel(x_ref, mod_ref, g_ref, w_ref, b_ref, h_ref, comb_ref):
    tm = x_ref.shape[0]
    h = _norm_mod(x_ref[...], g_ref[...], mod_ref[0, 3:4, :], mod_ref[0, 4:5, :])
    h_ref[...] = h.astype(BF16)
    logits = jnp.dot(h, w_ref[...], precision=lax.Precision.HIGHEST,
                     preferred_element_type=F32) + b_ref[...]
    lane = lax.broadcasted_iota(I32, (tm, LANES), 1)
    big = np.int32(LANES)

    def top1(vals):
        m = jnp.max(vals, axis=-1, keepdims=True)
        idx = jnp.min(jnp.where(vals == m, lane, big), axis=-1, keepdims=True)
        return m, idx

    gl = jnp.where(lane < N_GROUPS, logits, -jnp.inf)
    gmax, gsel = top1(gl)
    p_g = 1.0 / jnp.sum(jnp.exp(gl - gmax), axis=-1, keepdims=True)
    e_lo = _R_EXP0 + gsel * EXPERTS_PER_GROUP
    el = jnp.where((lane >= e_lo) & (lane < e_lo + EXPERTS_PER_GROUP), logits, -jnp.inf)
    v1, i1 = top1(el)
    v2, i2 = top1(jnp.where(lane == i1, -jnp.inf, el))
    e2 = jnp.exp(v2 - v1)
    g1 = p_g / (1.0 + e2)
    g2 = p_g * e2 / (1.0 + e2)
    comb_ref[...] = jnp.where(lane == i1, g1, 0.0) + jnp.where(lane == i2, g2, 0.0)


def _router(x, mod_l, g, w_r, b_r, batch, seq):
    n, d = x.shape
    tm = 512
    tps = seq // tm
    row = lambda b, t: (b * tps + t, 0)
    full = lambda b, t: (0, 0)
    return pl.pallas_call(
        _router_kernel,
        grid=(batch, tps),
        in_specs=[pl.BlockSpec((tm, d), row),
                  pl.BlockSpec((1, 6, d), lambda b, t: (b, 0, 0)),
                  pl.BlockSpec((1, d), full),
                  pl.BlockSpec((d, LANES), full),
                  pl.BlockSpec((1, LANES), full)],
        out_specs=[pl.BlockSpec((tm, d), row), pl.BlockSpec((tm, LANES), row)],
        out_shape=[jax.ShapeDtypeStruct((n, d), BF16), jax.ShapeDtypeStruct((n, LANES), F32)],
        compiler_params=_params("parallel", "parallel"),
        name="router",
    )(x, mod_l, g, w_r, b_r)


def _experts_kernel(x_ref, h_ref, comb_ref, wg_ref, wu_ref, wd_ref, mod_ref, gout_ref, o_ref, acc_ref,
                    *, final_norm):
    e = pl.program_id(1)

    @pl.when(e == 0)
    def _():
        acc_ref[...] = jnp.zeros_like(acc_ref)

    h = h_ref[...]
    hg = jnp.dot(h, wg_ref[0], preferred_element_type=F32)
    hu = jnp.dot(h, wu_ref[0], preferred_element_type=F32)
    lane = lax.broadcasted_iota(I32, comb_ref.shape, 1)
    gate = jnp.sum(jnp.where(lane == e + _R_EXP0, comb_ref[...], 0.0), axis=-1, keepdims=True)
    act = (hg * jax.nn.sigmoid(hg)) * hu * gate
    acc_ref[...] += jnp.dot(act.astype(BF16), wd_ref[0], preferred_element_type=F32)

    @pl.when(e == pl.num_programs(1) - 1)
    def _():
        y = x_ref[...] + mod_ref[0, 5:6, :] * acc_ref[...]
        if final_norm:
            y = y * lax.rsqrt(jnp.mean(y * y, axis=-1, keepdims=True) + EPS) * gout_ref[...]
        o_ref[...] = y


def _experts(x, h, comb, wg, wu, wd, mod_l, g_out, batch, seq, final_norm):
    n, d = x.shape
    ne, _, f = wg.shape
    tm = min(1024, seq)
    tps = seq // tm
    row = lambda i, e: (i, 0)
    return pl.pallas_call(
        functools.partial(_experts_kernel, final_norm=final_norm),
        grid=(n // tm, ne),
        in_specs=[pl.BlockSpec((tm, d), row),
                  pl.BlockSpec((tm, d), row),
                  pl.BlockSpec((tm, LANES), row),
                  pl.BlockSpec((1, d, f), lambda i, e: (e, 0, 0)),
                  pl.BlockSpec((1, d, f), lambda i, e: (e, 0, 0)),
                  pl.BlockSpec((1, f, d), lambda i, e: (e, 0, 0)),
                  pl.BlockSpec((1, 6, d), lambda i, e: (i // tps, 0, 0)),
                  pl.BlockSpec((1, d), lambda i, e: (0, 0))],
        out_specs=pl.BlockSpec((tm, d), row),
        out_shape=jax.ShapeDtypeStruct((n, d), F32),
        scratch_shapes=[pltpu.VMEM((tm, d), F32)],
        compiler_params=_params("parallel", "arbitrary"),
        name="experts",
    )(x, h, comb, wg, wu, wd, mod_l, g_out)


def kernel(x, c, w_mod, b_mod, norm_mix, norm_ffn, w_in, kv_norm, w_kv_up, sgu_norm, w_s, b_s, w_o,
           w_pool, pool_scale, w_gr, b_gr, w_er, b_er, w_gate, w_up, w_down, norm_out):
    batch, seq, d = x.shape
    depth = w_mod.shape[0]
    n = batch * seq
    xs = x.reshape(n, d)
    mod = _modulation(c, w_mod, b_mod).reshape(depth, batch, 6, d)

    pos = np.arange(SGU_CHUNK)
    sgu_mask = jnp.asarray((pos[None, :] // CHUNK) <= (pos[:, None] // CHUNK))
    sizes = (A_WIDTH, KV_LATENT, IDX_HEADS * IDX_DIM, IDX_DIM, IDX_HEADS, B_WIDTH, B_WIDTH)
    offs = np.concatenate([[0], np.cumsum(sizes)])
    pad = LANES - IDX_DIM - IDX_HEADS

    for i in range(depth):
        j = i // 2
        mod_l = mod[i]
        g_mix = norm_mix[i].reshape(1, d)
        if i % 2 == 0:
            wi = w_in[j]
            w_in_r = jnp.concatenate(
                [wi[:, offs[0]:offs[3]], wi[:, offs[3]:offs[5]], jnp.zeros((d, pad), wi.dtype),
                 wi[:, offs[5]:offs[7]]], axis=1).astype(BF16)
            w_s_m = jnp.where(sgu_mask[None], w_s[j], 0.0).astype(BF16)
            b_s_b = jnp.broadcast_to(b_s[j][:, :, None], (B_GROUPS, SGU_CHUNK, B_WIDTH // B_GROUPS))
            q, kv, qi, misc, bo = _in_proj(
                xs, mod_l, g_mix, w_in_r, kv_norm[j].reshape(1, -1), w_kv_up[j].astype(BF16),
                sgu_norm[j].reshape(1, -1), w_s_m, b_s_b, batch, seq)
            a = _attention(q, qi, misc, kv, batch, seq)
            xs = _out_proj(xs, a, bo, w_o[j].astype(BF16), mod_l, batch, seq)
        else:
            xs = _pool_mixer(xs, mod_l, g_mix, w_pool[j].astype(BF16), pool_scale[j].reshape(1, d),
                             batch, seq)
        w_r = jnp.concatenate([w_gr[i], w_er[i],
                               jnp.zeros((d, LANES - N_GROUPS - N_EXPERTS), F32)], axis=1)
        b_r = jnp.concatenate([b_gr[i], b_er[i],
                               jnp.zeros((LANES - N_GROUPS - N_EXPERTS,), F32)]).reshape(1, LANES)
        h, comb = _router(xs, mod_l, norm_ffn[i].reshape(1, d), w_r, b_r, batch, seq)
        xs = _experts(xs, h, comb, w_gate[i].astype(BF16), w_up[i].astype(BF16),
                      w_down[i].astype(BF16), mod_l, norm_out.reshape(1, d), batch, seq,
                      final_norm=(i == depth - 1))
    return xs.reshape(batch, seq, d)
```

```python
import functools

import numpy as np
import jax
import jax.numpy as jnp
from jax import lax
from jax.experimental import pallas as pl
from jax.experimental.pallas import tpu as pltpu

F32 = jnp.float32
BF16 = jnp.bfloat16
I32 = jnp.int32

EPS = 1e-6
CHUNK = 64
Q_BLOCK = 256
A_HEADS = 8
A_HEAD_DIM = 64
A_WIDTH = A_HEADS * A_HEAD_DIM
KV_LATENT = 128
IDX_HEADS = 8
IDX_DIM = 64
TOPK_MAX = 256
B_GROUPS = 4
SGU_CHUNK = 128
POOL_WINDOWS = (2, 4, 8, 16)
POOL_HALO = 16
N_GROUPS = 4
EXPERTS_PER_GROUP = 8
N_EXPERTS = N_GROUPS * EXPERTS_PER_GROUP
LANES = 128
NEG_BIG = -1e30
INT_MIN = np.int32(-2 ** 31)
KEY_NEG_INF = np.int32(np.int32(-8388608) ^ np.int32(0x7FFFFFFF))
VMEM_LIMIT = 56 * 1024 * 1024

_NT = (((1,), (1,)), ((), ()))


def _params(*sem):
    return pltpu.CompilerParams(dimension_semantics=sem, vmem_limit_bytes=VMEM_LIMIT)


def _norm_mod(x, g, shift, scale):
    y = x * lax.rsqrt(jnp.mean(x * x, axis=-1, keepdims=True) + EPS)
    return (y * g) * (1.0 + scale) + shift


def _mod_kernel(c_ref, w_ref, b_ref, o_ref):
    c = c_ref[...]
    cs = c * jax.nn.sigmoid(c)
    o_ref[0] = jnp.dot(cs.astype(BF16), w_ref[0].astype(BF16),
                       preferred_element_type=F32) + b_ref[0]


def _modulation(c, w_mod, b_mod):
    depth, d, d6 = w_mod.shape
    b = c.shape[0]
    tn = 1024
    return pl.pallas_call(
        _mod_kernel,
        grid=(depth, d6 // tn),
        in_specs=[pl.BlockSpec((b, d), lambda i, j: (0, 0)),
                  pl.BlockSpec((1, d, tn), lambda i, j: (i, 0, j)),
                  pl.BlockSpec((1, 1, tn), lambda i, j: (i, 0, j))],
        out_specs=pl.BlockSpec((1, b, tn), lambda i, j: (i, 0, j)),
        out_shape=jax.ShapeDtypeStruct((depth, b, d6), F32),
        compiler_params=_params("parallel", "parallel"),
        name="modulation",
    )(c, w_mod, b_mod.reshape(depth, 1, d6))


_C_Q = 0
_C_KV = _C_Q + A_WIDTH
_C_QI = _C_KV + KV_LATENT
_C_MISC = _C_QI + IDX_HEADS * IDX_DIM
_C_U = _C_MISC + LANES
B_WIDTH = 512
_C_V = _C_U + B_WIDTH
_C_END = _C_V + B_WIDTH


def _in_proj_kernel(x_ref, mod_ref, g_ref, w_ref, kvn_ref, wkv_ref, lng_ref, ws_ref, bs_ref,
                    q_ref, kv_ref, qi_ref, misc_ref, b_ref):
    tm = x_ref.shape[0]
    h = _norm_mod(x_ref[...], g_ref[...], mod_ref[0, 0:1, :], mod_ref[0, 1:2, :]).astype(BF16)

    def proj(lo, hi):
        return jnp.dot(h, w_ref[:, lo:hi], preferred_element_type=F32)

    q_ref[...] = proj(_C_Q, _C_KV).astype(BF16)
    qi_ref[...] = proj(_C_QI, _C_MISC).astype(BF16)

    kvl = proj(_C_KV, _C_QI)
    kvn = kvl * lax.rsqrt(jnp.mean(kvl * kvl, axis=-1, keepdims=True) + EPS) * kvn_ref[...]
    kv_ref[...] = jnp.dot(kvn.astype(BF16), wkv_ref[...], preferred_element_type=F32).astype(BF16)

    lane = lax.broadcasted_iota(I32, (1, LANES), 1)
    idx_scale = (IDX_HEADS ** -0.5) * (IDX_DIM ** -0.5)
    w_lane = (lane >= IDX_DIM) & (lane < IDX_DIM + IDX_HEADS)
    misc_ref[...] = proj(_C_MISC, _C_U) * jnp.where(w_lane, idx_scale, 1.0)

    gu = jax.nn.gelu(proj(_C_U, _C_V))
    gv = jax.nn.gelu(proj(_C_V, _C_END))
    mu = jnp.mean(gv, axis=-1, keepdims=True)
    cen = gv - mu
    var = jnp.mean(cen * cen, axis=-1, keepdims=True)
    vn = (cen * lax.rsqrt(var + EPS) * lng_ref[...]).astype(BF16)
    gd = B_WIDTH // B_GROUPS
    for r in range(tm // SGU_CHUNK):
        rows = slice(r * SGU_CHUNK, (r + 1) * SGU_CHUNK)
        for g in range(B_GROUPS):
            cols = slice(g * gd, (g + 1) * gd)
            mixed = jnp.dot(ws_ref[g], vn[rows, cols], preferred_element_type=F32) + bs_ref[g]
            b_ref[rows, cols] = (gu[rows, cols] * mixed).astype(BF16)


def _in_proj(x, mod_l, g, w_in_r, kv_norm, w_kv_up, sgu_norm, w_s_m, b_s_b, batch, seq):
    n, d = x.shape
    tm = 512
    tps = seq // tm
    row = lambda b, t: (b * tps + t, 0)
    full2 = lambda b, t: (0, 0)
    full3 = lambda b, t: (0, 0, 0)
    outs = [(A_WIDTH, BF16), (2 * A_HEAD_DIM, BF16), (IDX_HEADS * IDX_DIM, BF16), (LANES, F32),
            (B_WIDTH, BF16)]
    return pl.pallas_call(
        _in_proj_kernel,
        grid=(batch, tps),
        in_specs=[pl.BlockSpec((tm, d), row),
                  pl.BlockSpec((1, 6, d), lambda b, t: (b, 0, 0)),
                  pl.BlockSpec((1, d), full2),
                  pl.BlockSpec(w_in_r.shape, full2),
                  pl.BlockSpec((1, KV_LATENT), full2),
                  pl.BlockSpec(w_kv_up.shape, full2),
                  pl.BlockSpec((1, B_WIDTH), full2),
                  pl.BlockSpec(w_s_m.shape, full3),
                  pl.BlockSpec(b_s_b.shape, full3)],
        out_specs=[pl.BlockSpec((tm, w), row) for w, _ in outs],
        out_shape=[jax.ShapeDtypeStruct((n, w), dt) for w, dt in outs],
        compiler_params=_params("parallel", "parallel"),
        name="in_proj_sgu",
    )(x, mod_l, g, w_in_r, kv_norm, w_kv_up, sgu_norm, w_s_m, b_s_b)


def _attn_block(q_ref, qi_ref, mq_ref, mk_ref, kv_ref, o_ref, key_ref, pen_ref, *, topk, t, svis):
    qb = q_ref.shape[0]
    col = lax.broadcasted_iota(I32, (qb, svis), 1)
    qpos = t * qb + lax.broadcasted_iota(I32, (qb, svis), 0)
    visible = (col // CHUNK) <= (qpos // CHUNK)

    if svis > topk:
        k_idx = mk_ref[0:svis, 0:IDX_DIM].astype(BF16)
        mq = mq_ref[...]
        qi = qi_ref[...]
        score = jnp.zeros((qb, svis), F32)
        for h in range(IDX_HEADS):
            logits = lax.dot_general(qi[:, h * IDX_DIM:(h + 1) * IDX_DIM], k_idx, _NT,
                                     preferred_element_type=F32)
            score = score + mq[:, IDX_DIM + h:IDX_DIM + h + 1] * jnp.maximum(logits, 0.0)
        score = jnp.where(visible, score, -jnp.inf)

        bits = pltpu.bitcast(score, I32)
        key_ref[:, 0:svis] = bits ^ ((bits >> 31) & np.int32(0x7FFFFFFF))

        kf = float(topk)

        def bit_step(i, prefix):
            trial = prefix | lax.shift_left(np.int32(1), 31 - i)
            cnt = jnp.sum(jnp.where(key_ref[:, 0:svis] >= (trial ^ INT_MIN), 1.0, 0.0),
                          axis=-1, keepdims=True)
            return jnp.where(cnt >= kf, trial, prefix)

        thr = lax.fori_loop(0, 32, bit_step, jnp.zeros((qb, 1), I32)) ^ INT_MIN

        key = key_ref[:, 0:svis]
        n_gt = jnp.sum(jnp.where(key > thr, 1.0, 0.0), axis=-1, keepdims=True)
        n_eq = jnp.sum(jnp.where(key == thr, 1.0, 0.0), axis=-1, keepdims=True)
        need = kf - n_gt
        pen_ref[:, 0:svis] = jnp.where((key >= thr) & visible, 0.0, NEG_BIG)

        tie_rows = jnp.where((n_eq > need) & (thr > KEY_NEG_INF), 1.0, 0.0)

        @pl.when(jnp.max(tie_rows) > 0.0)
        def _():
            eq = key_ref[:, 0:svis] == thr
            nbits = int(svis - 1).bit_length()

            def idx_step(i, p):
                trial = p | lax.shift_left(np.int32(1), nbits - 1 - i)
                cnt = jnp.sum(jnp.where(eq & (col < trial), 1.0, 0.0), axis=-1, keepdims=True)
                return jnp.where(cnt < need, trial, p)

            last = lax.fori_loop(0, nbits, idx_step, jnp.zeros((qb, 1), I32))
            k2 = key_ref[:, 0:svis]
            sel = (k2 > thr) | ((k2 == thr) & (col <= last))
            pen_ref[:, 0:svis] = jnp.where(sel & visible, 0.0, NEG_BIG)
    else:
        pen_ref[:, 0:svis] = jnp.where(visible, 0.0, NEG_BIG)

    dist = jnp.abs(qpos - col).astype(F32)
    kk = kv_ref[0:svis, 0:A_HEAD_DIM]
    vv = kv_ref[0:svis, A_HEAD_DIM:2 * A_HEAD_DIM]
    q = q_ref[...]
    outs = []
    for h in range(A_HEADS):
        slope = float(2.0 ** (-8.0 * (h + 1) / A_HEADS))
        att = lax.dot_general(q[:, h * A_HEAD_DIM:(h + 1) * A_HEAD_DIM], kk, _NT,
                              preferred_element_type=F32)
        att = att * (A_HEAD_DIM ** -0.5) - slope * dist + pen_ref[:, 0:svis]
        m = jnp.max(att, axis=-1, keepdims=True)
        p = jnp.exp(att - m)
        l = jnp.sum(p, axis=-1, keepdims=True)
        o = jnp.dot(p.astype(BF16), vv, preferred_element_type=F32)
        outs.append(o / l)
    o_ref[...] = jnp.concatenate(outs, axis=-1).astype(BF16)


def _attn_kernel(q_ref, qi_ref, mq_ref, mk_ref, kv_ref, o_ref, key_ref, pen_ref, *, topk):
    qb, seq = key_ref.shape
    t = pl.program_id(1)
    for v in range(seq // qb):
        @pl.when(t == v)
        def _(v=v):
            _attn_block(q_ref, qi_ref, mq_ref, mk_ref, kv_ref, o_ref, key_ref, pen_ref,
                        topk=topk, t=v, svis=(v + 1) * qb)


def _attention(q, qi, misc, kv, batch, seq):
    n = q.shape[0]
    qb = min(Q_BLOCK, seq)
    nb = seq // qb
    topk = min(TOPK_MAX, seq // 4)
    rowq = lambda b, t: (b * nb + t, 0)
    rowk = lambda b, t: (b, 0)
    return pl.pallas_call(
        functools.partial(_attn_kernel, topk=topk),
        grid=(batch, nb),
        in_specs=[pl.BlockSpec((qb, A_WIDTH), rowq),
                  pl.BlockSpec((qb, IDX_HEADS * IDX_DIM), rowq),
                  pl.BlockSpec((qb, LANES), rowq),
                  pl.BlockSpec((seq, LANES), rowk),
                  pl.BlockSpec((seq, 2 * A_HEAD_DIM), rowk)],
        out_specs=pl.BlockSpec((qb, A_WIDTH), rowq),
        out_shape=jax.ShapeDtypeStruct((n, A_WIDTH), BF16),
        scratch_shapes=[pltpu.VMEM((qb, seq), I32), pltpu.VMEM((qb, seq), F32)],
        compiler_params=_params("parallel", "parallel"),
        name="dsa_attention",
    )(q, qi, misc, misc, kv)


def _out_proj_kernel(x_ref, a_ref, b_ref, w_ref, mod_ref, o_ref):
    y = jnp.dot(a_ref[...], w_ref[0:A_WIDTH, :], preferred_element_type=F32)
    y = y + jnp.dot(b_ref[...], w_ref[A_WIDTH:, :], preferred_element_type=F32)
    o_ref[...] = x_ref[...] + mod_ref[0, 2:3, :] * y


def _out_proj(x, a, bo, w_o, mod_l, batch, seq):
    n, d = x.shape
    tm = 512
    tps = seq // tm
    row = lambda b, t: (b * tps + t, 0)
    return pl.pallas_call(
        _out_proj_kernel,
        grid=(batch, tps),
        in_specs=[pl.BlockSpec((tm, d), row),
                  pl.BlockSpec((tm, A_WIDTH), row),
                  pl.BlockSpec((tm, B_WIDTH), row),
                  pl.BlockSpec(w_o.shape, lambda b, t: (0, 0)),
                  pl.BlockSpec((1, 6, d), lambda b, t: (b, 0, 0))],
        out_specs=pl.BlockSpec((tm, d), row),
        out_shape=jax.ShapeDtypeStruct((n, d), F32),
        compiler_params=_params("parallel", "parallel"),
        name="out_proj",
    )(x, a, bo, w_o, mod_l)


def _pool_kernel(x_ref, halo_ref, mod_ref, g_ref, w_ref, sc_ref, o_ref):
    tm, d = x_ref.shape
    t = pl.program_id(1)
    g = g_ref[...]
    shift = mod_ref[0, 0:1, :]
    scale = mod_ref[0, 1:2, :]
    x = x_ref[...]
    h = _norm_mod(x, g, shift, scale)
    hh = _norm_mod(halo_ref[...], g, shift, scale)
    hh = jnp.where(t > 0, hh, 0.0)
    ext = jnp.concatenate([hh, h], axis=0)
    pos = t * tm + lax.broadcasted_iota(I32, (tm, 1), 0)
    gd = d // len(POOL_WINDOWS)
    ys = []
    for gi, win in enumerate(POOL_WINDOWS):
        cols = slice(gi * gd, (gi + 1) * gd)
        cur = ext[:, cols]
        k = 1
        while k < win:
            cur = cur + pltpu.roll(cur, k, axis=0)
            k *= 2
        cnt = jnp.minimum(pos + 1, win).astype(F32)
        y = (cur[POOL_HALO:, :] / cnt - h[:, cols]).astype(BF16)
        ys.append(jnp.dot(y, w_ref[gi], preferred_element_type=F32))
    y = jnp.concatenate(ys, axis=-1) * sc_ref[...]
    o_ref[...] = x + mod_ref[0, 2:3, :] * y


def _pool_mixer(x, mod_l, g, w_pool, scale, batch, seq):
    n, d = x.shape
    tm = 512
    tps = seq // tm
    row = lambda b, t: (b * tps + t, 0)
    hpt = tm // POOL_HALO
    halo = lambda b, t: (jnp.maximum((b * tps + t) * hpt - 1, 0), 0)
    return pl.pallas_call(
        _pool_kernel,
        grid=(batch, tps),
        in_specs=[pl.BlockSpec((tm, d), row),
                  pl.BlockSpec((POOL_HALO, d), halo),
                  pl.BlockSpec((1, 6, d), lambda b, t: (b, 0, 0)),
                  pl.BlockSpec((1, d), lambda b, t: (0, 0)),
                  pl.BlockSpec(w_pool.shape, lambda b, t: (0, 0, 0)),
                  pl.BlockSpec((1, d), lambda b, t: (0, 0))],
        out_specs=pl.BlockSpec((tm, d), row),
        out_shape=jax.ShapeDtypeStruct((n, d), F32),
        compiler_params=_params("parallel", "parallel"),
        name="pool_mixer",
    )(x, x, mod_l, g, w_pool, scale)


_R_EXP0 = N_GROUPS


def _router_kernel(x_ref, mod_ref, g_ref, w_ref, b_ref, h_ref, comb_ref):
    tm = x_ref.shape[0]
    h = _norm_mod(x_ref[...], g_ref[...], mod_ref[0, 3:4, :], mod_ref[0, 4:5, :])
    h_ref[...] = h.astype(BF16)
    logits = jnp.dot(h, w_ref[...], precision=lax.Precision.HIGHEST,
                     preferred_element_type=F32) + b_ref[...]
    lane = lax.broadcasted_iota(I32, (tm, LANES), 1)
    big = np.int32(LANES)

    def top1(vals):
        m = jnp.max(vals, axis=-1, keepdims=True)
        idx = jnp.min(jnp.where(vals == m, lane, big), axis=-1, keepdims=True)
        return m, idx

    gl = jnp.where(lane < N_GROUPS, logits, -jnp.inf)
    gmax, gsel = top1(gl)
    p_g = 1.0 / jnp.sum(jnp.exp(gl - gmax), axis=-1, keepdims=True)
    e_lo = _R_EXP0 + gsel * EXPERTS_PER_GROUP
    el = jnp.where((lane >= e_lo) & (lane < e_lo + EXPERTS_PER_GROUP), logits, -jnp.inf)
    v1, i1 = top1(el)
    v2, i2 = top1(jnp.where(lane == i1, -jnp.inf, el))
    e2 = jnp.exp(v2 - v1)
    g1 = p_g / (1.0 + e2)
    g2 = p_g * e2 / (1.0 + e2)
    comb_ref[...] = jnp.where(lane == i1, g1, 0.0) + jnp.where(lane == i2, g2, 0.0)


def _router(x, mod_l, g, w_r, b_r, batch, seq):
    n, d = x.shape
    tm = 512
    tps = seq // tm
    row = lambda b, t: (b * tps + t, 0)
    full = lambda b, t: (0, 0)
    return pl.pallas_call(
        _router_kernel,
        grid=(batch, tps),
        in_specs=[pl.BlockSpec((tm, d), row),
                  pl.BlockSpec((1, 6, d), lambda b, t: (b, 0, 0)),
                  pl.BlockSpec((1, d), full),
                  pl.BlockSpec((d, LANES), full),
                  pl.BlockSpec((1, LANES), full)],
        out_specs=[pl.BlockSpec((tm, d), row), pl.BlockSpec((tm, LANES), row)],
        out_shape=[jax.ShapeDtypeStruct((n, d), BF16), jax.ShapeDtypeStruct((n, LANES), F32)],
        compiler_params=_params("parallel", "parallel"),
        name="router",
    )(x, mod_l, g, w_r, b_r)


def _experts_kernel(x_ref, h_ref, comb_ref, wg_ref, wu_ref, wd_ref, mod_ref, gout_ref, o_ref, acc_ref,
                    *, final_norm):
    e = pl.program_id(1)

    @pl.when(e == 0)
    def _():
        acc_ref[...] = jnp.zeros_like(acc_ref)

    h = h_ref[...]
    hg = jnp.dot(h, wg_ref[0], preferred_element_type=F32)
    hu = jnp.dot(h, wu_ref[0], preferred_element_type=F32)
    lane = lax.broadcasted_iota(I32, comb_ref.shape, 1)
    gate = jnp.sum(jnp.where(lane == e + _R_EXP0, comb_ref[...], 0.0), axis=-1, keepdims=True)
    act = (hg * jax.nn.sigmoid(hg)) * hu * gate
    acc_ref[...] += jnp.dot(act.astype(BF16), wd_ref[0], preferred_element_type=F32)

    @pl.when(e == pl.num_programs(1) - 1)
    def _():
        y = x_ref[...] + mod_ref[0, 5:6, :] * acc_ref[...]
        if final_norm:
            y = y * lax.rsqrt(jnp.mean(y * y, axis=-1, keepdims=True) + EPS) * gout_ref[...]
        o_ref[...] = y


def _experts(x, h, comb, wg, wu, wd, mod_l, g_out, batch, seq, final_norm):
    n, d = x.shape
    ne, _, f = wg.shape
    tm = min(1024, seq)
    tps = seq // tm
    row = lambda i, e: (i, 0)
    return pl.pallas_call(
        functools.partial(_experts_kernel, final_norm=final_norm),
        grid=(n // tm, ne),
        in_specs=[pl.BlockSpec((tm, d), row),
                  pl.BlockSpec((tm, d), row),
                  pl.BlockSpec((tm, LANES), row),
                  pl.BlockSpec((1, d, f), lambda i, e: (e, 0, 0)),
                  pl.BlockSpec((1, d, f), lambda i, e: (e, 0, 0)),
                  pl.BlockSpec((1, f, d), lambda i, e: (e, 0, 0)),
                  pl.BlockSpec((1, 6, d), lambda i, e: (i // tps, 0, 0)),
                  pl.BlockSpec((1, d), lambda i, e: (0, 0))],
        out_specs=pl.BlockSpec((tm, d), row),
        out_shape=jax.ShapeDtypeStruct((n, d), F32),
        scratch_shapes=[pltpu.VMEM((tm, d), F32)],
        compiler_params=_params("parallel", "arbitrary"),
        name="experts",
    )(x, h, comb, wg, wu, wd, mod_l, g_out)


def kernel(x, c, w_mod, b_mod, norm_mix, norm_ffn, w_in, kv_norm, w_kv_up, sgu_norm, w_s, b_s, w_o,
           w_pool, pool_scale, w_gr, b_gr, w_er, b_er, w_gate, w_up, w_down, norm_out):
    batch, seq, d = x.shape
    depth = w_mod.shape[0]
    n = batch * seq
    xs = x.reshape(n, d)
    mod = _modulation(c, w_mod, b_mod).reshape(depth, batch, 6, d)

    pos = np.arange(SGU_CHUNK)
    sgu_mask = jnp.asarray((pos[None, :] // CHUNK) <= (pos[:, None] // CHUNK))
    sizes = (A_WIDTH, KV_LATENT, IDX_HEADS * IDX_DIM, IDX_DIM, IDX_HEADS, B_WIDTH, B_WIDTH)
    offs = np.concatenate([[0], np.cumsum(sizes)])
    pad = LANES - IDX_DIM - IDX_HEADS

    for i in range(depth):
        j = i // 2
        mod_l = mod[i]
        g_mix = norm_mix[i].reshape(1, d)
        if i % 2 == 0:
            wi = w_in[j]
            w_in_r = jnp.concatenate(
                [wi[:, offs[0]:offs[3]], wi[:, offs[3]:offs[5]], jnp.zeros((d, pad), wi.dtype),
                 wi[:, offs[5]:offs[7]]], axis=1).astype(BF16)
            w_s_m = jnp.where(sgu_mask[None], w_s[j], 0.0).astype(BF16)
            b_s_b = jnp.broadcast_to(b_s[j][:, :, None], (B_GROUPS, SGU_CHUNK, B_WIDTH // B_GROUPS))
            q, kv, qi, misc, bo = _in_proj(
                xs, mod_l, g_mix, w_in_r, kv_norm[j].reshape(1, -1), w_kv_up[j].astype(BF16),
                sgu_norm[j].reshape(1, -1), w_s_m, b_s_b, batch, seq)
            a = _attention(q, qi, misc, kv, batch, seq)
            xs = _out_proj(xs, a, bo, w_o[j].astype(BF16), mod_l, batch, seq)
        else:
            xs = _pool_mixer(xs, mod_l, g_mix, w_pool[j].astype(BF16), pool_scale[j].reshape(1, d),
                             batch, seq)
        w_r = jnp.concatenate([w_gr[i], w_er[i],
                               jnp.zeros((d, LANES - N_GROUPS - N_EXPERTS), F32)], axis=1)
        b_r = jnp.concatenate([b_gr[i], b_er[i],
                               jnp.zeros((LANES - N_GROUPS - N_EXPERTS,), F32)]).reshape(1, LANES)
        h, comb = _router(xs, mod_l, norm_ffn[i].reshape(1, d), w_r, b_r, batch, seq)
        xs = _experts(xs, h, comb, w_gate[i].astype(BF16), w_up[i].astype(BF16),
                      w_down[i].astype(BF16), mod_l, norm_out.reshape(1, d), batch, seq,
                      final_norm=(i == depth - 1))
    return xs.reshape(batch, seq, d)
```

```python
import functools

import numpy as np
import jax
import jax.numpy as jnp
from jax import lax
from jax.experimental import pallas as pl
from jax.experimental.pallas import tpu as pltpu

F32 = jnp.float32
BF16 = jnp.bfloat16
I32 = jnp.int32

EPS = 1e-6
CHUNK = 64
CHUNK_SHIFT = 6
Q_BLOCK = 256
A_HEADS = 8
A_HEAD_DIM = 64
A_WIDTH = A_HEADS * A_HEAD_DIM
KV_LATENT = 128
IDX_HEADS = 8
IDX_DIM = 64
TOPK_MAX = 256
B_GROUPS = 4
SGU_CHUNK = 128
POOL_WINDOWS = (2, 4, 8, 16)
POOL_HALO = 16
N_GROUPS = 4
EXPERTS_PER_GROUP = 8
N_EXPERTS = N_GROUPS * EXPERTS_PER_GROUP
LANES = 128
NEG_BIG = -1e30
INT_MIN = np.int32(-2 ** 31)
KEY_NEG_INF = np.int32(np.int32(-8388608) ^ np.int32(0x7FFFFFFF))
VMEM_LIMIT = 56 * 1024 * 1024

_NT = (((1,), (1,)), ((), ()))


def _params(*sem):
    return pltpu.CompilerParams(dimension_semantics=sem, vmem_limit_bytes=VMEM_LIMIT)


def _norm_mod(x, g, shift, scale):
    y = x * lax.rsqrt(jnp.mean(x * x, axis=-1, keepdims=True) + EPS)
    return (y * g) * (1.0 + scale) + shift


def _mod_kernel(c_ref, w_ref, b_ref, o_ref):
    c = c_ref[...]
    cs = c * jax.nn.sigmoid(c)
    o_ref[0] = jnp.dot(cs.astype(BF16), w_ref[0].astype(BF16),
                       preferred_element_type=F32) + b_ref[0]


def _modulation(c, w_mod, b_mod):
    depth, d, d6 = w_mod.shape
    b = c.shape[0]
    tn = 1024
    return pl.pallas_call(
        _mod_kernel,
        grid=(depth, d6 // tn),
        in_specs=[pl.BlockSpec((b, d), lambda i, j: (0, 0)),
                  pl.BlockSpec((1, d, tn), lambda i, j: (i, 0, j)),
                  pl.BlockSpec((1, 1, tn), lambda i, j: (i, 0, j))],
        out_specs=pl.BlockSpec((1, b, tn), lambda i, j: (i, 0, j)),
        out_shape=jax.ShapeDtypeStruct((depth, b, d6), F32),
        compiler_params=_params("parallel", "parallel"),
        name="modulation",
    )(c, w_mod, b_mod.reshape(depth, 1, d6))


_C_Q = 0
_C_KV = _C_Q + A_WIDTH
_C_QI = _C_KV + KV_LATENT
_C_MISC = _C_QI + IDX_HEADS * IDX_DIM
_C_U = _C_MISC + LANES
B_WIDTH = 512
_C_V = _C_U + B_WIDTH
_C_END = _C_V + B_WIDTH


def _in_proj_kernel(x_ref, mod_ref, g_ref, w_ref, kvn_ref, wkv_ref, lng_ref, ws_ref, bs_ref,
                    q_ref, kv_ref, qi_ref, misc_ref, b_ref):
    tm = x_ref.shape[0]
    h = _norm_mod(x_ref[...], g_ref[...], mod_ref[0, 0:1, :], mod_ref[0, 1:2, :]).astype(BF16)

    def proj(lo, hi):
        return jnp.dot(h, w_ref[:, lo:hi], preferred_element_type=F32)

    q_ref[...] = proj(_C_Q, _C_KV).astype(BF16)
    qi_ref[...] = proj(_C_QI, _C_MISC).astype(BF16)

    kvl = proj(_C_KV, _C_QI)
    kvn = kvl * lax.rsqrt(jnp.mean(kvl * kvl, axis=-1, keepdims=True) + EPS) * kvn_ref[...]
    kv_ref[...] = jnp.dot(kvn.astype(BF16), wkv_ref[...], preferred_element_type=F32).astype(BF16)

    lane = lax.broadcasted_iota(I32, (1, LANES), 1)
    idx_scale = (IDX_HEADS ** -0.5) * (IDX_DIM ** -0.5)
    w_lane = (lane >= IDX_DIM) & (lane < IDX_DIM + IDX_HEADS)
    misc_ref[...] = proj(_C_MISC, _C_U) * jnp.where(w_lane, idx_scale, 1.0)

    gu = jax.nn.gelu(proj(_C_U, _C_V))
    gv = jax.nn.gelu(proj(_C_V, _C_END))
    mu = jnp.mean(gv, axis=-1, keepdims=True)
    cen = gv - mu
    var = jnp.mean(cen * cen, axis=-1, keepdims=True)
    vn = (cen * lax.rsqrt(var + EPS) * lng_ref[...]).astype(BF16)
    gd = B_WIDTH // B_GROUPS
    for r in range(tm // SGU_CHUNK):
        rows = slice(r * SGU_CHUNK, (r + 1) * SGU_CHUNK)
        for g in range(B_GROUPS):
            cols = slice(g * gd, (g + 1) * gd)
            mixed = jnp.dot(ws_ref[g], vn[rows, cols], preferred_element_type=F32) + bs_ref[g]
            b_ref[rows, cols] = (gu[rows, cols] * mixed).astype(BF16)


def _in_proj(x, mod_l, g, w_in_r, kv_norm, w_kv_up, sgu_norm, w_s_m, b_s_b, batch, seq):
    n, d = x.shape
    tm = 512
    tps = seq // tm
    row = lambda b, t: (b * tps + t, 0)
    full2 = lambda b, t: (0, 0)
    full3 = lambda b, t: (0, 0, 0)
    outs = [(A_WIDTH, BF16), (2 * A_HEAD_DIM, BF16), (IDX_HEADS * IDX_DIM, BF16), (LANES, F32),
            (B_WIDTH, BF16)]
    return pl.pallas_call(
        _in_proj_kernel,
        grid=(batch, tps),
        in_specs=[pl.BlockSpec((tm, d), row),
                  pl.BlockSpec((1, 6, d), lambda b, t: (b, 0, 0)),
                  pl.BlockSpec((1, d), full2),
                  pl.BlockSpec(w_in_r.shape, full2),
                  pl.BlockSpec((1, KV_LATENT), full2),
                  pl.BlockSpec(w_kv_up.shape, full2),
                  pl.BlockSpec((1, B_WIDTH), full2),
                  pl.BlockSpec(w_s_m.shape, full3),
                  pl.BlockSpec(b_s_b.shape, full3)],
        out_specs=[pl.BlockSpec((tm, w), row) for w, _ in outs],
        out_shape=[jax.ShapeDtypeStruct((n, w), dt) for w, dt in outs],
        compiler_params=_params("parallel", "parallel"),
        name="in_proj_sgu",
    )(x, mod_l, g, w_in_r, kv_norm, w_kv_up, sgu_norm, w_s_m, b_s_b)


I16 = jnp.int16
SEL_FIELD_BITS = 5
ATT_SCALE = A_HEAD_DIM ** -0.5
assert ATT_SCALE == 2.0 ** round(np.log2(ATT_SCALE)), "folded into bf16 q, must be a power of two"


def _lane_tiles(x):
    return [x[:, j * LANES:(j + 1) * LANES] for j in range(x.shape[1] // LANES)]


def _fold_lanes(x):
    tiles = _lane_tiles(x)
    acc = tiles[0]
    for tile in tiles[1:]:
        acc = acc + tile
    return acc


def _lane_total(x):
    return jnp.dot(x.astype(BF16), jnp.ones((LANES, LANES), BF16), preferred_element_type=F32)


def _radix4_select(tiles_ref, nkt, kneed):
    _, qb, _ = tiles_ref.shape
    c1 = 1
    c2 = c1 + (1 << SEL_FIELD_BITS)
    c3 = c2 + (1 << (2 * SEL_FIELD_BITS))
    fmask = (1 << SEL_FIELD_BITS) - 1

    def step(i, prefix):
        unit = lax.shift_left(np.int32(1), 14 - 2 * i)
        ts = [(prefix + (j * unit - 32768)).astype(I16) for j in (1, 2, 3)]

        def count_tile(kt, acc):
            for vt in _lane_tiles(tiles_ref[kt]):
                acc = acc + jnp.where(
                    vt >= ts[2], np.int16(c3),
                    jnp.where(vt >= ts[1], np.int16(c2),
                              jnp.where(vt >= ts[0], np.int16(c1), np.int16(0))))
            return acc

        acc = lax.fori_loop(0, nkt, count_tile, jnp.zeros((qb, LANES), I16)).astype(I32)
        cnt = [_lane_total(((acc >> (j * SEL_FIELD_BITS)) & fmask).astype(F32)) for j in range(3)]
        digit = jnp.where(cnt[2] >= kneed, 3,
                          jnp.where(cnt[1] >= kneed, 2, jnp.where(cnt[0] >= kneed, 1, 0)))
        return prefix + digit * unit

    return lax.fori_loop(0, 8, step, jnp.zeros((qb, LANES), I32))


def _attn_kernel(q_ref, qi_ref, mq_ref, mk_ref, kv_ref, o_ref,
                 key_ref, hi_ref, lo_ref, pen_ref, nd_ref, w_ref, s_ref, m_ref, l_ref, acc_ref, *, topk):
    _, qb, tk = key_ref.shape
    t = pl.program_id(1)
    nkt = t + 1
    kf = float(topk)
    row = lax.broadcasted_iota(I32, (qb, tk), 0)
    lane = lax.broadcasted_iota(I32, (qb, tk), 1)
    qchunk = (t * qb + row) >> CHUNK_SHIFT
    rel = row - lane

    def tile_geom(kt):
        col = kt * tk + lane
        visible = (col >> CHUNK_SHIFT) <= qchunk
        neg_dist = -jnp.abs(rel + (t - kt) * tk).astype(F32)
        return col, visible, neg_dist

    def row_off(kt):
        return pl.multiple_of(kt * tk, tk)

    @pl.when(nkt * tk <= topk)
    def _():
        def fill(kt, carry):
            _, visible, neg_dist = tile_geom(kt)
            pen_ref[kt] = jnp.where(visible, 0.0, NEG_BIG)
            nd_ref[kt] = neg_dist
            return carry

        lax.fori_loop(0, nkt, fill, 0)

    @pl.when(nkt * tk > topk)
    def _():
        qi = qi_ref[...]
        mq = mq_ref[...]
        qih = [qi[:, h * IDX_DIM:(h + 1) * IDX_DIM] for h in range(IDX_HEADS)]
        for h in range(IDX_HEADS):
            w_ref[h] = jnp.broadcast_to(mq[:, IDX_DIM + h:IDX_DIM + h + 1], (qb, LANES))

        def score_tile(kt, carry):
            k_idx = mk_ref[pl.ds(row_off(kt), tk), 0:IDX_DIM].astype(BF16)
            parts = [jnp.zeros((qb, LANES), F32)] * (tk // LANES)
            for h in range(IDX_HEADS):
                logits = lax.dot_general(qih[h], k_idx, _NT, preferred_element_type=F32)
                w = w_ref[h]
                parts = [p + w * jnp.maximum(lt, 0.0) for p, lt in zip(parts, _lane_tiles(logits))]
            _, visible, neg_dist = tile_geom(kt)
            score = jnp.where(visible, jnp.concatenate(parts, axis=-1), -jnp.inf)
            bits = pltpu.bitcast(score, I32)
            key = bits ^ ((bits >> 31) & np.int32(0x7FFFFFFF))
            key_ref[kt] = key
            hi_ref[kt] = (key >> 16).astype(I16)
            lo_ref[kt] = ((key & 0xFFFF) - 32768).astype(I16)
            nd_ref[kt] = neg_dist
            return carry

        lax.fori_loop(0, nkt, score_tile, 0)

        t_hi = _radix4_select(hi_ref, nkt, jnp.full((qb, LANES), kf, F32)) - 32768
        t_hi16 = t_hi.astype(I16)

        def tie_tile(kt, acc):
            masked = []
            for ht, lt in zip(_lane_tiles(hi_ref[kt]), _lane_tiles(lo_ref[kt])):
                masked.append(jnp.where(ht == t_hi16, lt, np.int16(-32768)))
                acc = acc + jnp.where(ht > t_hi16, np.int16(1), np.int16(0))
            lo_ref[kt] = jnp.concatenate(masked, axis=-1)
            return acc

        above = lax.fori_loop(0, nkt, tie_tile, jnp.zeros((qb, LANES), I16))
        need_lo = kf - _lane_total(above.astype(F32))
        thr = (t_hi << 16) | _radix4_select(lo_ref, nkt, need_lo)

        def pen_tile(kt, carry):
            n_gt, n_eq = carry
            _, visible, _ = tile_geom(kt)
            pens = []
            for key, vis in zip(_lane_tiles(key_ref[kt]), _lane_tiles(visible)):
                pens.append(jnp.where((key >= thr) & vis, 0.0, NEG_BIG))
                n_gt = n_gt + jnp.where(key > thr, 1.0, 0.0)
                n_eq = n_eq + jnp.where(key == thr, 1.0, 0.0)
            pen_ref[kt] = jnp.concatenate(pens, axis=-1)
            return n_gt, n_eq

        zero = jnp.zeros((qb, LANES), F32)
        n_gt, n_eq = lax.fori_loop(0, nkt, pen_tile, (zero, zero))
        need = kf - _lane_total(n_gt)
        n_eq = _lane_total(n_eq)

        tie_rows = jnp.where((n_eq > need) & (thr > KEY_NEG_INF), 1.0, 0.0)

        @pl.when(jnp.max(tie_rows) > 0.0)
        def _():
            nbits = int(key_ref.shape[0] * tk - 1).bit_length()
            thr1 = thr[:, 0:1]
            need1 = need[:, 0:1]

            def idx_step(i, p):
                trial = p | lax.shift_left(np.int32(1), nbits - 1 - i)

                def count_tile(kt, acc):
                    col, _, _ = tile_geom(kt)
                    hit = (key_ref[kt] == thr1) & (col < trial)
                    return acc + _fold_lanes(jnp.where(hit, 1.0, 0.0))

                acc = lax.fori_loop(0, nkt, count_tile, jnp.zeros((qb, LANES), F32))
                cnt = jnp.sum(acc, axis=-1, keepdims=True)
                return jnp.where(cnt < need1, trial, p)

            last = lax.fori_loop(0, nbits, idx_step, jnp.zeros((qb, 1), I32))

            def repen_tile(kt, carry):
                key = key_ref[kt]
                col, visible, _ = tile_geom(kt)
                sel = (key > thr1) | ((key == thr1) & (col <= last))
                pen_ref[kt] = jnp.where(sel & visible, 0.0, NEG_BIG)
                return carry

            lax.fori_loop(0, nkt, repen_tile, 0)

    qs = (q_ref[...].astype(F32) * ATT_SCALE).astype(BF16)
    qh = [qs[:, h * A_HEAD_DIM:(h + 1) * A_HEAD_DIM] for h in range(A_HEADS)]
    for h in range(A_HEADS):
        m_ref[h] = jnp.full((qb, LANES), NEG_BIG, F32)
        l_ref[h] = jnp.zeros((qb, LANES), F32)
        acc_ref[h] = jnp.zeros((qb, A_HEAD_DIM), F32)

    def logit_tile(kt, carry):
        kk = kv_ref[pl.ds(row_off(kt), tk), 0:A_HEAD_DIM]
        pen = _lane_tiles(pen_ref[kt])
        nd = _lane_tiles(nd_ref[kt])
        for h in range(A_HEADS):
            slope = float(2.0 ** (-8.0 * (h + 1) / A_HEADS))
            s = lax.dot_general(qh[h], kk, _NT, preferred_element_type=F32)
            st = [a + (slope * b + c) for a, b, c in zip(_lane_tiles(s), nd, pen)]
            s_ref[h, kt] = jnp.concatenate(st, axis=-1)
            tile_max = m_ref[h]
            for x in st:
                tile_max = jnp.maximum(tile_max, x)
            m_ref[h] = tile_max
        return carry

    lax.fori_loop(0, nkt, logit_tile, 0)
    for h in range(A_HEADS):
        m_ref[h] = jnp.broadcast_to(jnp.max(m_ref[h], axis=-1, keepdims=True), (qb, LANES))

    def prob_tile(kt, carry):
        vv = kv_ref[pl.ds(row_off(kt), tk), A_HEAD_DIM:2 * A_HEAD_DIM]
        for h in range(A_HEADS):
            m = m_ref[h]
            ps = [jnp.exp(x - m) for x in _lane_tiles(s_ref[h, kt])]
            psum = l_ref[h]
            for x in ps:
                psum = psum + x
            l_ref[h] = psum
            p = jnp.concatenate(ps, axis=-1).astype(BF16)
            acc_ref[h] = acc_ref[h] + jnp.dot(p, vv, preferred_element_type=F32)
        return carry

    lax.fori_loop(0, nkt, prob_tile, 0)
    outs = [acc_ref[h] / jnp.sum(l_ref[h], axis=-1, keepdims=True) for h in range(A_HEADS)]
    o_ref[...] = jnp.concatenate(outs, axis=-1).astype(BF16)


def _attention(q, qi, misc, kv, batch, seq):
    n = q.shape[0]
    qb = min(Q_BLOCK, seq)
    nb = seq // qb
    topk = min(TOPK_MAX, seq // 4)
    assert seq // LANES < (1 << SEL_FIELD_BITS)
    rowq = lambda b, t: (b * nb + t, 0)
    rowk = lambda b, t: (b, 0)
    tiles = (nb, qb, qb)
    return pl.pallas_call(
        functools.partial(_attn_kernel, topk=topk),
        grid=(batch, nb),
        in_specs=[pl.BlockSpec((qb, A_WIDTH), rowq),
                  pl.BlockSpec((qb, IDX_HEADS * IDX_DIM), rowq),
                  pl.BlockSpec((qb, LANES), rowq),
                  pl.BlockSpec((seq, LANES), rowk),
                  pl.BlockSpec((seq, 2 * A_HEAD_DIM), rowk)],
        out_specs=pl.BlockSpec((qb, A_WIDTH), rowq),
        out_shape=jax.ShapeDtypeStruct((n, A_WIDTH), BF16),
        scratch_shapes=[pltpu.VMEM(tiles, I32), pltpu.VMEM(tiles, I16), pltpu.VMEM(tiles, I16),
                        pltpu.VMEM(tiles, F32), pltpu.VMEM(tiles, F32),
                        pltpu.VMEM((IDX_HEADS, qb, LANES), F32),
                        pltpu.VMEM((A_HEADS,) + tiles, F32),
                        pltpu.VMEM((A_HEADS, qb, LANES), F32),
                        pltpu.VMEM((A_HEADS, qb, LANES), F32),
                        pltpu.VMEM((A_HEADS, qb, A_HEAD_DIM), F32)],
        compiler_params=_params("parallel", "parallel"),
        name="dsa_attention",
    )(q, qi, misc, misc, kv)


def _out_proj_kernel(x_ref, a_ref, b_ref, w_ref, mod_ref, o_ref):
    y = jnp.dot(a_ref[...], w_ref[0:A_WIDTH, :], preferred_element_type=F32)
    y = y + jnp.dot(b_ref[...], w_ref[A_WIDTH:, :], preferred_element_type=F32)
    o_ref[...] = x_ref[...] + mod_ref[0, 2:3, :] * y


def _out_proj(x, a, bo, w_o, mod_l, batch, seq):
    n, d = x.shape
    tm = 512
    tps = seq // tm
    row = lambda b, t: (b * tps + t, 0)
    return pl.pallas_call(
        _out_proj_kernel,
        grid=(batch, tps),
        in_specs=[pl.BlockSpec((tm, d), row),
                  pl.BlockSpec((tm, A_WIDTH), row),
                  pl.BlockSpec((tm, B_WIDTH), row),
                  pl.BlockSpec(w_o.shape, lambda b, t: (0, 0)),
                  pl.BlockSpec((1, 6, d), lambda b, t: (b, 0, 0))],
        out_specs=pl.BlockSpec((tm, d), row),
        out_shape=jax.ShapeDtypeStruct((n, d), F32),
        compiler_params=_params("parallel", "parallel"),
        name="out_proj",
    )(x, a, bo, w_o, mod_l)


def _pool_kernel(x_ref, halo_ref, mod_ref, g_ref, w_ref, sc_ref, o_ref):
    tm, d = x_ref.shape
    t = pl.program_id(1)
    g = g_ref[...]
    shift = mod_ref[0, 0:1, :]
    scale = mod_ref[0, 1:2, :]
    x = x_ref[...]
    h = _norm_mod(x, g, shift, scale)
    hh = _norm_mod(halo_ref[...], g, shift, scale)
    hh = jnp.where(t > 0, hh, 0.0)
    ext = jnp.concatenate([hh, h], axis=0)
    pos = t * tm + lax.broadcasted_iota(I32, (tm, 1), 0)
    gd = d // len(POOL_WINDOWS)
    ys = []
    for gi, win in enumerate(POOL_WINDOWS):
        cols = slice(gi * gd, (gi + 1) * gd)
        cur = ext[:, cols]
        k = 1
        while k < win:
            cur = cur + pltpu.roll(cur, k, axis=0)
            k *= 2
        cnt = jnp.minimum(pos + 1, win).astype(F32)
        y = (cur[POOL_HALO:, :] / cnt - h[:, cols]).astype(BF16)
        ys.append(jnp.dot(y, w_ref[gi], preferred_element_type=F32))
    y = jnp.concatenate(ys, axis=-1) * sc_ref[...]
    o_ref[...] = x + mod_ref[0, 2:3, :] * y


def _pool_mixer(x, mod_l, g, w_pool, scale, batch, seq):
    n, d = x.shape
    tm = 512
    tps = seq // tm
    row = lambda b, t: (b * tps + t, 0)
    hpt = tm // POOL_HALO
    halo = lambda b, t: (jnp.maximum((b * tps + t) * hpt - 1, 0), 0)
    return pl.pallas_call(
        _pool_kernel,
        grid=(batch, tps),
        in_specs=[pl.BlockSpec((tm, d), row),
                  pl.BlockSpec((POOL_HALO, d), halo),
                  pl.BlockSpec((1, 6, d), lambda b, t: (b, 0, 0)),
                  pl.BlockSpec((1, d), lambda b, t: (0, 0)),
                  pl.BlockSpec(w_pool.shape, lambda b, t: (0, 0, 0)),
                  pl.BlockSpec((1, d), lambda b, t: (0, 0))],
        out_specs=pl.BlockSpec((tm, d), row),
        out_shape=jax.ShapeDtypeStruct((n, d), F32),
        compiler_params=_params("parallel", "parallel"),
        name="pool_mixer",
    )(x, x, mod_l, g, w_pool, scale)


_R_EXP0 = N_GROUPS


def _router_kernel(x_ref, mod_ref, g_ref, w_ref, b_ref, h_ref, comb_ref):
    tm = x_ref.shape[0]
    h = _norm_mod(x_ref[...], g_ref[...], mod_ref[0, 3:4, :], mod_ref[0, 4:5, :])
    h_ref[...] = h.astype(BF16)
    logits = jnp.dot(h, w_ref[...], precision=lax.Precision.HIGHEST,
                     preferred_element_type=F32) + b_ref[...]
    lane = lax.broadcasted_iota(I32, (tm, LANES), 1)
    big = np.int32(LANES)

    def top1(vals):
        m = jnp.max(vals, axis=-1, keepdims=True)
        idx = jnp.min(jnp.where(vals == m, lane, big), axis=-1, keepdims=True)
        return m, idx

    gl = jnp.where(lane < N_GROUPS, logits, -jnp.inf)
    gmax, gsel = top1(gl)
    p_g = 1.0 / jnp.sum(jnp.exp(gl - gmax), axis=-1, keepdims=True)
    e_lo = _R_EXP0 + gsel * EXPERTS_PER_GROUP
    el = jnp.where((lane >= e_lo) & (lane < e_lo + EXPERTS_PER_GROUP), logits, -jnp.inf)
    v1, i1 = top1(el)
    v2, i2 = top1(jnp.where(lane == i1, -jnp.inf, el))
    e2 = jnp.exp(v2 - v1)
    g1 = p_g / (1.0 + e2)
    g2 = p_g * e2 / (1.0 + e2)
    comb_ref[...] = jnp.where(lane == i1, g1, 0.0) + jnp.where(lane == i2, g2, 0.0)


def _router(x, mod_l, g, w_r, b_r, batch, seq):
    n, d = x.shape
    tm = 512
    tps = seq // tm
    row = lambda b, t: (b * tps + t, 0)
    full = lambda b, t: (0, 0)
    return pl.pallas_call(
        _router_kernel,
        grid=(batch, tps),
        in_specs=[pl.BlockSpec((tm, d), row),
                  pl.BlockSpec((1, 6, d), lambda b, t: (b, 0, 0)),
                  pl.BlockSpec((1, d), full),
                  pl.BlockSpec((d, LANES), full),
                  pl.BlockSpec((1, LANES), full)],
        out_specs=[pl.BlockSpec((tm, d), row), pl.BlockSpec((tm, LANES), row)],
        out_shape=[jax.ShapeDtypeStruct((n, d), BF16), jax.ShapeDtypeStruct((n, LANES), F32)],
        compiler_params=_params("parallel", "parallel"),
        name="router",
    )(x, mod_l, g, w_r, b_r)


def _experts_kernel(x_ref, h_ref, comb_ref, wg_ref, wu_ref, wd_ref, mod_ref, gout_ref, o_ref, acc_ref,
                    *, final_norm):
    e = pl.program_id(1)

    @pl.when(e == 0)
    def _():
        acc_ref[...] = jnp.zeros_like(acc_ref)

    h = h_ref[...]
    hg = jnp.dot(h, wg_ref[0], preferred_element_type=F32)
    hu = jnp.dot(h, wu_ref[0], preferred_element_type=F32)
    lane = lax.broadcasted_iota(I32, comb_ref.shape, 1)
    gate = jnp.sum(jnp.where(lane == e + _R_EXP0, comb_ref[...], 0.0), axis=-1, keepdims=True)
    act = (hg * jax.nn.sigmoid(hg)) * hu * gate
    acc_ref[...] += jnp.dot(act.astype(BF16), wd_ref[0], preferred_element_type=F32)

    @pl.when(e == pl.num_programs(1) - 1)
    def _():
        y = x_ref[...] + mod_ref[0, 5:6, :] * acc_ref[...]
        if final_norm:
            y = y * lax.rsqrt(jnp.mean(y * y, axis=-1, keepdims=True) + EPS) * gout_ref[...]
        o_ref[...] = y


def _experts(x, h, comb, wg, wu, wd, mod_l, g_out, batch, seq, final_norm):
    n, d = x.shape
    ne, _, f = wg.shape
    tm = min(1024, seq)
    tps = seq // tm
    row = lambda i, e: (i, 0)
    return pl.pallas_call(
        functools.partial(_experts_kernel, final_norm=final_norm),
        grid=(n // tm, ne),
        in_specs=[pl.BlockSpec((tm, d), row),
                  pl.BlockSpec((tm, d), row),
                  pl.BlockSpec((tm, LANES), row),
                  pl.BlockSpec((1, d, f), lambda i, e: (e, 0, 0)),
                  pl.BlockSpec((1, d, f), lambda i, e: (e, 0, 0)),
                  pl.BlockSpec((1, f, d), lambda i, e: (e, 0, 0)),
                  pl.BlockSpec((1, 6, d), lambda i, e: (i // tps, 0, 0)),
                  pl.BlockSpec((1, d), lambda i, e: (0, 0))],
        out_specs=pl.BlockSpec((tm, d), row),
        out_shape=jax.ShapeDtypeStruct((n, d), F32),
        scratch_shapes=[pltpu.VMEM((tm, d), F32)],
        compiler_params=_params("parallel", "arbitrary"),
        name="experts",
    )(x, h, comb, wg, wu, wd, mod_l, g_out)


def kernel(x, c, w_mod, b_mod, norm_mix, norm_ffn, w_in, kv_norm, w_kv_up, sgu_norm, w_s, b_s, w_o,
           w_pool, pool_scale, w_gr, b_gr, w_er, b_er, w_gate, w_up, w_down, norm_out):
    batch, seq, d = x.shape
    depth = w_mod.shape[0]
    n = batch * seq
    xs = x.reshape(n, d)
    mod = _modulation(c, w_mod, b_mod).reshape(depth, batch, 6, d)

    pos = np.arange(SGU_CHUNK)
    sgu_mask = jnp.asarray((pos[None, :] // CHUNK) <= (pos[:, None] // CHUNK))
    sizes = (A_WIDTH, KV_LATENT, IDX_HEADS * IDX_DIM, IDX_DIM, IDX_HEADS, B_WIDTH, B_WIDTH)
    offs = np.concatenate([[0], np.cumsum(sizes)])
    pad = LANES - IDX_DIM - IDX_HEADS

    for i in range(depth):
        j = i // 2
        mod_l = mod[i]
        g_mix = norm_mix[i].reshape(1, d)
        if i % 2 == 0:
            wi = w_in[j]
            w_in_r = jnp.concatenate(
                [wi[:, offs[0]:offs[3]], wi[:, offs[3]:offs[5]], jnp.zeros((d, pad), wi.dtype),
                 wi[:, offs[5]:offs[7]]], axis=1).astype(BF16)
            w_s_m = jnp.where(sgu_mask[None], w_s[j], 0.0).astype(BF16)
            b_s_b = jnp.broadcast_to(b_s[j][:, :, None], (B_GROUPS, SGU_CHUNK, B_WIDTH // B_GROUPS))
            q, kv, qi, misc, bo = _in_proj(
                xs, mod_l, g_mix, w_in_r, kv_norm[j].reshape(1, -1), w_kv_up[j].astype(BF16),
                sgu_norm[j].reshape(1, -1), w_s_m, b_s_b, batch, seq)
            a = _attention(q, qi, misc, kv, batch, seq)
            xs = _out_proj(xs, a, bo, w_o[j].astype(BF16), mod_l, batch, seq)
        else:
            xs = _pool_mixer(xs, mod_l, g_mix, w_pool[j].astype(BF16), pool_scale[j].reshape(1, d),
                             batch, seq)
        w_r = jnp.concatenate([w_gr[i], w_er[i],
                               jnp.zeros((d, LANES - N_GROUPS - N_EXPERTS), F32)], axis=1)
        b_r = jnp.concatenate([b_gr[i], b_er[i],
                               jnp.zeros((LANES - N_GROUPS - N_EXPERTS,), F32)]).reshape(1, LANES)
        h, comb = _router(xs, mod_l, norm_ffn[i].reshape(1, d), w_r, b_r, batch, seq)
        xs = _experts(xs, h, comb, w_gate[i].astype(BF16), w_up[i].astype(BF16),
                      w_down[i].astype(BF16), mod_l, norm_out.reshape(1, d), batch, seq,
                      final_norm=(i == depth - 1))
    return xs.reshape(batch, seq, d)
```

```python
import functools

import numpy as np
import jax
import jax.numpy as jnp
from jax import lax
from jax.experimental import pallas as pl
from jax.experimental.pallas import tpu as pltpu

F32 = jnp.float32
BF16 = jnp.bfloat16
I32 = jnp.int32

EPS = 1e-6
CHUNK = 64
CHUNK_SHIFT = 6
Q_BLOCK = 256
A_HEADS = 8
A_HEAD_DIM = 64
A_WIDTH = A_HEADS * A_HEAD_DIM
KV_LATENT = 128
IDX_HEADS = 8
IDX_DIM = 64
TOPK_MAX = 256
B_GROUPS = 4
SGU_CHUNK = 128
POOL_WINDOWS = (2, 4, 8, 16)
POOL_HALO = 16
N_GROUPS = 4
EXPERTS_PER_GROUP = 8
N_EXPERTS = N_GROUPS * EXPERTS_PER_GROUP
LANES = 128
NEG_BIG = -1e30
INT_MIN = np.int32(-2 ** 31)
KEY_NEG_INF = np.int32(np.int32(-8388608) ^ np.int32(0x7FFFFFFF))
VMEM_LIMIT = 56 * 1024 * 1024

_NT = (((1,), (1,)), ((), ()))


def _params(*sem):
    return pltpu.CompilerParams(dimension_semantics=sem, vmem_limit_bytes=VMEM_LIMIT)


def _norm_mod(x, g, shift, scale):
    y = x * lax.rsqrt(jnp.mean(x * x, axis=-1, keepdims=True) + EPS)
    return (y * g) * (1.0 + scale) + shift


def _mod_kernel(c_ref, w_ref, b_ref, o_ref):
    c = c_ref[...]
    cs = c * jax.nn.sigmoid(c)
    o_ref[0] = jnp.dot(cs.astype(BF16), w_ref[0].astype(BF16),
                       preferred_element_type=F32) + b_ref[0]


def _modulation(c, w_mod, b_mod):
    depth, d, d6 = w_mod.shape
    b = c.shape[0]
    tn = 1024
    return pl.pallas_call(
        _mod_kernel,
        grid=(depth, d6 // tn),
        in_specs=[pl.BlockSpec((b, d), lambda i, j: (0, 0)),
                  pl.BlockSpec((1, d, tn), lambda i, j: (i, 0, j)),
                  pl.BlockSpec((1, 1, tn), lambda i, j: (i, 0, j))],
        out_specs=pl.BlockSpec((1, b, tn), lambda i, j: (i, 0, j)),
        out_shape=jax.ShapeDtypeStruct((depth, b, d6), F32),
        compiler_params=_params("parallel", "parallel"),
        name="modulation",
    )(c, w_mod, b_mod.reshape(depth, 1, d6))


_C_Q = 0
_C_KV = _C_Q + A_WIDTH
_C_QI = _C_KV + KV_LATENT
_C_MISC = _C_QI + IDX_HEADS * IDX_DIM
_C_U = _C_MISC + LANES
B_WIDTH = 512
_C_V = _C_U + B_WIDTH
_C_END = _C_V + B_WIDTH


def _in_proj_kernel(x_ref, mod_ref, g_ref, w_ref, kvn_ref, wkv_ref, lng_ref, ws_ref, bs_ref,
                    q_ref, kv_ref, qi_ref, misc_ref, b_ref):
    tm = x_ref.shape[0]
    h = _norm_mod(x_ref[...], g_ref[...], mod_ref[0, 0:1, :], mod_ref[0, 1:2, :]).astype(BF16)

    def proj(lo, hi):
        return jnp.dot(h, w_ref[:, lo:hi], preferred_element_type=F32)

    q_ref[...] = proj(_C_Q, _C_KV).astype(BF16)
    qi_ref[...] = proj(_C_QI, _C_MISC).astype(BF16)

    kvl = proj(_C_KV, _C_QI)
    kvn = kvl * lax.rsqrt(jnp.mean(kvl * kvl, axis=-1, keepdims=True) + EPS) * kvn_ref[...]
    kv_ref[...] = jnp.dot(kvn.astype(BF16), wkv_ref[...], preferred_element_type=F32).astype(BF16)

    lane = lax.broadcasted_iota(I32, (1, LANES), 1)
    idx_scale = (IDX_HEADS ** -0.5) * (IDX_DIM ** -0.5)
    w_lane = (lane >= IDX_DIM) & (lane < IDX_DIM + IDX_HEADS)
    misc_ref[...] = proj(_C_MISC, _C_U) * jnp.where(w_lane, idx_scale, 1.0)

    gu = jax.nn.gelu(proj(_C_U, _C_V))
    gv = jax.nn.gelu(proj(_C_V, _C_END))
    mu = jnp.mean(gv, axis=-1, keepdims=True)
    cen = gv - mu
    var = jnp.mean(cen * cen, axis=-1, keepdims=True)
    vn = (cen * lax.rsqrt(var + EPS) * lng_ref[...]).astype(BF16)
    gd = B_WIDTH // B_GROUPS
    for r in range(tm // SGU_CHUNK):
        rows = slice(r * SGU_CHUNK, (r + 1) * SGU_CHUNK)
        for g in range(B_GROUPS):
            cols = slice(g * gd, (g + 1) * gd)
            mixed = jnp.dot(ws_ref[g], vn[rows, cols], preferred_element_type=F32) + bs_ref[g]
            b_ref[rows, cols] = (gu[rows, cols] * mixed).astype(BF16)


def _in_proj(x, mod_l, g, w_in_r, kv_norm, w_kv_up, sgu_norm, w_s_m, b_s_b, batch, seq):
    n, d = x.shape
    tm = 512
    tps = seq // tm
    row = lambda b, t: (b * tps + t, 0)
    full2 = lambda b, t: (0, 0)
    full3 = lambda b, t: (0, 0, 0)
    outs = [(A_WIDTH, BF16), (2 * A_HEAD_DIM, BF16), (IDX_HEADS * IDX_DIM, BF16), (LANES, F32),
            (B_WIDTH, BF16)]
    return pl.pallas_call(
        _in_proj_kernel,
        grid=(batch, tps),
        in_specs=[pl.BlockSpec((tm, d), row),
                  pl.BlockSpec((1, 6, d), lambda b, t: (b, 0, 0)),
                  pl.BlockSpec((1, d), full2),
                  pl.BlockSpec(w_in_r.shape, full2),
                  pl.BlockSpec((1, KV_LATENT), full2),
                  pl.BlockSpec(w_kv_up.shape, full2),
                  pl.BlockSpec((1, B_WIDTH), full2),
                  pl.BlockSpec(w_s_m.shape, full3),
                  pl.BlockSpec(b_s_b.shape, full3)],
        out_specs=[pl.BlockSpec((tm, w), row) for w, _ in outs],
        out_shape=[jax.ShapeDtypeStruct((n, w), dt) for w, dt in outs],
        compiler_params=_params("parallel", "parallel"),
        name="in_proj_sgu",
    )(x, mod_l, g, w_in_r, kv_norm, w_kv_up, sgu_norm, w_s_m, b_s_b)


I16 = jnp.int16
SEL_FIELD_BITS = 5
ATT_SCALE = A_HEAD_DIM ** -0.5
assert ATT_SCALE == 2.0 ** round(np.log2(ATT_SCALE)), "folded into bf16 q, must be a power of two"


def _lane_tiles(x):
    return [x[:, j * LANES:(j + 1) * LANES] for j in range(x.shape[1] // LANES)]


def _fold_lanes(x):
    tiles = _lane_tiles(x)
    acc = tiles[0]
    for tile in tiles[1:]:
        acc = acc + tile
    return acc


def _lane_total(x):
    return jnp.dot(x.astype(BF16), jnp.ones((LANES, LANES), BF16), preferred_element_type=F32)


def _radix4_select(tiles_ref, nkt, kneed):
    _, qb, _ = tiles_ref.shape
    c1 = 1
    c2 = c1 + (1 << SEL_FIELD_BITS)
    c3 = c2 + (1 << (2 * SEL_FIELD_BITS))
    fmask = (1 << SEL_FIELD_BITS) - 1

    def step(i, prefix):
        unit = lax.shift_left(np.int32(1), 14 - 2 * i)
        ts = [(prefix + (j * unit - 32768)).astype(I16) for j in (1, 2, 3)]

        def count_tile(kt, acc):
            for vt in _lane_tiles(tiles_ref[kt]):
                acc = acc + jnp.where(
                    vt >= ts[2], np.int16(c3),
                    jnp.where(vt >= ts[1], np.int16(c2),
                              jnp.where(vt >= ts[0], np.int16(c1), np.int16(0))))
            return acc

        acc = lax.fori_loop(0, nkt, count_tile, jnp.zeros((qb, LANES), I16)).astype(I32)
        cnt = [_lane_total(((acc >> (j * SEL_FIELD_BITS)) & fmask).astype(F32)) for j in range(3)]
        digit = jnp.where(cnt[2] >= kneed, 3,
                          jnp.where(cnt[1] >= kneed, 2, jnp.where(cnt[0] >= kneed, 1, 0)))
        return prefix + digit * unit

    return lax.fori_loop(0, 8, step, jnp.zeros((qb, LANES), I32))


def _attn_kernel(q_ref, qi_ref, mq_ref, mk_ref, kv_ref, o_ref,
                 key_ref, hi_ref, lo_ref, pen_ref, nd_ref, w_ref, s_ref, m_ref, l_ref, acc_ref, *, topk):
    _, qb, tk = key_ref.shape
    t = pl.program_id(1)
    nkt = t + 1
    kf = float(topk)
    row = lax.broadcasted_iota(I32, (qb, tk), 0)
    lane = lax.broadcasted_iota(I32, (qb, tk), 1)
    qchunk = (t * qb + row) >> CHUNK_SHIFT
    rel = row - lane

    def tile_geom(kt):
        col = kt * tk + lane
        visible = (col >> CHUNK_SHIFT) <= qchunk
        neg_dist = -jnp.abs(rel + (t - kt) * tk).astype(F32)
        return col, visible, neg_dist

    def row_off(kt):
        return pl.multiple_of(kt * tk, tk)

    @pl.when(nkt * tk <= topk)
    def _():
        def fill(kt, carry):
            _, visible, neg_dist = tile_geom(kt)
            pen_ref[kt] = jnp.where(visible, 0.0, NEG_BIG)
            nd_ref[kt] = neg_dist
            return carry

        lax.fori_loop(0, nkt, fill, 0)

    @pl.when(nkt * tk > topk)
    def _():
        qi = qi_ref[...]
        mq = mq_ref[...]
        qih = [qi[:, h * IDX_DIM:(h + 1) * IDX_DIM] for h in range(IDX_HEADS)]
        for h in range(IDX_HEADS):
            w_ref[h] = jnp.broadcast_to(mq[:, IDX_DIM + h:IDX_DIM + h + 1], (qb, LANES))

        def score_tile(kt, carry):
            k_idx = mk_ref[pl.ds(row_off(kt), tk), 0:IDX_DIM].astype(BF16)
            parts = [jnp.zeros((qb, LANES), F32)] * (tk // LANES)
            for h in range(IDX_HEADS):
                logits = lax.dot_general(qih[h], k_idx, _NT, preferred_element_type=F32)
                w = w_ref[h]
                parts = [p + w * jnp.maximum(lt, 0.0) for p, lt in zip(parts, _lane_tiles(logits))]
            _, visible, neg_dist = tile_geom(kt)
            score = jnp.where(visible, jnp.concatenate(parts, axis=-1), -jnp.inf)
            bits = pltpu.bitcast(score, I32)
            key = bits ^ ((bits >> 31) & np.int32(0x7FFFFFFF))
            key_ref[kt] = key
            hi_ref[kt] = (key >> 16).astype(I16)
            lo_ref[kt] = ((key & 0xFFFF) - 32768).astype(I16)
            nd_ref[kt] = neg_dist
            return carry

        lax.fori_loop(0, nkt, score_tile, 0)

        t_hi = _radix4_select(hi_ref, nkt, jnp.full((qb, LANES), kf, F32)) - 32768
        t_hi16 = t_hi.astype(I16)

        def tie_tile(kt, acc):
            masked = []
            for ht, lt in zip(_lane_tiles(hi_ref[kt]), _lane_tiles(lo_ref[kt])):
                masked.append(jnp.where(ht == t_hi16, lt, np.int16(-32768)))
                acc = acc + jnp.where(ht > t_hi16, np.int16(1), np.int16(0))
            lo_ref[kt] = jnp.concatenate(masked, axis=-1)
            return acc

        above = lax.fori_loop(0, nkt, tie_tile, jnp.zeros((qb, LANES), I16))
        need_lo = kf - _lane_total(above.astype(F32))
        thr = (t_hi << 16) | _radix4_select(lo_ref, nkt, need_lo)

        def pen_tile(kt, carry):
            n_gt, n_eq = carry
            _, visible, _ = tile_geom(kt)
            pens = []
            for key, vis in zip(_lane_tiles(key_ref[kt]), _lane_tiles(visible)):
                pens.append(jnp.where((key >= thr) & vis, 0.0, NEG_BIG))
                n_gt = n_gt + jnp.where(key > thr, 1.0, 0.0)
                n_eq = n_eq + jnp.where(key == thr, 1.0, 0.0)
            pen_ref[kt] = jnp.concatenate(pens, axis=-1)
            return n_gt, n_eq

        zero = jnp.zeros((qb, LANES), F32)
        n_gt, n_eq = lax.fori_loop(0, nkt, pen_tile, (zero, zero))
        need = kf - _lane_total(n_gt)
        n_eq = _lane_total(n_eq)

        tie_rows = jnp.where((n_eq > need) & (thr > KEY_NEG_INF), 1.0, 0.0)

        @pl.when(jnp.max(tie_rows) > 0.0)
        def _():
            nbits = int(key_ref.shape[0] * tk - 1).bit_length()
            thr1 = thr[:, 0:1]
            need1 = need[:, 0:1]

            def idx_step(i, p):
                trial = p | lax.shift_left(np.int32(1), nbits - 1 - i)

                def count_tile(kt, acc):
                    col, _, _ = tile_geom(kt)
                    hit = (key_ref[kt] == thr1) & (col < trial)
                    return acc + _fold_lanes(jnp.where(hit, 1.0, 0.0))

                acc = lax.fori_loop(0, nkt, count_tile, jnp.zeros((qb, LANES), F32))
                cnt = jnp.sum(acc, axis=-1, keepdims=True)
                return jnp.where(cnt < need1, trial, p)

            last = lax.fori_loop(0, nbits, idx_step, jnp.zeros((qb, 1), I32))

            def repen_tile(kt, carry):
                key = key_ref[kt]
                col, visible, _ = tile_geom(kt)
                sel = (key > thr1) | ((key == thr1) & (col <= last))
                pen_ref[kt] = jnp.where(sel & visible, 0.0, NEG_BIG)
                return carry

            lax.fori_loop(0, nkt, repen_tile, 0)

    qs = (q_ref[...].astype(F32) * ATT_SCALE).astype(BF16)
    qh = [qs[:, h * A_HEAD_DIM:(h + 1) * A_HEAD_DIM] for h in range(A_HEADS)]
    for h in range(A_HEADS):
        m_ref[h] = jnp.full((qb, LANES), NEG_BIG, F32)
        l_ref[h] = jnp.zeros((qb, LANES), F32)
        acc_ref[h] = jnp.zeros((qb, A_HEAD_DIM), F32)

    def logit_tile(kt, carry):
        kk = kv_ref[pl.ds(row_off(kt), tk), 0:A_HEAD_DIM]
        pen = _lane_tiles(pen_ref[kt])
        nd = _lane_tiles(nd_ref[kt])
        for h in range(A_HEADS):
            slope = float(2.0 ** (-8.0 * (h + 1) / A_HEADS))
            s = lax.dot_general(qh[h], kk, _NT, preferred_element_type=F32)
            st = [a + (slope * b + c) for a, b, c in zip(_lane_tiles(s), nd, pen)]
            s_ref[h, kt] = jnp.concatenate(st, axis=-1)
            tile_max = m_ref[h]
            for x in st:
                tile_max = jnp.maximum(tile_max, x)
            m_ref[h] = tile_max
        return carry

    lax.fori_loop(0, nkt, logit_tile, 0)
    for h in range(A_HEADS):
        m_ref[h] = jnp.broadcast_to(jnp.max(m_ref[h], axis=-1, keepdims=True), (qb, LANES))

    def prob_tile(kt, carry):
        vv = kv_ref[pl.ds(row_off(kt), tk), A_HEAD_DIM:2 * A_HEAD_DIM]
        for h in range(A_HEADS):
            m = m_ref[h]
            ps = [jnp.exp(x - m) for x in _lane_tiles(s_ref[h, kt])]
            psum = l_ref[h]
            for x in ps:
                psum = psum + x
            l_ref[h] = psum
            p = jnp.concatenate(ps, axis=-1).astype(BF16)
            acc_ref[h] = acc_ref[h] + jnp.dot(p, vv, preferred_element_type=F32)
        return carry

    lax.fori_loop(0, nkt, prob_tile, 0)
    outs = [acc_ref[h] / jnp.sum(l_ref[h], axis=-1, keepdims=True) for h in range(A_HEADS)]
    o_ref[...] = jnp.concatenate(outs, axis=-1).astype(BF16)


def _attention(q, qi, misc, kv, batch, seq):
    n = q.shape[0]
    qb = min(Q_BLOCK, seq)
    nb = seq // qb
    topk = min(TOPK_MAX, seq // 4)
    assert seq // LANES < (1 << SEL_FIELD_BITS)
    rowq = lambda b, t: (b * nb + t, 0)
    rowk = lambda b, t: (b, 0)
    tiles = (nb, qb, qb)
    return pl.pallas_call(
        functools.partial(_attn_kernel, topk=topk),
        grid=(batch, nb),
        in_specs=[pl.BlockSpec((qb, A_WIDTH), rowq),
                  pl.BlockSpec((qb, IDX_HEADS * IDX_DIM), rowq),
                  pl.BlockSpec((qb, LANES), rowq),
                  pl.BlockSpec((seq, LANES), rowk),
                  pl.BlockSpec((seq, 2 * A_HEAD_DIM), rowk)],
        out_specs=pl.BlockSpec((qb, A_WIDTH), rowq),
        out_shape=jax.ShapeDtypeStruct((n, A_WIDTH), BF16),
        scratch_shapes=[pltpu.VMEM(tiles, I32), pltpu.VMEM(tiles, I16), pltpu.VMEM(tiles, I16),
                        pltpu.VMEM(tiles, F32), pltpu.VMEM(tiles, F32),
                        pltpu.VMEM((IDX_HEADS, qb, LANES), F32),
                        pltpu.VMEM((A_HEADS,) + tiles, F32),
                        pltpu.VMEM((A_HEADS, qb, LANES), F32),
                        pltpu.VMEM((A_HEADS, qb, LANES), F32),
                        pltpu.VMEM((A_HEADS, qb, A_HEAD_DIM), F32)],
        compiler_params=_params("parallel", "parallel"),
        name="dsa_attention",
    )(q, qi, misc, misc, kv)


def _out_proj_kernel(x_ref, a_ref, b_ref, w_ref, mod_ref, o_ref):
    y = jnp.dot(a_ref[...], w_ref[0:A_WIDTH, :], preferred_element_type=F32)
    y = y + jnp.dot(b_ref[...], w_ref[A_WIDTH:, :], preferred_element_type=F32)
    o_ref[...] = x_ref[...] + mod_ref[0, 2:3, :] * y


def _out_proj(x, a, bo, w_o, mod_l, batch, seq):
    n, d = x.shape
    tm = 512
    tps = seq // tm
    row = lambda b, t: (b * tps + t, 0)
    return pl.pallas_call(
        _out_proj_kernel,
        grid=(batch, tps),
        in_specs=[pl.BlockSpec((tm, d), row),
                  pl.BlockSpec((tm, A_WIDTH), row),
                  pl.BlockSpec((tm, B_WIDTH), row),
                  pl.BlockSpec(w_o.shape, lambda b, t: (0, 0)),
                  pl.BlockSpec((1, 6, d), lambda b, t: (b, 0, 0))],
        out_specs=pl.BlockSpec((tm, d), row),
        out_shape=jax.ShapeDtypeStruct((n, d), F32),
        compiler_params=_params("parallel", "parallel"),
        name="out_proj",
    )(x, a, bo, w_o, mod_l)


def _pool_kernel(x_ref, halo_ref, mod_ref, g_ref, w_ref, sc_ref, o_ref):
    tm, d = x_ref.shape
    t = pl.program_id(1)
    g = g_ref[...]
    shift = mod_ref[0, 0:1, :]
    scale = mod_ref[0, 1:2, :]
    x = x_ref[...]
    h = _norm_mod(x, g, shift, scale)
    hh = _norm_mod(halo_ref[...], g, shift, scale)
    hh = jnp.where(t > 0, hh, 0.0)
    ext = jnp.concatenate([hh, h], axis=0)
    pos = t * tm + lax.broadcasted_iota(I32, (tm, 1), 0)
    gd = d // len(POOL_WINDOWS)
    ys = []
    for gi, win in enumerate(POOL_WINDOWS):
        cols = slice(gi * gd, (gi + 1) * gd)
        cur = ext[:, cols]
        k = 1
        while k < win:
            cur = cur + pltpu.roll(cur, k, axis=0)
            k *= 2
        cnt = jnp.minimum(pos + 1, win).astype(F32)
        y = (cur[POOL_HALO:, :] / cnt - h[:, cols]).astype(BF16)
        ys.append(jnp.dot(y, w_ref[gi], preferred_element_type=F32))
    y = jnp.concatenate(ys, axis=-1) * sc_ref[...]
    o_ref[...] = x + mod_ref[0, 2:3, :] * y


def _pool_mixer(x, mod_l, g, w_pool, scale, batch, seq):
    n, d = x.shape
    tm = 512
    tps = seq // tm
    row = lambda b, t: (b * tps + t, 0)
    hpt = tm // POOL_HALO
    halo = lambda b, t: (jnp.maximum((b * tps + t) * hpt - 1, 0), 0)
    return pl.pallas_call(
        _pool_kernel,
        grid=(batch, tps),
        in_specs=[pl.BlockSpec((tm, d), row),
                  pl.BlockSpec((POOL_HALO, d), halo),
                  pl.BlockSpec((1, 6, d), lambda b, t: (b, 0, 0)),
                  pl.BlockSpec((1, d), lambda b, t: (0, 0)),
                  pl.BlockSpec(w_pool.shape, lambda b, t: (0, 0, 0)),
                  pl.BlockSpec((1, d), lambda b, t: (0, 0))],
        out_specs=pl.BlockSpec((tm, d), row),
        out_shape=jax.ShapeDtypeStruct((n, d), F32),
        compiler_params=_params("parallel", "parallel"),
        name="pool_mixer",
    )(x, x, mod_l, g, w_pool, scale)


_R_EXP0 = N_GROUPS
RT_TM = 512
SEG_ALIGN = 16
SEG_SHIFT = 4
E_TM = 1152

_TN = (((0,), (0,)), ((), ()))


def _moe_layout(n):
    tm = min(RT_TM, n)
    xs_rows = tm + N_GROUPS * SEG_ALIGN
    e_tm = E_TM if n >= 4 * E_TM else 128
    worst = n + (n // tm) * N_GROUPS * SEG_ALIGN + N_GROUPS * e_tm
    rows = -(-worst // e_tm) * e_tm
    return tm, xs_rows, e_tm, rows


def _perm_matrix(d_row, rows):
    tm = d_row.shape[1]
    return jnp.where(lax.broadcasted_iota(I32, (rows, tm), 0) == d_row, 1.0, 0.0).astype(BF16)


def _split_bf16(a):
    hi = a.astype(BF16)
    return hi, (a - hi.astype(F32)).astype(BF16)


def _route_kernel(x_ref, mod_ref, g_ref, w_ref, b_ref, tri_ref, xs_ref, gs_ref, d_ref, cnt_ref):
    tm = x_ref.shape[0]
    xs_rows = xs_ref.shape[0]
    h = _norm_mod(x_ref[...], g_ref[...], mod_ref[0, 3:4, :], mod_ref[0, 4:5, :])
    logits = jnp.dot(h, w_ref[...], precision=lax.Precision.HIGHEST,
                     preferred_element_type=F32) + b_ref[...]
    lane = lax.broadcasted_iota(I32, (tm, LANES), 1)
    big = np.int32(LANES)

    def top1(vals):
        m = jnp.max(vals, axis=-1, keepdims=True)
        idx = jnp.min(jnp.where(vals == m, lane, big), axis=-1, keepdims=True)
        return m, idx

    gl = jnp.where(lane < N_GROUPS, logits, -jnp.inf)
    gmax, gsel = top1(gl)
    p_g = 1.0 / jnp.sum(jnp.exp(gl - gmax), axis=-1, keepdims=True)
    e_lo = _R_EXP0 + gsel * EXPERTS_PER_GROUP
    el = jnp.where((lane >= e_lo) & (lane < e_lo + EXPERTS_PER_GROUP), logits, -jnp.inf)
    v1, i1 = top1(el)
    v2, i2 = top1(jnp.where(lane == i1, -jnp.inf, el))
    e2 = jnp.exp(v2 - v1)
    g1 = p_g / (1.0 + e2)
    g2 = p_g * e2 / (1.0 + e2)
    cg = jnp.where(lane == i1 - e_lo, g1, 0.0) + jnp.where(lane == i2 - e_lo, g2, 0.0)

    member = jnp.where(lane == gsel, 1.0, 0.0).T[0:8, :]
    rank = jnp.dot(member.astype(BF16), tri_ref[...], preferred_element_type=F32)
    count = jnp.sum(member, axis=1, keepdims=True)
    cnt_ref[0] = jnp.broadcast_to(count, (8, LANES))
    padded = jnp.ceil(count * (1.0 / SEG_ALIGN)) * SEG_ALIGN
    d_row = jnp.zeros((1, tm), F32)
    seg = jnp.zeros((1, 1), F32)
    for g in range(N_GROUPS):
        d_row = d_row + member[g:g + 1, :] * (rank[g:g + 1, :] + seg)
        seg = seg + padded[g:g + 1, :]
    d_row = d_row.astype(I32)
    d_ref[0] = d_row

    perm = _perm_matrix(d_row, xs_rows)
    xs_ref[...] = jnp.dot(perm, h.astype(BF16), preferred_element_type=F32).astype(BF16)
    cg_hi, cg_lo = _split_bf16(cg)
    gs_ref[...] = (jnp.dot(perm, cg_hi, preferred_element_type=F32)
                   + jnp.dot(perm, cg_lo, preferred_element_type=F32))


def _route(x, mod_l, g, w_r, b_r, tri, batch, seq):
    n, d = x.shape
    tm, xs_rows, _, _ = _moe_layout(n)
    nt = n // tm
    tps = seq // tm
    full = lambda i: (0, 0)
    return pl.pallas_call(
        _route_kernel,
        grid=(nt,),
        in_specs=[pl.BlockSpec((tm, d), lambda i: (i, 0)),
                  pl.BlockSpec((1, 6, d), lambda i: (i // tps, 0, 0)),
                  pl.BlockSpec((1, d), full),
                  pl.BlockSpec((d, LANES), full),
                  pl.BlockSpec((1, LANES), full),
                  pl.BlockSpec((tm, tm), full)],
        out_specs=[pl.BlockSpec((xs_rows, d), lambda i: (i, 0)),
                   pl.BlockSpec((xs_rows, LANES), lambda i: (i, 0)),
                   pl.BlockSpec((1, 1, tm), lambda i: (i, 0, 0)),
                   pl.BlockSpec((1, 8, LANES), lambda i: (i, 0, 0))],
        out_shape=[jax.ShapeDtypeStruct((nt * xs_rows, d), BF16),
                   jax.ShapeDtypeStruct((nt * xs_rows, LANES), F32),
                   jax.ShapeDtypeStruct((nt, 1, tm), I32),
                   jax.ShapeDtypeStruct((nt, 8, LANES), F32)],
        compiler_params=_params("parallel"),
        name="route_sort",
    )(x, mod_l, g, w_r, b_r, tri)


def _dispatch_kernel(src_ref, dst_ref, len_ref, fill_ref, xt_hbm, gt_hbm, xs_hbm, gs_hbm,
                     zx_ref, zg_ref, sem_x, sem_g):
    i = pl.program_id(0)
    nt = src_ref.shape[0]

    def chunk_copies(x_src, g_src, src_row, dst_row):
        src_row = pl.multiple_of(src_row, SEG_ALIGN)
        dst_row = pl.multiple_of(dst_row, SEG_ALIGN)
        return (pltpu.make_async_copy(x_src.at[pl.ds(src_row, SEG_ALIGN)],
                                      xs_hbm.at[pl.ds(dst_row, SEG_ALIGN)], sem_x),
                pltpu.make_async_copy(g_src.at[pl.ds(src_row, SEG_ALIGN)],
                                      gs_hbm.at[pl.ds(dst_row, SEG_ALIGN)], sem_g))

    def run(x_src, g_src, jobs):
        total = jnp.int32(0)
        for src, dst, rows in jobs:
            nch = rows >> SEG_SHIFT

            def issue(k, carry, src=src, dst=dst):
                for cp in chunk_copies(x_src, g_src, src + k * SEG_ALIGN, dst + k * SEG_ALIGN):
                    cp.start()
                return carry

            lax.fori_loop(0, nch, issue, 0)
            total = total + nch

        def drain(k, carry):
            for cp in chunk_copies(x_src, g_src, 0, 0):
                cp.wait()
            return carry

        lax.fori_loop(0, total, drain, 0)

    @pl.when(i < nt)
    def _():
        t = jnp.minimum(i, nt - 1)
        run(xt_hbm, gt_hbm, [(src_ref[t, g], dst_ref[t, g], len_ref[t, g]) for g in range(N_GROUPS)])

    @pl.when(i == nt)
    def _():
        zx_ref[...] = jnp.zeros_like(zx_ref)
        zg_ref[...] = jnp.zeros_like(zg_ref)
        for g in range(N_GROUPS + 1):
            start = fill_ref[0, g]
            nch = fill_ref[1, g] >> SEG_SHIFT

            def issue(k, carry, start=start):
                for cp in chunk_copies(zx_ref, zg_ref, 0, start + k * SEG_ALIGN):
                    cp.start()
                return carry

            lax.fori_loop(0, nch, issue, 0)

            def drain(k, carry):
                for cp in chunk_copies(zx_ref, zg_ref, 0, 0):
                    cp.wait()
                return carry

            lax.fori_loop(0, nch, drain, 0)


def _dispatch(xt, gt, src_off, dst_off, seg_len, fill, rows):
    d = xt.shape[1]
    nt = src_off.shape[0]
    return pl.pallas_call(
        _dispatch_kernel,
        grid_spec=pltpu.PrefetchScalarGridSpec(
            num_scalar_prefetch=4,
            grid=(nt + 1,),
            in_specs=[pl.BlockSpec(memory_space=pl.ANY), pl.BlockSpec(memory_space=pl.ANY)],
            out_specs=[pl.BlockSpec(memory_space=pl.ANY), pl.BlockSpec(memory_space=pl.ANY)],
            scratch_shapes=[pltpu.VMEM((SEG_ALIGN, d), BF16), pltpu.VMEM((SEG_ALIGN, LANES), F32),
                            pltpu.SemaphoreType.DMA, pltpu.SemaphoreType.DMA]),
        out_shape=[jax.ShapeDtypeStruct((rows, d), BF16),
                   jax.ShapeDtypeStruct((rows, LANES), F32)],
        compiler_params=_params("arbitrary"),
        name="dispatch",
    )(src_off, dst_off, seg_len, fill, xt, gt)


def _experts_kernel(grp_ref, valid_ref, x_ref, gs_ref, wg_ref, wu_ref, wd_ref, o_ref, acc_ref):
    j = pl.program_id(0)
    e = pl.program_id(1)

    @pl.when(e == 0)
    def _():
        acc_ref[...] = jnp.zeros_like(acc_ref)

    @pl.when(valid_ref[j] > 0)
    def _():
        x = x_ref[...]
        hg = jnp.dot(x, wg_ref[0], preferred_element_type=F32)
        hu = jnp.dot(x, wu_ref[0], preferred_element_type=F32)
        lane = lax.broadcasted_iota(I32, gs_ref.shape, 1)
        gate = jnp.sum(jnp.where(lane == e, gs_ref[...], 0.0), axis=-1, keepdims=True)
        act = (hg * jax.nn.sigmoid(hg)) * hu * gate
        acc_ref[...] += jnp.dot(act.astype(BF16), wd_ref[0], preferred_element_type=F32)

    @pl.when(e == pl.num_programs(1) - 1)
    def _():
        o_ref[...] = acc_ref[...]


def _experts(xs, gs, tile_grp, tile_valid, wg, wu, wd, e_tm):
    rows, d = xs.shape
    _, _, f = wg.shape
    ntile = tile_grp.shape[0]
    row = lambda j, e, grp, valid: (j, 0)
    wsel = lambda j, e, grp, valid: (
        grp[j] * EXPERTS_PER_GROUP + jnp.where(valid[j] > 0, e, EXPERTS_PER_GROUP - 1), 0, 0)
    return pl.pallas_call(
        _experts_kernel,
        grid_spec=pltpu.PrefetchScalarGridSpec(
            num_scalar_prefetch=2,
            grid=(ntile, EXPERTS_PER_GROUP),
            in_specs=[pl.BlockSpec((e_tm, d), row),
                      pl.BlockSpec((e_tm, LANES), row),
                      pl.BlockSpec((1, d, f), wsel),
                      pl.BlockSpec((1, d, f), wsel),
                      pl.BlockSpec((1, f, d), wsel)],
            out_specs=pl.BlockSpec((e_tm, d), row),
            scratch_shapes=[pltpu.VMEM((e_tm, d), F32)]),
        out_shape=jax.ShapeDtypeStruct((rows, d), F32),
        compiler_params=_params("arbitrary", "arbitrary"),
        name="experts",
    )(tile_grp, tile_valid, xs, gs, wg, wu, wd)


def _combine_kernel(meta_ref, x_ref, d_ref, mod_ref, gout_ref, ys_hbm, o_ref, yseg, sem, *, final_norm):
    i = pl.program_id(0)
    xs_rows = yseg.shape[0]
    yseg[...] = jnp.zeros_like(yseg)

    def seg_copy(src_row, dst_row):
        src_row = pl.multiple_of(src_row, SEG_ALIGN)
        dst_row = pl.multiple_of(dst_row, SEG_ALIGN)
        return pltpu.make_async_copy(ys_hbm.at[pl.ds(src_row, SEG_ALIGN)],
                                     yseg.at[pl.ds(dst_row, SEG_ALIGN)], sem)

    seg = jnp.int32(0)
    total_chunks = jnp.int32(0)
    for g in range(N_GROUPS):
        base = meta_ref[i, g]
        c16 = meta_ref[i, N_GROUPS + g]
        nch = c16 >> SEG_SHIFT

        def issue(k, carry, seg=seg, base=base):
            seg_copy(base + k * SEG_ALIGN, seg + k * SEG_ALIGN).start()
            return carry

        lax.fori_loop(0, nch, issue, 0)
        seg = seg + c16
        total_chunks = total_chunks + nch

    def drain(k, carry):
        seg_copy(0, 0).wait()
        return carry

    lax.fori_loop(0, total_chunks, drain, 0)

    perm = _perm_matrix(d_ref[0], xs_rows)
    y_hi, y_lo = _split_bf16(yseg[...])
    y = (lax.dot_general(perm, y_hi, _TN, preferred_element_type=F32)
         + lax.dot_general(perm, y_lo, _TN, preferred_element_type=F32))
    out = x_ref[...] + mod_ref[0, 5:6, :] * y
    if final_norm:
        out = out * lax.rsqrt(jnp.mean(out * out, axis=-1, keepdims=True) + EPS) * gout_ref[...]
    o_ref[...] = out


def _combine(x, d_rows, meta, ys, mod_l, g_out, batch, seq, final_norm):
    n, d = x.shape
    tm, xs_rows, _, _ = _moe_layout(n)
    tps = seq // tm
    return pl.pallas_call(
        functools.partial(_combine_kernel, final_norm=final_norm),
        grid_spec=pltpu.PrefetchScalarGridSpec(
            num_scalar_prefetch=1,
            grid=(n // tm,),
            in_specs=[pl.BlockSpec((tm, d), lambda i, meta: (i, 0)),
                      pl.BlockSpec((1, 1, tm), lambda i, meta: (i, 0, 0)),
                      pl.BlockSpec((1, 6, d), lambda i, meta: (i // tps, 0, 0)),
                      pl.BlockSpec((1, d), lambda i, meta: (0, 0)),
                      pl.BlockSpec(memory_space=pl.ANY)],
            out_specs=pl.BlockSpec((tm, d), lambda i, meta: (i, 0)),
            scratch_shapes=[pltpu.VMEM((xs_rows, d), F32), pltpu.SemaphoreType.DMA]),
        out_shape=jax.ShapeDtypeStruct((n, d), F32),
        compiler_params=_params("arbitrary"),
        name="combine",
    )(meta, x, d_rows, mod_l, g_out, ys)


def _moe_plan(cnt, n):
    tm, xs_rows, e_tm, rows = _moe_layout(n)
    nt = cnt.shape[0]
    seg_len = ((cnt + (SEG_ALIGN - 1)) // SEG_ALIGN) * SEG_ALIGN
    totals = jnp.sum(seg_len, axis=0)
    region = ((totals + (e_tm - 1)) // e_tm) * e_tm
    region_end = jnp.cumsum(region)
    region_start = region_end - region
    src_off = (jnp.arange(nt, dtype=I32)[:, None] * xs_rows
               + jnp.cumsum(seg_len, axis=1) - seg_len)
    dst_off = region_start[None, :] + jnp.cumsum(seg_len, axis=0) - seg_len
    fill_start = jnp.concatenate([region_start + totals, region_end[-1:]])
    fill_len = jnp.concatenate([region - totals, rows - region_end[-1:]])
    fill = jnp.stack([fill_start, fill_len]).astype(I32)
    j = jnp.arange(rows // e_tm, dtype=I32) * e_tm
    valid = j < region_end[-1]
    grp = jnp.minimum(jnp.sum(j[:, None] >= region_end[None, :], axis=1), N_GROUPS - 1)
    last_grp = jnp.max(jnp.where(region > 0, jnp.arange(N_GROUPS), 0))
    grp = jnp.where(valid, grp, last_grp).astype(I32)
    return (src_off.astype(I32), dst_off.astype(I32), seg_len.astype(I32), fill, grp,
            valid.astype(I32))


def _moe(x, mod_l, g_ffn, w_r, b_r, tri, wg, wu, wd, g_out, batch, seq, final_norm):
    n = x.shape[0]
    _, _, e_tm, rows = _moe_layout(n)
    xt, gt, d_rows, cnt = _route(x, mod_l, g_ffn, w_r, b_r, tri, batch, seq)
    cnt = cnt[:, 0:N_GROUPS, 0].astype(I32)
    src_off, dst_off, seg_len, fill, grp, valid = _moe_plan(cnt, n)
    xs, gs = _dispatch(xt, gt, src_off, dst_off, seg_len, fill, rows)
    ys = _experts(xs, gs, grp, valid, wg, wu, wd, e_tm)
    meta = jnp.concatenate([dst_off, seg_len], axis=1)
    return _combine(x, d_rows, meta, ys, mod_l, g_out, batch, seq, final_norm)


def kernel(x, c, w_mod, b_mod, norm_mix, norm_ffn, w_in, kv_norm, w_kv_up, sgu_norm, w_s, b_s, w_o,
           w_pool, pool_scale, w_gr, b_gr, w_er, b_er, w_gate, w_up, w_down, norm_out):
    batch, seq, d = x.shape
    depth = w_mod.shape[0]
    n = batch * seq
    xs = x.reshape(n, d)
    mod = _modulation(c, w_mod, b_mod).reshape(depth, batch, 6, d)

    pos = np.arange(SGU_CHUNK)
    sgu_mask = jnp.asarray((pos[None, :] // CHUNK) <= (pos[:, None] // CHUNK))
    sizes = (A_WIDTH, KV_LATENT, IDX_HEADS * IDX_DIM, IDX_DIM, IDX_HEADS, B_WIDTH, B_WIDTH)
    offs = np.concatenate([[0], np.cumsum(sizes)])
    pad = LANES - IDX_DIM - IDX_HEADS
    rt_tm = _moe_layout(n)[0]
    tri = jnp.asarray(np.triu(np.ones((rt_tm, rt_tm), np.float32), 1), BF16)

    for i in range(depth):
        j = i // 2
        mod_l = mod[i]
        g_mix = norm_mix[i].reshape(1, d)
        if i % 2 == 0:
            wi = w_in[j]
            w_in_r = jnp.concatenate(
                [wi[:, offs[0]:offs[3]], wi[:, offs[3]:offs[5]], jnp.zeros((d, pad), wi.dtype),
                 wi[:, offs[5]:offs[7]]], axis=1).astype(BF16)
            w_s_m = jnp.where(sgu_mask[None], w_s[j], 0.0).astype(BF16)
            b_s_b = jnp.broadcast_to(b_s[j][:, :, None], (B_GROUPS, SGU_CHUNK, B_WIDTH // B_GROUPS))
            q, kv, qi, misc, bo = _in_proj(
                xs, mod_l, g_mix, w_in_r, kv_norm[j].reshape(1, -1), w_kv_up[j].astype(BF16),
                sgu_norm[j].reshape(1, -1), w_s_m, b_s_b, batch, seq)
            a = _attention(q, qi, misc, kv, batch, seq)
            xs = _out_proj(xs, a, bo, w_o[j].astype(BF16), mod_l, batch, seq)
        else:
            xs = _pool_mixer(xs, mod_l, g_mix, w_pool[j].astype(BF16), pool_scale[j].reshape(1, d),
                             batch, seq)
        w_r = jnp.concatenate([w_gr[i], w_er[i],
                               jnp.zeros((d, LANES - N_GROUPS - N_EXPERTS), F32)], axis=1)
        b_r = jnp.concatenate([b_gr[i], b_er[i],
                               jnp.zeros((LANES - N_GROUPS - N_EXPERTS,), F32)]).reshape(1, LANES)
        xs = _moe(xs, mod_l, norm_ffn[i].reshape(1, d), w_r, b_r, tri,
                  w_gate[i].astype(BF16), w_up[i].astype(BF16), w_down[i].astype(BF16),
                  norm_out.reshape(1, d), batch, seq, final_norm=(i == depth - 1))
    return xs.reshape(batch, seq, d)
```

```python
import functools

import numpy as np
import jax
import jax.numpy as jnp
from jax import lax
from jax.experimental import pallas as pl
from jax.experimental.pallas import tpu as pltpu

F32 = jnp.float32
BF16 = jnp.bfloat16
I32 = jnp.int32

EPS = 1e-6
CHUNK = 64
CHUNK_SHIFT = 6
Q_BLOCK = 256
A_HEADS = 8
A_HEAD_DIM = 64
A_WIDTH = A_HEADS * A_HEAD_DIM
KV_LATENT = 128
IDX_HEADS = 8
IDX_DIM = 64
TOPK_MAX = 256
B_GROUPS = 4
SGU_CHUNK = 128
POOL_WINDOWS = (2, 4, 8, 16)
POOL_HALO = 16
N_GROUPS = 4
EXPERTS_PER_GROUP = 8
N_EXPERTS = N_GROUPS * EXPERTS_PER_GROUP
LANES = 128
NEG_BIG = -1e30
INT_MIN = np.int32(-2 ** 31)
KEY_NEG_INF = np.int32(np.int32(-8388608) ^ np.int32(0x7FFFFFFF))
VMEM_LIMIT = 56 * 1024 * 1024

_NT = (((1,), (1,)), ((), ()))


def _params(*sem):
    return pltpu.CompilerParams(dimension_semantics=sem, vmem_limit_bytes=VMEM_LIMIT)


def _norm_mod(x, g, shift, scale):
    y = x * lax.rsqrt(jnp.mean(x * x, axis=-1, keepdims=True) + EPS)
    return (y * g) * (1.0 + scale) + shift


def _mod_kernel(c_ref, w_ref, b_ref, o_ref):
    c = c_ref[...]
    cs = c * jax.nn.sigmoid(c)
    o_ref[0] = jnp.dot(cs.astype(BF16), w_ref[0].astype(BF16),
                       preferred_element_type=F32) + b_ref[0]


def _modulation(c, w_mod, b_mod):
    depth, d, d6 = w_mod.shape
    b = c.shape[0]
    tn = 1024
    return pl.pallas_call(
        _mod_kernel,
        grid=(depth, d6 // tn),
        in_specs=[pl.BlockSpec((b, d), lambda i, j: (0, 0)),
                  pl.BlockSpec((1, d, tn), lambda i, j: (i, 0, j)),
                  pl.BlockSpec((1, 1, tn), lambda i, j: (i, 0, j))],
        out_specs=pl.BlockSpec((1, b, tn), lambda i, j: (i, 0, j)),
        out_shape=jax.ShapeDtypeStruct((depth, b, d6), F32),
        compiler_params=_params("parallel", "parallel"),
        name="modulation",
    )(c, w_mod, b_mod.reshape(depth, 1, d6))


_C_Q = 0
_C_KV = _C_Q + A_WIDTH
_C_QI = _C_KV + KV_LATENT
_C_MISC = _C_QI + IDX_HEADS * IDX_DIM
_C_U = _C_MISC + LANES
B_WIDTH = 512
_C_V = _C_U + B_WIDTH
_C_END = _C_V + B_WIDTH


def _in_proj_kernel(x_ref, mod_ref, g_ref, w_ref, kvn_ref, wkv_ref, lng_ref, ws_ref, bs_ref,
                    q_ref, kv_ref, qi_ref, misc_ref, b_ref):
    tm = x_ref.shape[0]
    h = _norm_mod(x_ref[...], g_ref[...], mod_ref[0, 0:1, :], mod_ref[0, 1:2, :]).astype(BF16)

    def proj(lo, hi):
        return jnp.dot(h, w_ref[:, lo:hi], preferred_element_type=F32)

    q_ref[...] = proj(_C_Q, _C_KV).astype(BF16)
    qi_ref[...] = proj(_C_QI, _C_MISC).astype(BF16)

    kvl = proj(_C_KV, _C_QI)
    kvn = kvl * lax.rsqrt(jnp.mean(kvl * kvl, axis=-1, keepdims=True) + EPS) * kvn_ref[...]
    kv_ref[...] = jnp.dot(kvn.astype(BF16), wkv_ref[...], preferred_element_type=F32).astype(BF16)

    lane = lax.broadcasted_iota(I32, (1, LANES), 1)
    idx_scale = (IDX_HEADS ** -0.5) * (IDX_DIM ** -0.5)
    w_lane = (lane >= IDX_DIM) & (lane < IDX_DIM + IDX_HEADS)
    misc_ref[...] = proj(_C_MISC, _C_U) * jnp.where(w_lane, idx_scale, 1.0)

    gu = jax.nn.gelu(proj(_C_U, _C_V))
    gv = jax.nn.gelu(proj(_C_V, _C_END))
    mu = jnp.mean(gv, axis=-1, keepdims=True)
    cen = gv - mu
    var = jnp.mean(cen * cen, axis=-1, keepdims=True)
    vn = (cen * lax.rsqrt(var + EPS) * lng_ref[...]).astype(BF16)
    gd = B_WIDTH // B_GROUPS
    for r in range(tm // SGU_CHUNK):
        rows = slice(r * SGU_CHUNK, (r + 1) * SGU_CHUNK)
        for g in range(B_GROUPS):
            cols = slice(g * gd, (g + 1) * gd)
            mixed = jnp.dot(ws_ref[g], vn[rows, cols], preferred_element_type=F32) + bs_ref[g]
            b_ref[rows, cols] = (gu[rows, cols] * mixed).astype(BF16)


def _in_proj(x, mod_l, g, w_in_r, kv_norm, w_kv_up, sgu_norm, w_s_m, b_s_b, batch, seq):
    n, d = x.shape
    tm = 512
    tps = seq // tm
    row = lambda b, t: (b * tps + t, 0)
    full2 = lambda b, t: (0, 0)
    full3 = lambda b, t: (0, 0, 0)
    outs = [(A_WIDTH, BF16), (2 * A_HEAD_DIM, BF16), (IDX_HEADS * IDX_DIM, BF16), (LANES, F32),
            (B_WIDTH, BF16)]
    return pl.pallas_call(
        _in_proj_kernel,
        grid=(batch, tps),
        in_specs=[pl.BlockSpec((tm, d), row),
                  pl.BlockSpec((1, 6, d), lambda b, t: (b, 0, 0)),
                  pl.BlockSpec((1, d), full2),
                  pl.BlockSpec(w_in_r.shape, full2),
                  pl.BlockSpec((1, KV_LATENT), full2),
                  pl.BlockSpec(w_kv_up.shape, full2),
                  pl.BlockSpec((1, B_WIDTH), full2),
                  pl.BlockSpec(w_s_m.shape, full3),
                  pl.BlockSpec(b_s_b.shape, full3)],
        out_specs=[pl.BlockSpec((tm, w), row) for w, _ in outs],
        out_shape=[jax.ShapeDtypeStruct((n, w), dt) for w, dt in outs],
        compiler_params=_params("parallel", "parallel"),
        name="in_proj_sgu",
    )(x, mod_l, g, w_in_r, kv_norm, w_kv_up, sgu_norm, w_s_m, b_s_b)


I16 = jnp.int16
SEL_FIELD_BITS = 5
ATT_SCALE = A_HEAD_DIM ** -0.5
assert ATT_SCALE == 2.0 ** round(np.log2(ATT_SCALE)), "folded into bf16 q, must be a power of two"


def _lane_tiles(x):
    return [x[:, j * LANES:(j + 1) * LANES] for j in range(x.shape[1] // LANES)]


def _fold_lanes(x):
    tiles = _lane_tiles(x)
    acc = tiles[0]
    for tile in tiles[1:]:
        acc = acc + tile
    return acc


def _lane_total(x):
    return jnp.dot(x.astype(BF16), jnp.ones((LANES, LANES), BF16), preferred_element_type=F32)


def _radix4_select(tiles_ref, nkt, kneed):
    _, qb, _ = tiles_ref.shape
    c1 = 1
    c2 = c1 + (1 << SEL_FIELD_BITS)
    c3 = c2 + (1 << (2 * SEL_FIELD_BITS))
    fmask = (1 << SEL_FIELD_BITS) - 1

    def step(i, prefix):
        unit = lax.shift_left(np.int32(1), 14 - 2 * i)
        ts = [(prefix + (j * unit - 32768)).astype(I16) for j in (1, 2, 3)]

        def count_tile(kt, acc):
            for vt in _lane_tiles(tiles_ref[kt]):
                acc = acc + jnp.where(
                    vt >= ts[2], np.int16(c3),
                    jnp.where(vt >= ts[1], np.int16(c2),
                              jnp.where(vt >= ts[0], np.int16(c1), np.int16(0))))
            return acc

        acc = lax.fori_loop(0, nkt, count_tile, jnp.zeros((qb, LANES), I16)).astype(I32)
        cnt = [_lane_total(((acc >> (j * SEL_FIELD_BITS)) & fmask).astype(F32)) for j in range(3)]
        digit = jnp.where(cnt[2] >= kneed, 3,
                          jnp.where(cnt[1] >= kneed, 2, jnp.where(cnt[0] >= kneed, 1, 0)))
        return prefix + digit * unit

    return lax.fori_loop(0, 8, step, jnp.zeros((qb, LANES), I32))


def _attn_kernel(q_ref, qi_ref, mq_ref, mk_ref, kv_ref, o_ref,
                 key_ref, hi_ref, lo_ref, pen_ref, nd_ref, w_ref, s_ref, m_ref, l_ref, acc_ref, *, topk):
    _, qb, tk = key_ref.shape
    t = pl.program_id(1)
    nkt = t + 1
    kf = float(topk)
    row = lax.broadcasted_iota(I32, (qb, tk), 0)
    lane = lax.broadcasted_iota(I32, (qb, tk), 1)
    qchunk = (t * qb + row) >> CHUNK_SHIFT
    rel = row - lane

    def tile_geom(kt):
        col = kt * tk + lane
        visible = (col >> CHUNK_SHIFT) <= qchunk
        neg_dist = -jnp.abs(rel + (t - kt) * tk).astype(F32)
        return col, visible, neg_dist

    def row_off(kt):
        return pl.multiple_of(kt * tk, tk)

    @pl.when(nkt * tk <= topk)
    def _():
        def fill(kt, carry):
            _, visible, neg_dist = tile_geom(kt)
            pen_ref[kt] = jnp.where(visible, 0.0, NEG_BIG)
            nd_ref[kt] = neg_dist
            return carry

        lax.fori_loop(0, nkt, fill, 0)

    @pl.when(nkt * tk > topk)
    def _():
        qi = qi_ref[...]
        mq = mq_ref[...]
        qih = [qi[:, h * IDX_DIM:(h + 1) * IDX_DIM] for h in range(IDX_HEADS)]
        for h in range(IDX_HEADS):
            w_ref[h] = jnp.broadcast_to(mq[:, IDX_DIM + h:IDX_DIM + h + 1], (qb, LANES))

        def score_tile(kt, carry):
            k_idx = mk_ref[pl.ds(row_off(kt), tk), 0:IDX_DIM].astype(BF16)
            parts = [jnp.zeros((qb, LANES), F32)] * (tk // LANES)
            for h in range(IDX_HEADS):
                logits = lax.dot_general(qih[h], k_idx, _NT, preferred_element_type=F32)
                w = w_ref[h]
                parts = [p + w * jnp.maximum(lt, 0.0) for p, lt in zip(parts, _lane_tiles(logits))]
            _, visible, neg_dist = tile_geom(kt)
            score = jnp.where(visible, jnp.concatenate(parts, axis=-1), -jnp.inf)
            bits = pltpu.bitcast(score, I32)
            key = bits ^ ((bits >> 31) & np.int32(0x7FFFFFFF))
            key_ref[kt] = key
            hi_ref[kt] = (key >> 16).astype(I16)
            lo_ref[kt] = ((key & 0xFFFF) - 32768).astype(I16)
            nd_ref[kt] = neg_dist
            return carry

        lax.fori_loop(0, nkt, score_tile, 0)

        t_hi = _radix4_select(hi_ref, nkt, jnp.full((qb, LANES), kf, F32)) - 32768
        t_hi16 = t_hi.astype(I16)

        def tie_tile(kt, acc):
            masked = []
            for ht, lt in zip(_lane_tiles(hi_ref[kt]), _lane_tiles(lo_ref[kt])):
                masked.append(jnp.where(ht == t_hi16, lt, np.int16(-32768)))
                acc = acc + jnp.where(ht > t_hi16, np.int16(1), np.int16(0))
            lo_ref[kt] = jnp.concatenate(masked, axis=-1)
            return acc

        above = lax.fori_loop(0, nkt, tie_tile, jnp.zeros((qb, LANES), I16))
        need_lo = kf - _lane_total(above.astype(F32))
        thr = (t_hi << 16) | _radix4_select(lo_ref, nkt, need_lo)

        def pen_tile(kt, carry):
            n_gt, n_eq = carry
            _, visible, _ = tile_geom(kt)
            pens = []
            for key, vis in zip(_lane_tiles(key_ref[kt]), _lane_tiles(visible)):
                pens.append(jnp.where((key >= thr) & vis, 0.0, NEG_BIG))
                n_gt = n_gt + jnp.where(key > thr, 1.0, 0.0)
                n_eq = n_eq + jnp.where(key == thr, 1.0, 0.0)
            pen_ref[kt] = jnp.concatenate(pens, axis=-1)
            return n_gt, n_eq

        zero = jnp.zeros((qb, LANES), F32)
        n_gt, n_eq = lax.fori_loop(0, nkt, pen_tile, (zero, zero))
        need = kf - _lane_total(n_gt)
        n_eq = _lane_total(n_eq)

        tie_rows = jnp.where((n_eq > need) & (thr > KEY_NEG_INF), 1.0, 0.0)

        @pl.when(jnp.max(tie_rows) > 0.0)
        def _():
            nbits = int(key_ref.shape[0] * tk - 1).bit_length()
            thr1 = thr[:, 0:1]
            need1 = need[:, 0:1]

            def idx_step(i, p):
                trial = p | lax.shift_left(np.int32(1), nbits - 1 - i)

                def count_tile(kt, acc):
                    col, _, _ = tile_geom(kt)
                    hit = (key_ref[kt] == thr1) & (col < trial)
                    return acc + _fold_lanes(jnp.where(hit, 1.0, 0.0))

                acc = lax.fori_loop(0, nkt, count_tile, jnp.zeros((qb, LANES), F32))
                cnt = jnp.sum(acc, axis=-1, keepdims=True)
                return jnp.where(cnt < need1, trial, p)

            last = lax.fori_loop(0, nbits, idx_step, jnp.zeros((qb, 1), I32))

            def repen_tile(kt, carry):
                key = key_ref[kt]
                col, visible, _ = tile_geom(kt)
                sel = (key > thr1) | ((key == thr1) & (col <= last))
                pen_ref[kt] = jnp.where(sel & visible, 0.0, NEG_BIG)
                return carry

            lax.fori_loop(0, nkt, repen_tile, 0)

    qs = (q_ref[...].astype(F32) * ATT_SCALE).astype(BF16)
    qh = [qs[:, h * A_HEAD_DIM:(h + 1) * A_HEAD_DIM] for h in range(A_HEADS)]
    for h in range(A_HEADS):
        m_ref[h] = jnp.full((qb, LANES), NEG_BIG, F32)
        l_ref[h] = jnp.zeros((qb, LANES), F32)
        acc_ref[h] = jnp.zeros((qb, A_HEAD_DIM), F32)

    def logit_tile(kt, carry):
        kk = kv_ref[pl.ds(row_off(kt), tk), 0:A_HEAD_DIM]
        pen = _lane_tiles(pen_ref[kt])
        nd = _lane_tiles(nd_ref[kt])
        for h in range(A_HEADS):
            slope = float(2.0 ** (-8.0 * (h + 1) / A_HEADS))
            s = lax.dot_general(qh[h], kk, _NT, preferred_element_type=F32)
            st = [a + (slope * b + c) for a, b, c in zip(_lane_tiles(s), nd, pen)]
            s_ref[h, kt] = jnp.concatenate(st, axis=-1)
            tile_max = m_ref[h]
            for x in st:
                tile_max = jnp.maximum(tile_max, x)
            m_ref[h] = tile_max
        return carry

    lax.fori_loop(0, nkt, logit_tile, 0)
    for h in range(A_HEADS):
        m_ref[h] = jnp.broadcast_to(jnp.max(m_ref[h], axis=-1, keepdims=True), (qb, LANES))

    def prob_tile(kt, carry):
        vv = kv_ref[pl.ds(row_off(kt), tk), A_HEAD_DIM:2 * A_HEAD_DIM]
        for h in range(A_HEADS):
            m = m_ref[h]
            ps = [jnp.exp(x - m) for x in _lane_tiles(s_ref[h, kt])]
            psum = l_ref[h]
            for x in ps:
                psum = psum + x
            l_ref[h] = psum
            p = jnp.concatenate(ps, axis=-1).astype(BF16)
            acc_ref[h] = acc_ref[h] + jnp.dot(p, vv, preferred_element_type=F32)
        return carry

    lax.fori_loop(0, nkt, prob_tile, 0)
    outs = [acc_ref[h] / jnp.sum(l_ref[h], axis=-1, keepdims=True) for h in range(A_HEADS)]
    o_ref[...] = jnp.concatenate(outs, axis=-1).astype(BF16)


def _attention(q, qi, misc, kv, batch, seq):
    n = q.shape[0]
    qb = min(Q_BLOCK, seq)
    nb = seq // qb
    topk = min(TOPK_MAX, seq // 4)
    assert seq // LANES < (1 << SEL_FIELD_BITS)
    rowq = lambda b, t: (b * nb + t, 0)
    rowk = lambda b, t: (b, 0)
    tiles = (nb, qb, qb)
    return pl.pallas_call(
        functools.partial(_attn_kernel, topk=topk),
        grid=(batch, nb),
        in_specs=[pl.BlockSpec((qb, A_WIDTH), rowq),
                  pl.BlockSpec((qb, IDX_HEADS * IDX_DIM), rowq),
                  pl.BlockSpec((qb, LANES), rowq),
                  pl.BlockSpec((seq, LANES), rowk),
                  pl.BlockSpec((seq, 2 * A_HEAD_DIM), rowk)],
        out_specs=pl.BlockSpec((qb, A_WIDTH), rowq),
        out_shape=jax.ShapeDtypeStruct((n, A_WIDTH), BF16),
        scratch_shapes=[pltpu.VMEM(tiles, I32), pltpu.VMEM(tiles, I16), pltpu.VMEM(tiles, I16),
                        pltpu.VMEM(tiles, F32), pltpu.VMEM(tiles, F32),
                        pltpu.VMEM((IDX_HEADS, qb, LANES), F32),
                        pltpu.VMEM((A_HEADS,) + tiles, F32),
                        pltpu.VMEM((A_HEADS, qb, LANES), F32),
                        pltpu.VMEM((A_HEADS, qb, LANES), F32),
                        pltpu.VMEM((A_HEADS, qb, A_HEAD_DIM), F32)],
        compiler_params=_params("parallel", "parallel"),
        name="dsa_attention",
    )(q, qi, misc, misc, kv)


def _out_proj_kernel(x_ref, a_ref, b_ref, w_ref, mod_ref, o_ref):
    y = jnp.dot(a_ref[...], w_ref[0:A_WIDTH, :], preferred_element_type=F32)
    y = y + jnp.dot(b_ref[...], w_ref[A_WIDTH:, :], preferred_element_type=F32)
    o_ref[...] = x_ref[...] + mod_ref[0, 2:3, :] * y


def _out_proj(x, a, bo, w_o, mod_l, batch, seq):
    n, d = x.shape
    tm = 512
    tps = seq // tm
    row = lambda b, t: (b * tps + t, 0)
    return pl.pallas_call(
        _out_proj_kernel,
        grid=(batch, tps),
        in_specs=[pl.BlockSpec((tm, d), row),
                  pl.BlockSpec((tm, A_WIDTH), row),
                  pl.BlockSpec((tm, B_WIDTH), row),
                  pl.BlockSpec(w_o.shape, lambda b, t: (0, 0)),
                  pl.BlockSpec((1, 6, d), lambda b, t: (b, 0, 0))],
        out_specs=pl.BlockSpec((tm, d), row),
        out_shape=jax.ShapeDtypeStruct((n, d), F32),
        compiler_params=_params("parallel", "parallel"),
        name="out_proj",
    )(x, a, bo, w_o, mod_l)


def _pool_kernel(x_ref, halo_ref, mod_ref, g_ref, w_ref, sc_ref, o_ref):
    tm, d = x_ref.shape
    t = pl.program_id(1)
    g = g_ref[...]
    shift = mod_ref[0, 0:1, :]
    scale = mod_ref[0, 1:2, :]
    x = x_ref[...]
    h = _norm_mod(x, g, shift, scale)
    hh = _norm_mod(halo_ref[...], g, shift, scale)
    hh = jnp.where(t > 0, hh, 0.0)
    ext = jnp.concatenate([hh, h], axis=0)
    pos = t * tm + lax.broadcasted_iota(I32, (tm, 1), 0)
    gd = d // len(POOL_WINDOWS)
    ys = []
    for gi, win in enumerate(POOL_WINDOWS):
        cols = slice(gi * gd, (gi + 1) * gd)
        cur = ext[:, cols]
        k = 1
        while k < win:
            cur = cur + pltpu.roll(cur, k, axis=0)
            k *= 2
        cnt = jnp.minimum(pos + 1, win).astype(F32)
        y = (cur[POOL_HALO:, :] / cnt - h[:, cols]).astype(BF16)
        ys.append(jnp.dot(y, w_ref[gi], preferred_element_type=F32))
    y = jnp.concatenate(ys, axis=-1) * sc_ref[...]
    o_ref[...] = x + mod_ref[0, 2:3, :] * y


def _pool_mixer(x, mod_l, g, w_pool, scale, batch, seq):
    n, d = x.shape
    tm = 512
    tps = seq // tm
    row = lambda b, t: (b * tps + t, 0)
    hpt = tm // POOL_HALO
    halo = lambda b, t: (jnp.maximum((b * tps + t) * hpt - 1, 0), 0)
    return pl.pallas_call(
        _pool_kernel,
        grid=(batch, tps),
        in_specs=[pl.BlockSpec((tm, d), row),
                  pl.BlockSpec((POOL_HALO, d), halo),
                  pl.BlockSpec((1, 6, d), lambda b, t: (b, 0, 0)),
                  pl.BlockSpec((1, d), lambda b, t: (0, 0)),
                  pl.BlockSpec(w_pool.shape, lambda b, t: (0, 0, 0)),
                  pl.BlockSpec((1, d), lambda b, t: (0, 0))],
        out_specs=pl.BlockSpec((tm, d), row),
        out_shape=jax.ShapeDtypeStruct((n, d), F32),
        compiler_params=_params("parallel", "parallel"),
        name="pool_mixer",
    )(x, x, mod_l, g, w_pool, scale)


_R_EXP0 = N_GROUPS
RT_TM = 512
SEG_ALIGN = 16
SEG_SHIFT = 4
E_TM = 1152

_TN = (((0,), (0,)), ((), ()))


def _moe_layout(n):
    tm = min(RT_TM, n)
    xs_rows = tm + N_GROUPS * SEG_ALIGN
    e_tm = E_TM if n >= 4 * E_TM else 128
    worst = n + (n // tm) * N_GROUPS * SEG_ALIGN + N_GROUPS * e_tm
    rows = -(-worst // e_tm) * e_tm
    return tm, xs_rows, e_tm, rows


def _perm_matrix(d_row, rows):
    tm = d_row.shape[1]
    return jnp.where(lax.broadcasted_iota(I32, (rows, tm), 0) == d_row, 1.0, 0.0).astype(BF16)


def _split_bf16(a):
    hi = a.astype(BF16)
    return hi, (a - hi.astype(F32)).astype(BF16)


def _route_kernel(x_ref, mod_ref, g_ref, w_ref, b_ref, tri_ref, xs_ref, gs_ref, d_ref, cnt_ref):
    tm = x_ref.shape[0]
    xs_rows = xs_ref.shape[0]
    h = _norm_mod(x_ref[...], g_ref[...], mod_ref[0, 3:4, :], mod_ref[0, 4:5, :])
    logits = jnp.dot(h, w_ref[...], precision=lax.Precision.HIGHEST,
                     preferred_element_type=F32) + b_ref[...]
    lane = lax.broadcasted_iota(I32, (tm, LANES), 1)
    big = np.int32(LANES)

    def top1(vals):
        m = jnp.max(vals, axis=-1, keepdims=True)
        idx = jnp.min(jnp.where(vals == m, lane, big), axis=-1, keepdims=True)
        return m, idx

    gl = jnp.where(lane < N_GROUPS, logits, -jnp.inf)
    gmax, gsel = top1(gl)
    p_g = 1.0 / jnp.sum(jnp.exp(gl - gmax), axis=-1, keepdims=True)
    e_lo = _R_EXP0 + gsel * EXPERTS_PER_GROUP
    el = jnp.where((lane >= e_lo) & (lane < e_lo + EXPERTS_PER_GROUP), logits, -jnp.inf)
    v1, i1 = top1(el)
    v2, i2 = top1(jnp.where(lane == i1, -jnp.inf, el))
    e2 = jnp.exp(v2 - v1)
    g1 = p_g / (1.0 + e2)
    g2 = p_g * e2 / (1.0 + e2)
    cg = jnp.where(lane == i1 - e_lo, g1, 0.0) + jnp.where(lane == i2 - e_lo, g2, 0.0)

    member = jnp.where(lane == gsel, 1.0, 0.0).T[0:8, :]
    rank = jnp.dot(member.astype(BF16), tri_ref[...], preferred_element_type=F32)
    count = jnp.sum(member, axis=1, keepdims=True)
    cnt_ref[0] = jnp.broadcast_to(count, (8, LANES))
    padded = jnp.ceil(count * (1.0 / SEG_ALIGN)) * SEG_ALIGN
    d_row = jnp.zeros((1, tm), F32)
    seg = jnp.zeros((1, 1), F32)
    for g in range(N_GROUPS):
        d_row = d_row + member[g:g + 1, :] * (rank[g:g + 1, :] + seg)
        seg = seg + padded[g:g + 1, :]
    d_row = d_row.astype(I32)
    d_ref[0] = d_row

    perm = _perm_matrix(d_row, xs_rows)
    xs_ref[...] = jnp.dot(perm, h.astype(BF16), preferred_element_type=F32).astype(BF16)
    cg_hi, cg_lo = _split_bf16(cg)
    gs_ref[...] = (jnp.dot(perm, cg_hi, preferred_element_type=F32)
                   + jnp.dot(perm, cg_lo, preferred_element_type=F32))


def _route(x, mod_l, g, w_r, b_r, tri, batch, seq):
    n, d = x.shape
    tm, xs_rows, _, _ = _moe_layout(n)
    nt = n // tm
    tps = seq // tm
    full = lambda i: (0, 0)
    return pl.pallas_call(
        _route_kernel,
        grid=(nt,),
        in_specs=[pl.BlockSpec((tm, d), lambda i: (i, 0)),
                  pl.BlockSpec((1, 6, d), lambda i: (i // tps, 0, 0)),
                  pl.BlockSpec((1, d), full),
                  pl.BlockSpec((d, LANES), full),
                  pl.BlockSpec((1, LANES), full),
                  pl.BlockSpec((tm, tm), full)],
        out_specs=[pl.BlockSpec((xs_rows, d), lambda i: (i, 0)),
                   pl.BlockSpec((xs_rows, LANES), lambda i: (i, 0)),
                   pl.BlockSpec((1, 1, tm), lambda i: (i, 0, 0)),
                   pl.BlockSpec((1, 8, LANES), lambda i: (i, 0, 0))],
        out_shape=[jax.ShapeDtypeStruct((nt * xs_rows, d), BF16),
                   jax.ShapeDtypeStruct((nt * xs_rows, LANES), F32),
                   jax.ShapeDtypeStruct((nt, 1, tm), I32),
                   jax.ShapeDtypeStruct((nt, 8, LANES), F32)],
        compiler_params=_params("parallel"),
        name="route_sort",
    )(x, mod_l, g, w_r, b_r, tri)


def _dispatch_kernel(src_ref, dst_ref, len_ref, fill_ref, xt_ref, gt_ref, xs_hbm, gs_hbm,
                     zx_ref, zg_ref, sem_x, sem_g):
    i = pl.program_id(0)
    nt = src_ref.shape[0]

    def chunk_copies(x_src, g_src, src_row, dst_row):
        src_row = pl.multiple_of(src_row, SEG_ALIGN)
        dst_row = pl.multiple_of(dst_row, SEG_ALIGN)
        return (pltpu.make_async_copy(x_src.at[pl.ds(src_row, SEG_ALIGN)],
                                      xs_hbm.at[pl.ds(dst_row, SEG_ALIGN)], sem_x),
                pltpu.make_async_copy(g_src.at[pl.ds(src_row, SEG_ALIGN)],
                                      gs_hbm.at[pl.ds(dst_row, SEG_ALIGN)], sem_g))

    def run(x_src, g_src, jobs):
        total = jnp.int32(0)
        for src, dst, rows in jobs:
            nch = rows >> SEG_SHIFT

            def issue(k, carry, src=src, dst=dst):
                for cp in chunk_copies(x_src, g_src, src + k * SEG_ALIGN, dst + k * SEG_ALIGN):
                    cp.start()
                return carry

            lax.fori_loop(0, nch, issue, 0)
            total = total + nch

        def drain(k, carry):
            for cp in chunk_copies(x_src, g_src, 0, 0):
                cp.wait()
            return carry

        lax.fori_loop(0, total, drain, 0)

    @pl.when(i < nt)
    def _():
        t = jnp.minimum(i, nt - 1)
        run(xt_ref, gt_ref, [(src_ref[t, g], dst_ref[t, g], len_ref[t, g]) for g in range(N_GROUPS)])

    @pl.when(i == nt)
    def _():
        zx_ref[...] = jnp.zeros_like(zx_ref)
        zg_ref[...] = jnp.zeros_like(zg_ref)
        for g in range(N_GROUPS + 1):
            start = fill_ref[0, g]
            nch = fill_ref[1, g] >> SEG_SHIFT

            def issue(k, carry, start=start):
                for cp in chunk_copies(zx_ref, zg_ref, 0, start + k * SEG_ALIGN):
                    cp.start()
                return carry

            lax.fori_loop(0, nch, issue, 0)

            def drain(k, carry):
                for cp in chunk_copies(zx_ref, zg_ref, 0, 0):
                    cp.wait()
                return carry

            lax.fori_loop(0, nch, drain, 0)


def _dispatch(xt, gt, src_off, dst_off, seg_len, fill, rows):
    d = xt.shape[1]
    nt = src_off.shape[0]
    xs_rows = xt.shape[0] // nt
    tile = lambda i, *_: (jnp.minimum(i, nt - 1), 0)
    return pl.pallas_call(
        _dispatch_kernel,
        grid_spec=pltpu.PrefetchScalarGridSpec(
            num_scalar_prefetch=4,
            grid=(nt + 1,),
            in_specs=[pl.BlockSpec((xs_rows, d), tile), pl.BlockSpec((xs_rows, LANES), tile)],
            out_specs=[pl.BlockSpec(memory_space=pl.ANY), pl.BlockSpec(memory_space=pl.ANY)],
            scratch_shapes=[pltpu.VMEM((SEG_ALIGN, d), BF16), pltpu.VMEM((SEG_ALIGN, LANES), F32),
                            pltpu.SemaphoreType.DMA, pltpu.SemaphoreType.DMA]),
        out_shape=[jax.ShapeDtypeStruct((rows, d), BF16),
                   jax.ShapeDtypeStruct((rows, LANES), F32)],
        compiler_params=_params("arbitrary"),
        name="dispatch",
    )(src_off, dst_off, seg_len, fill, xt, gt)


def _experts_kernel(grp_ref, valid_ref, x_ref, gs_ref, wg_ref, wu_ref, wd_ref, o_ref, acc_ref):
    j = pl.program_id(0)
    e = pl.program_id(1)

    @pl.when(e == 0)
    def _():
        acc_ref[...] = jnp.zeros_like(acc_ref)

    @pl.when(valid_ref[j] > 0)
    def _():
        x = x_ref[...]
        hg = jnp.dot(x, wg_ref[0, 0].astype(BF16), preferred_element_type=F32)
        hu = jnp.dot(x, wu_ref[0, 0].astype(BF16), preferred_element_type=F32)
        lane = lax.broadcasted_iota(I32, gs_ref.shape, 1)
        gate = jnp.sum(jnp.where(lane == e, gs_ref[...], 0.0), axis=-1, keepdims=True)
        act = (hg * jax.nn.sigmoid(hg)) * hu * gate
        acc_ref[...] += jnp.dot(act.astype(BF16), wd_ref[0, 0].astype(BF16),
                                preferred_element_type=F32)

    @pl.when(e == pl.num_programs(1) - 1)
    def _():
        o_ref[...] = acc_ref[...]


def _experts(xs, gs, tile_grp, tile_valid, wg, wu, wd, layer, e_tm):
    rows, d = xs.shape
    f = wg.shape[-1]
    ntile = tile_grp.shape[0]
    row = lambda j, e, grp, valid: (j, 0)
    wsel = lambda j, e, grp, valid: (
        layer, grp[j] * EXPERTS_PER_GROUP + jnp.where(valid[j] > 0, e, EXPERTS_PER_GROUP - 1), 0, 0)
    return pl.pallas_call(
        _experts_kernel,
        grid_spec=pltpu.PrefetchScalarGridSpec(
            num_scalar_prefetch=2,
            grid=(ntile, EXPERTS_PER_GROUP),
            in_specs=[pl.BlockSpec((e_tm, d), row),
                      pl.BlockSpec((e_tm, LANES), row),
                      pl.BlockSpec((1, 1, d, f), wsel),
                      pl.BlockSpec((1, 1, d, f), wsel),
                      pl.BlockSpec((1, 1, f, d), wsel)],
            out_specs=pl.BlockSpec((e_tm, d), row),
            scratch_shapes=[pltpu.VMEM((e_tm, d), F32)]),
        out_shape=jax.ShapeDtypeStruct((rows, d), F32),
        compiler_params=_params("arbitrary", "arbitrary"),
        name="experts",
    )(tile_grp, tile_valid, xs, gs, wg, wu, wd)


def _combine_kernel(meta_ref, x_ref, d_ref, mod_ref, gout_ref, ys_hbm, o_ref, yseg, sem, *, final_norm):
    i = pl.program_id(0)
    xs_rows = yseg.shape[0]
    yseg[...] = jnp.zeros_like(yseg)

    def seg_copy(src_row, dst_row):
        src_row = pl.multiple_of(src_row, SEG_ALIGN)
        dst_row = pl.multiple_of(dst_row, SEG_ALIGN)
        return pltpu.make_async_copy(ys_hbm.at[pl.ds(src_row, SEG_ALIGN)],
                                     yseg.at[pl.ds(dst_row, SEG_ALIGN)], sem)

    seg = jnp.int32(0)
    total_chunks = jnp.int32(0)
    for g in range(N_GROUPS):
        base = meta_ref[i, g]
        c16 = meta_ref[i, N_GROUPS + g]
        nch = c16 >> SEG_SHIFT

        def issue(k, carry, seg=seg, base=base):
            seg_copy(base + k * SEG_ALIGN, seg + k * SEG_ALIGN).start()
            return carry

        lax.fori_loop(0, nch, issue, 0)
        seg = seg + c16
        total_chunks = total_chunks + nch

    def drain(k, carry):
        seg_copy(0, 0).wait()
        return carry

    lax.fori_loop(0, total_chunks, drain, 0)

    perm = _perm_matrix(d_ref[0], xs_rows)
    y_hi, y_lo = _split_bf16(yseg[...])
    y = (lax.dot_general(perm, y_hi, _TN, preferred_element_type=F32)
         + lax.dot_general(perm, y_lo, _TN, preferred_element_type=F32))
    out = x_ref[...] + mod_ref[0, 5:6, :] * y
    if final_norm:
        out = out * lax.rsqrt(jnp.mean(out * out, axis=-1, keepdims=True) + EPS) * gout_ref[...]
    o_ref[...] = out


def _combine(x, d_rows, meta, ys, mod_l, g_out, batch, seq, final_norm):
    n, d = x.shape
    tm, xs_rows, _, _ = _moe_layout(n)
    tps = seq // tm
    return pl.pallas_call(
        functools.partial(_combine_kernel, final_norm=final_norm),
        grid_spec=pltpu.PrefetchScalarGridSpec(
            num_scalar_prefetch=1,
            grid=(n // tm,),
            in_specs=[pl.BlockSpec((tm, d), lambda i, meta: (i, 0)),
                      pl.BlockSpec((1, 1, tm), lambda i, meta: (i, 0, 0)),
                      pl.BlockSpec((1, 6, d), lambda i, meta: (i // tps, 0, 0)),
                      pl.BlockSpec((1, d), lambda i, meta: (0, 0)),
                      pl.BlockSpec(memory_space=pl.ANY)],
            out_specs=pl.BlockSpec((tm, d), lambda i, meta: (i, 0)),
            scratch_shapes=[pltpu.VMEM((xs_rows, d), F32), pltpu.SemaphoreType.DMA]),
        out_shape=jax.ShapeDtypeStruct((n, d), F32),
        compiler_params=_params("arbitrary"),
        name="combine",
    )(meta, x, d_rows, mod_l, g_out, ys)


def _moe_plan(cnt, n):
    _, _, e_tm, rows = _moe_layout(n)
    seg_len = ((cnt + (SEG_ALIGN - 1)) // SEG_ALIGN) * SEG_ALIGN
    totals = jnp.sum(seg_len, axis=0)
    region = ((totals + (e_tm - 1)) // e_tm) * e_tm
    region_end = jnp.cumsum(region)
    region_start = region_end - region
    src_off = jnp.cumsum(seg_len, axis=1) - seg_len
    dst_off = region_start[None, :] + jnp.cumsum(seg_len, axis=0) - seg_len
    fill_start = jnp.concatenate([region_start + totals, region_end[-1:]])
    fill_len = jnp.concatenate([region - totals, rows - region_end[-1:]])
    fill = jnp.stack([fill_start, fill_len]).astype(I32)
    j = jnp.arange(rows // e_tm, dtype=I32) * e_tm
    valid = j < region_end[-1]
    grp = jnp.minimum(jnp.sum(j[:, None] >= region_end[None, :], axis=1), N_GROUPS - 1)
    last_grp = jnp.max(jnp.where(region > 0, jnp.arange(N_GROUPS), 0))
    grp = jnp.where(valid, grp, last_grp).astype(I32)
    return (src_off.astype(I32), dst_off.astype(I32), seg_len.astype(I32), fill, grp,
            valid.astype(I32))


def _moe(x, mod_l, g_ffn, w_r, b_r, tri, wg, wu, wd, layer, g_out, batch, seq, final_norm):
    n = x.shape[0]
    _, _, e_tm, rows = _moe_layout(n)
    xt, gt, d_rows, cnt = _route(x, mod_l, g_ffn, w_r, b_r, tri, batch, seq)
    cnt = cnt[:, 0:N_GROUPS, 0].astype(I32)
    src_off, dst_off, seg_len, fill, grp, valid = _moe_plan(cnt, n)
    xs, gs = _dispatch(xt, gt, src_off, dst_off, seg_len, fill, rows)
    ys = _experts(xs, gs, grp, valid, wg, wu, wd, layer, e_tm)
    meta = jnp.concatenate([dst_off, seg_len], axis=1)
    return _combine(x, d_rows, meta, ys, mod_l, g_out, batch, seq, final_norm)


def kernel(x, c, w_mod, b_mod, norm_mix, norm_ffn, w_in, kv_norm, w_kv_up, sgu_norm, w_s, b_s, w_o,
           w_pool, pool_scale, w_gr, b_gr, w_er, b_er, w_gate, w_up, w_down, norm_out):
    batch, seq, d = x.shape
    depth = w_mod.shape[0]
    n = batch * seq
    xs = x.reshape(n, d)
    mod = _modulation(c, w_mod, b_mod).reshape(depth, batch, 6, d)

    pos = np.arange(SGU_CHUNK)
    sgu_mask = jnp.asarray((pos[None, :] // CHUNK) <= (pos[:, None] // CHUNK))
    sizes = (A_WIDTH, KV_LATENT, IDX_HEADS * IDX_DIM, IDX_DIM, IDX_HEADS, B_WIDTH, B_WIDTH)
    offs = np.concatenate([[0], np.cumsum(sizes)])
    pad = LANES - IDX_DIM - IDX_HEADS
    rt_tm = _moe_layout(n)[0]
    tri = jnp.asarray(np.triu(np.ones((rt_tm, rt_tm), np.float32), 1), BF16)

    for i in range(depth):
        j = i // 2
        mod_l = mod[i]
        g_mix = norm_mix[i].reshape(1, d)
        if i % 2 == 0:
            wi = w_in[j]
            w_in_r = jnp.concatenate(
                [wi[:, offs[0]:offs[3]], wi[:, offs[3]:offs[5]], jnp.zeros((d, pad), wi.dtype),
                 wi[:, offs[5]:offs[7]]], axis=1).astype(BF16)
            w_s_m = jnp.where(sgu_mask[None], w_s[j], 0.0).astype(BF16)
            b_s_b = jnp.broadcast_to(b_s[j][:, :, None], (B_GROUPS, SGU_CHUNK, B_WIDTH // B_GROUPS))
            q, kv, qi, misc, bo = _in_proj(
                xs, mod_l, g_mix, w_in_r, kv_norm[j].reshape(1, -1), w_kv_up[j].astype(BF16),
                sgu_norm[j].reshape(1, -1), w_s_m, b_s_b, batch, seq)
            a = _attention(q, qi, misc, kv, batch, seq)
            xs = _out_proj(xs, a, bo, w_o[j].astype(BF16), mod_l, batch, seq)
        else:
            xs = _pool_mixer(xs, mod_l, g_mix, w_pool[j].astype(BF16), pool_scale[j].reshape(1, d),
                             batch, seq)
        w_r = jnp.concatenate([w_gr[i], w_er[i],
                               jnp.zeros((d, LANES - N_GROUPS - N_EXPERTS), F32)], axis=1)
        b_r = jnp.concatenate([b_gr[i], b_er[i],
                               jnp.zeros((LANES - N_GROUPS - N_EXPERTS,), F32)]).reshape(1, LANES)
        xs = _moe(xs, mod_l, norm_ffn[i].reshape(1, d), w_r, b_r, tri, w_gate, w_up, w_down, i,
                  norm_out.reshape(1, d), batch, seq, final_norm=(i == depth - 1))
    return xs.reshape(batch, seq, d)
```

```python
import functools

import numpy as np
import jax
import jax.numpy as jnp
from jax import lax
from jax.experimental import pallas as pl
from jax.experimental.pallas import tpu as pltpu

F32 = jnp.float32
BF16 = jnp.bfloat16
I32 = jnp.int32

EPS = 1e-6
CHUNK = 64
CHUNK_SHIFT = 6
Q_BLOCK = 256
A_HEADS = 8
A_HEAD_DIM = 64
A_WIDTH = A_HEADS * A_HEAD_DIM
KV_LATENT = 128
IDX_HEADS = 8
IDX_DIM = 64
TOPK_MAX = 256
B_GROUPS = 4
SGU_CHUNK = 128
POOL_WINDOWS = (2, 4, 8, 16)
POOL_HALO = 16
N_GROUPS = 4
EXPERTS_PER_GROUP = 8
N_EXPERTS = N_GROUPS * EXPERTS_PER_GROUP
LANES = 128
NEG_BIG = -1e30
INT_MIN = np.int32(-2 ** 31)
KEY_NEG_INF = np.int32(np.int32(-8388608) ^ np.int32(0x7FFFFFFF))
VMEM_LIMIT = 56 * 1024 * 1024

_NT = (((1,), (1,)), ((), ()))


def _params(*sem):
    return pltpu.CompilerParams(dimension_semantics=sem, vmem_limit_bytes=VMEM_LIMIT)


def _norm_mod(x, g, shift, scale):
    y = x * lax.rsqrt(jnp.mean(x * x, axis=-1, keepdims=True) + EPS)
    return (y * g) * (1.0 + scale) + shift


def _mod_kernel(c_ref, w_ref, b_ref, o_ref):
    c = c_ref[...]
    cs = c * jax.nn.sigmoid(c)
    o_ref[0] = jnp.dot(cs.astype(BF16), w_ref[0].astype(BF16),
                       preferred_element_type=F32) + b_ref[0]


def _modulation(c, w_mod, b_mod):
    depth, d, d6 = w_mod.shape
    b = c.shape[0]
    tn = 1024
    return pl.pallas_call(
        _mod_kernel,
        grid=(depth, d6 // tn),
        in_specs=[pl.BlockSpec((b, d), lambda i, j: (0, 0)),
                  pl.BlockSpec((1, d, tn), lambda i, j: (i, 0, j)),
                  pl.BlockSpec((1, 1, tn), lambda i, j: (i, 0, j))],
        out_specs=pl.BlockSpec((1, b, tn), lambda i, j: (i, 0, j)),
        out_shape=jax.ShapeDtypeStruct((depth, b, d6), F32),
        compiler_params=_params("parallel", "parallel"),
        name="modulation",
    )(c, w_mod, b_mod.reshape(depth, 1, d6))


_C_Q = 0
_C_KV = _C_Q + A_WIDTH
_C_QI = _C_KV + KV_LATENT
_C_MISC = _C_QI + IDX_HEADS * IDX_DIM
_C_U = _C_MISC + LANES
B_WIDTH = 512
_C_V = _C_U + B_WIDTH
_C_END = _C_V + B_WIDTH


def _in_proj_kernel(x_ref, mod_ref, g_ref, w_ref, kvn_ref, wkv_ref, lng_ref, ws_ref, bs_ref,
                    q_ref, kv_ref, qi_ref, misc_ref, b_ref):
    tm = x_ref.shape[0]
    h = _norm_mod(x_ref[...], g_ref[...], mod_ref[0, 0:1, :], mod_ref[0, 1:2, :]).astype(BF16)

    def proj(lo, hi):
        return jnp.dot(h, w_ref[:, lo:hi], preferred_element_type=F32)

    q_ref[...] = proj(_C_Q, _C_KV).astype(BF16)
    qi_ref[...] = proj(_C_QI, _C_MISC).astype(BF16)

    kvl = proj(_C_KV, _C_QI)
    kvn = kvl * lax.rsqrt(jnp.mean(kvl * kvl, axis=-1, keepdims=True) + EPS) * kvn_ref[...]
    kv_ref[...] = jnp.dot(kvn.astype(BF16), wkv_ref[...], preferred_element_type=F32).astype(BF16)

    lane = lax.broadcasted_iota(I32, (1, LANES), 1)
    idx_scale = (IDX_HEADS ** -0.5) * (IDX_DIM ** -0.5)
    w_lane = (lane >= IDX_DIM) & (lane < IDX_DIM + IDX_HEADS)
    misc_ref[...] = proj(_C_MISC, _C_U) * jnp.where(w_lane, idx_scale, 1.0)

    gu = jax.nn.gelu(proj(_C_U, _C_V))
    gv = jax.nn.gelu(proj(_C_V, _C_END))
    mu = jnp.mean(gv, axis=-1, keepdims=True)
    cen = gv - mu
    var = jnp.mean(cen * cen, axis=-1, keepdims=True)
    vn = (cen * lax.rsqrt(var + EPS) * lng_ref[...]).astype(BF16)
    gd = B_WIDTH // B_GROUPS
    for r in range(tm // SGU_CHUNK):
        rows = slice(r * SGU_CHUNK, (r + 1) * SGU_CHUNK)
        for g in range(B_GROUPS):
            cols = slice(g * gd, (g + 1) * gd)
            mixed = jnp.dot(ws_ref[g], vn[rows, cols], preferred_element_type=F32) + bs_ref[g]
            b_ref[rows, cols] = (gu[rows, cols] * mixed).astype(BF16)


def _in_proj(x, mod_l, g, w_in_r, kv_norm, w_kv_up, sgu_norm, w_s_m, b_s_b, batch, seq):
    n, d = x.shape
    tm = 512
    tps = seq // tm
    row = lambda b, t: (b * tps + t, 0)
    full2 = lambda b, t: (0, 0)
    full3 = lambda b, t: (0, 0, 0)
    outs = [(A_WIDTH, BF16), (2 * A_HEAD_DIM, BF16), (IDX_HEADS * IDX_DIM, BF16), (LANES, F32),
            (B_WIDTH, BF16)]
    return pl.pallas_call(
        _in_proj_kernel,
        grid=(batch, tps),
        in_specs=[pl.BlockSpec((tm, d), row),
                  pl.BlockSpec((1, 6, d), lambda b, t: (b, 0, 0)),
                  pl.BlockSpec((1, d), full2),
                  pl.BlockSpec(w_in_r.shape, full2),
                  pl.BlockSpec((1, KV_LATENT), full2),
                  pl.BlockSpec(w_kv_up.shape, full2),
                  pl.BlockSpec((1, B_WIDTH), full2),
                  pl.BlockSpec(w_s_m.shape, full3),
                  pl.BlockSpec(b_s_b.shape, full3)],
        out_specs=[pl.BlockSpec((tm, w), row) for w, _ in outs],
        out_shape=[jax.ShapeDtypeStruct((n, w), dt) for w, dt in outs],
        compiler_params=_params("parallel", "parallel"),
        name="in_proj_sgu",
    )(x, mod_l, g, w_in_r, kv_norm, w_kv_up, sgu_norm, w_s_m, b_s_b)


I16 = jnp.int16
SEL_FIELD_BITS = 5
ATT_SCALE = A_HEAD_DIM ** -0.5
assert ATT_SCALE == 2.0 ** round(np.log2(ATT_SCALE)), "folded into bf16 q, must be a power of two"


def _lane_tiles(x):
    return [x[:, j * LANES:(j + 1) * LANES] for j in range(x.shape[1] // LANES)]


def _fold_lanes(x):
    tiles = _lane_tiles(x)
    acc = tiles[0]
    for tile in tiles[1:]:
        acc = acc + tile
    return acc


def _lane_total(x):
    return jnp.dot(x.astype(BF16), jnp.ones((LANES, LANES), BF16), preferred_element_type=F32)


def _radix4_select(tiles_ref, nkt, kneed):
    _, qb, _ = tiles_ref.shape
    c1 = 1
    c2 = c1 + (1 << SEL_FIELD_BITS)
    c3 = c2 + (1 << (2 * SEL_FIELD_BITS))
    fmask = (1 << SEL_FIELD_BITS) - 1

    def step(i, prefix):
        unit = lax.shift_left(np.int32(1), 14 - 2 * i)
        ts = [(prefix + (j * unit - 32768)).astype(I16) for j in (1, 2, 3)]

        def count_tile(kt, acc):
            for vt in _lane_tiles(tiles_ref[kt]):
                acc = acc + jnp.where(
                    vt >= ts[2], np.int16(c3),
                    jnp.where(vt >= ts[1], np.int16(c2),
                              jnp.where(vt >= ts[0], np.int16(c1), np.int16(0))))
            return acc

        acc = lax.fori_loop(0, nkt, count_tile, jnp.zeros((qb, LANES), I16)).astype(I32)
        cnt = [_lane_total(((acc >> (j * SEL_FIELD_BITS)) & fmask).astype(F32)) for j in range(3)]
        digit = jnp.where(cnt[2] >= kneed, 3,
                          jnp.where(cnt[1] >= kneed, 2, jnp.where(cnt[0] >= kneed, 1, 0)))
        return prefix + digit * unit

    return lax.fori_loop(0, 8, step, jnp.zeros((qb, LANES), I32))


def _attn_kernel(q_ref, qi_ref, mq_ref, mk_ref, kv_ref, o_ref,
                 key_ref, hi_ref, lo_ref, pen_ref, nd_ref, w_ref, s_ref, m_ref, l_ref, acc_ref, *, topk):
    _, qb, tk = key_ref.shape
    t = pl.program_id(1)
    nkt = t + 1
    kf = float(topk)
    row = lax.broadcasted_iota(I32, (qb, tk), 0)
    lane = lax.broadcasted_iota(I32, (qb, tk), 1)
    qchunk = (t * qb + row) >> CHUNK_SHIFT
    rel = row - lane

    def tile_geom(kt):
        col = kt * tk + lane
        visible = (col >> CHUNK_SHIFT) <= qchunk
        neg_dist = -jnp.abs(rel + (t - kt) * tk).astype(F32)
        return col, visible, neg_dist

    def row_off(kt):
        return pl.multiple_of(kt * tk, tk)

    @pl.when(nkt * tk <= topk)
    def _():
        def fill(kt, carry):
            _, visible, neg_dist = tile_geom(kt)
            pen_ref[kt] = jnp.where(visible, 0.0, NEG_BIG)
            nd_ref[kt] = neg_dist
            return carry

        lax.fori_loop(0, nkt, fill, 0)

    @pl.when(nkt * tk > topk)
    def _():
        qi = qi_ref[...]
        mq = mq_ref[...]
        qi_all = jnp.concatenate([qi[:, h * IDX_DIM:(h + 1) * IDX_DIM] for h in range(IDX_HEADS)],
                                 axis=0)
        for h in range(IDX_HEADS):
            w_ref[h] = jnp.broadcast_to(mq[:, IDX_DIM + h:IDX_DIM + h + 1], (qb, LANES))

        def score_tile(kt, carry):
            k_idx = mk_ref[pl.ds(row_off(kt), tk), 0:IDX_DIM].astype(BF16)
            logits = lax.dot_general(qi_all, k_idx, _NT, preferred_element_type=F32)
            parts = [jnp.zeros((qb, LANES), F32)] * (tk // LANES)
            for h in range(IDX_HEADS):
                w = w_ref[h]
                parts = [p + w * jnp.maximum(lt, 0.0)
                         for p, lt in zip(parts, _lane_tiles(logits[h * qb:(h + 1) * qb]))]
            _, visible, neg_dist = tile_geom(kt)
            score = jnp.where(visible, jnp.concatenate(parts, axis=-1), -jnp.inf)
            bits = pltpu.bitcast(score, I32)
            key = bits ^ ((bits >> 31) & np.int32(0x7FFFFFFF))
            key_ref[kt] = key
            hi_ref[kt] = (key >> 16).astype(I16)
            lo_ref[kt] = ((key & 0xFFFF) - 32768).astype(I16)
            nd_ref[kt] = neg_dist
            return carry

        lax.fori_loop(0, nkt, score_tile, 0)

        t_hi = _radix4_select(hi_ref, nkt, jnp.full((qb, LANES), kf, F32)) - 32768
        t_hi16 = t_hi.astype(I16)

        def tie_tile(kt, acc):
            masked = []
            for ht, lt in zip(_lane_tiles(hi_ref[kt]), _lane_tiles(lo_ref[kt])):
                masked.append(jnp.where(ht == t_hi16, lt, np.int16(-32768)))
                acc = acc + jnp.where(ht > t_hi16, np.int16(1), np.int16(0))
            lo_ref[kt] = jnp.concatenate(masked, axis=-1)
            return acc

        above = lax.fori_loop(0, nkt, tie_tile, jnp.zeros((qb, LANES), I16))
        need_lo = kf - _lane_total(above.astype(F32))
        thr = (t_hi << 16) | _radix4_select(lo_ref, nkt, need_lo)

        def pen_tile(kt, carry):
            n_gt, n_eq = carry
            _, visible, _ = tile_geom(kt)
            pens = []
            for key, vis in zip(_lane_tiles(key_ref[kt]), _lane_tiles(visible)):
                pens.append(jnp.where((key >= thr) & vis, 0.0, NEG_BIG))
                n_gt = n_gt + jnp.where(key > thr, 1.0, 0.0)
                n_eq = n_eq + jnp.where(key == thr, 1.0, 0.0)
            pen_ref[kt] = jnp.concatenate(pens, axis=-1)
            return n_gt, n_eq

        zero = jnp.zeros((qb, LANES), F32)
        n_gt, n_eq = lax.fori_loop(0, nkt, pen_tile, (zero, zero))
        need = kf - _lane_total(n_gt)
        n_eq = _lane_total(n_eq)

        tie_rows = jnp.where((n_eq > need) & (thr > KEY_NEG_INF), 1.0, 0.0)

        @pl.when(jnp.max(tie_rows) > 0.0)
        def _():
            nbits = int(key_ref.shape[0] * tk - 1).bit_length()
            thr1 = thr[:, 0:1]
            need1 = need[:, 0:1]

            def idx_step(i, p):
                trial = p | lax.shift_left(np.int32(1), nbits - 1 - i)

                def count_tile(kt, acc):
                    col, _, _ = tile_geom(kt)
                    hit = (key_ref[kt] == thr1) & (col < trial)
                    return acc + _fold_lanes(jnp.where(hit, 1.0, 0.0))

                acc = lax.fori_loop(0, nkt, count_tile, jnp.zeros((qb, LANES), F32))
                cnt = jnp.sum(acc, axis=-1, keepdims=True)
                return jnp.where(cnt < need1, trial, p)

            last = lax.fori_loop(0, nbits, idx_step, jnp.zeros((qb, 1), I32))

            def repen_tile(kt, carry):
                key = key_ref[kt]
                col, visible, _ = tile_geom(kt)
                sel = (key > thr1) | ((key == thr1) & (col <= last))
                pen_ref[kt] = jnp.where(sel & visible, 0.0, NEG_BIG)
                return carry

            lax.fori_loop(0, nkt, repen_tile, 0)

    qs = (q_ref[...].astype(F32) * ATT_SCALE).astype(BF16)
    q_all = jnp.concatenate([qs[:, h * A_HEAD_DIM:(h + 1) * A_HEAD_DIM] for h in range(A_HEADS)],
                            axis=0)
    for h in range(A_HEADS):
        m_ref[h] = jnp.full((qb, LANES), NEG_BIG, F32)
        l_ref[h] = jnp.zeros((qb, LANES), F32)
        acc_ref[h] = jnp.zeros((qb, A_HEAD_DIM), F32)

    def logit_tile(kt, carry):
        kk = kv_ref[pl.ds(row_off(kt), tk), 0:A_HEAD_DIM]
        pen = _lane_tiles(pen_ref[kt])
        nd = _lane_tiles(nd_ref[kt])
        s_all = lax.dot_general(q_all, kk, _NT, preferred_element_type=F32)
        for h in range(A_HEADS):
            slope = float(2.0 ** (-8.0 * (h + 1) / A_HEADS))
            s = s_all[h * qb:(h + 1) * qb]
            st = [a + (slope * b + c) for a, b, c in zip(_lane_tiles(s), nd, pen)]
            s_ref[h, kt] = jnp.concatenate(st, axis=-1)
            tile_max = m_ref[h]
            for x in st:
                tile_max = jnp.maximum(tile_max, x)
            m_ref[h] = tile_max
        return carry

    lax.fori_loop(0, nkt, logit_tile, 0)
    for h in range(A_HEADS):
        m_ref[h] = jnp.broadcast_to(jnp.max(m_ref[h], axis=-1, keepdims=True), (qb, LANES))

    def prob_tile(kt, carry):
        vv = kv_ref[pl.ds(row_off(kt), tk), A_HEAD_DIM:2 * A_HEAD_DIM]
        probs = []
        for h in range(A_HEADS):
            m = m_ref[h]
            ps = [jnp.exp(x - m) for x in _lane_tiles(s_ref[h, kt])]
            psum = l_ref[h]
            for x in ps:
                psum = psum + x
            l_ref[h] = psum
            probs.append(jnp.concatenate(ps, axis=-1).astype(BF16))
        pv = jnp.dot(jnp.concatenate(probs, axis=0), vv, preferred_element_type=F32)
        for h in range(A_HEADS):
            acc_ref[h] = acc_ref[h] + pv[h * qb:(h + 1) * qb]
        return carry

    lax.fori_loop(0, nkt, prob_tile, 0)
    outs = [acc_ref[h] / jnp.sum(l_ref[h], axis=-1, keepdims=True) for h in range(A_HEADS)]
    o_ref[...] = jnp.concatenate(outs, axis=-1).astype(BF16)


def _attention(q, qi, misc, kv, batch, seq):
    n = q.shape[0]
    qb = min(Q_BLOCK, seq)
    nb = seq // qb
    topk = min(TOPK_MAX, seq // 4)
    assert seq // LANES < (1 << SEL_FIELD_BITS)
    rowq = lambda b, t: (b * nb + t, 0)
    rowk = lambda b, t: (b, 0)
    tiles = (nb, qb, qb)
    return pl.pallas_call(
        functools.partial(_attn_kernel, topk=topk),
        grid=(batch, nb),
        in_specs=[pl.BlockSpec((qb, A_WIDTH), rowq),
                  pl.BlockSpec((qb, IDX_HEADS * IDX_DIM), rowq),
                  pl.BlockSpec((qb, LANES), rowq),
                  pl.BlockSpec((seq, LANES), rowk),
                  pl.BlockSpec((seq, 2 * A_HEAD_DIM), rowk)],
        out_specs=pl.BlockSpec((qb, A_WIDTH), rowq),
        out_shape=jax.ShapeDtypeStruct((n, A_WIDTH), BF16),
        scratch_shapes=[pltpu.VMEM(tiles, I32), pltpu.VMEM(tiles, I16), pltpu.VMEM(tiles, I16),
                        pltpu.VMEM(tiles, F32), pltpu.VMEM(tiles, F32),
                        pltpu.VMEM((IDX_HEADS, qb, LANES), F32),
                        pltpu.VMEM((A_HEADS,) + tiles, F32),
                        pltpu.VMEM((A_HEADS, qb, LANES), F32),
                        pltpu.VMEM((A_HEADS, qb, LANES), F32),
                        pltpu.VMEM((A_HEADS, qb, A_HEAD_DIM), F32)],
        compiler_params=_params("parallel", "parallel"),
        name="dsa_attention",
    )(q, qi, misc, misc, kv)


def _out_proj_kernel(x_ref, a_ref, b_ref, w_ref, mod_ref, o_ref):
    y = jnp.dot(a_ref[...], w_ref[0:A_WIDTH, :], preferred_element_type=F32)
    y = y + jnp.dot(b_ref[...], w_ref[A_WIDTH:, :], preferred_element_type=F32)
    o_ref[...] = x_ref[...] + mod_ref[0, 2:3, :] * y


def _out_proj(x, a, bo, w_o, mod_l, batch, seq):
    n, d = x.shape
    tm = 512
    tps = seq // tm
    row = lambda b, t: (b * tps + t, 0)
    return pl.pallas_call(
        _out_proj_kernel,
        grid=(batch, tps),
        in_specs=[pl.BlockSpec((tm, d), row),
                  pl.BlockSpec((tm, A_WIDTH), row),
                  pl.BlockSpec((tm, B_WIDTH), row),
                  pl.BlockSpec(w_o.shape, lambda b, t: (0, 0)),
                  pl.BlockSpec((1, 6, d), lambda b, t: (b, 0, 0))],
        out_specs=pl.BlockSpec((tm, d), row),
        out_shape=jax.ShapeDtypeStruct((n, d), F32),
        compiler_params=_params("parallel", "parallel"),
        name="out_proj",
    )(x, a, bo, w_o, mod_l)


def _pool_kernel(x_ref, halo_ref, mod_ref, g_ref, w_ref, sc_ref, o_ref):
    tm, d = x_ref.shape
    t = pl.program_id(1)
    g = g_ref[...]
    shift = mod_ref[0, 0:1, :]
    scale = mod_ref[0, 1:2, :]
    x = x_ref[...]
    h = _norm_mod(x, g, shift, scale)
    hh = _norm_mod(halo_ref[...], g, shift, scale)
    hh = jnp.where(t > 0, hh, 0.0)
    ext = jnp.concatenate([hh, h], axis=0)
    pos = t * tm + lax.broadcasted_iota(I32, (tm, 1), 0)
    gd = d // len(POOL_WINDOWS)
    ys = []
    for gi, win in enumerate(POOL_WINDOWS):
        cols = slice(gi * gd, (gi + 1) * gd)
        cur = ext[:, cols]
        k = 1
        while k < win:
            cur = cur + pltpu.roll(cur, k, axis=0)
            k *= 2
        cnt = jnp.minimum(pos + 1, win).astype(F32)
        y = (cur[POOL_HALO:, :] / cnt - h[:, cols]).astype(BF16)
        ys.append(jnp.dot(y, w_ref[gi], preferred_element_type=F32))
    y = jnp.concatenate(ys, axis=-1) * sc_ref[...]
    o_ref[...] = x + mod_ref[0, 2:3, :] * y


def _pool_mixer(x, mod_l, g, w_pool, scale, batch, seq):
    n, d = x.shape
    tm = 512
    tps = seq // tm
    row = lambda b, t: (b * tps + t, 0)
    hpt = tm // POOL_HALO
    halo = lambda b, t: (jnp.maximum((b * tps + t) * hpt - 1, 0), 0)
    return pl.pallas_call(
        _pool_kernel,
        grid=(batch, tps),
        in_specs=[pl.BlockSpec((tm, d), row),
                  pl.BlockSpec((POOL_HALO, d), halo),
                  pl.BlockSpec((1, 6, d), lambda b, t: (b, 0, 0)),
                  pl.BlockSpec((1, d), lambda b, t: (0, 0)),
                  pl.BlockSpec(w_pool.shape, lambda b, t: (0, 0, 0)),
                  pl.BlockSpec((1, d), lambda b, t: (0, 0))],
        out_specs=pl.BlockSpec((tm, d), row),
        out_shape=jax.ShapeDtypeStruct((n, d), F32),
        compiler_params=_params("parallel", "parallel"),
        name="pool_mixer",
    )(x, x, mod_l, g, w_pool, scale)


_R_EXP0 = N_GROUPS
RT_TM = 512
SEG_ALIGN = 16
SEG_SHIFT = 4
E_TM = 1152

_TN = (((0,), (0,)), ((), ()))


def _moe_layout(n):
    tm = min(RT_TM, n)
    xs_rows = tm + N_GROUPS * SEG_ALIGN
    e_tm = E_TM if n >= 4 * E_TM else 128
    worst = n + (n // tm) * N_GROUPS * SEG_ALIGN + N_GROUPS * e_tm
    rows = -(-worst // e_tm) * e_tm
    return tm, xs_rows, e_tm, rows


def _perm_matrix(d_row, rows):
    tm = d_row.shape[1]
    return jnp.where(lax.broadcasted_iota(I32, (rows, tm), 0) == d_row, 1.0, 0.0).astype(BF16)


def _split_bf16(a):
    hi = a.astype(BF16)
    return hi, (a - hi.astype(F32)).astype(BF16)


def _route_kernel(x_ref, mod_ref, g_ref, whi_ref, wlo_ref, b_ref, tri_ref,
                  xs_ref, gs_ref, d_ref, cnt_ref):
    tm = x_ref.shape[0]
    xs_rows = xs_ref.shape[0]
    h = _norm_mod(x_ref[...], g_ref[...], mod_ref[0, 3:4, :], mod_ref[0, 4:5, :])
    h_hi, h_lo = _split_bf16(h)
    logits = (jnp.dot(h_hi, wlo_ref[...], preferred_element_type=F32)
              + jnp.dot(h_lo, whi_ref[...], preferred_element_type=F32)
              + jnp.dot(h_hi, whi_ref[...], preferred_element_type=F32)) + b_ref[...]
    lane = lax.broadcasted_iota(I32, (tm, LANES), 1)
    big = np.int32(LANES)

    def top1(vals):
        m = jnp.max(vals, axis=-1, keepdims=True)
        idx = jnp.min(jnp.where(vals == m, lane, big), axis=-1, keepdims=True)
        return m, idx

    gl = jnp.where(lane < N_GROUPS, logits, -jnp.inf)
    gmax, gsel = top1(gl)
    p_g = 1.0 / jnp.sum(jnp.exp(gl - gmax), axis=-1, keepdims=True)
    e_lo = _R_EXP0 + gsel * EXPERTS_PER_GROUP
    el = jnp.where((lane >= e_lo) & (lane < e_lo + EXPERTS_PER_GROUP), logits, -jnp.inf)
    v1, i1 = top1(el)
    v2, i2 = top1(jnp.where(lane == i1, -jnp.inf, el))
    e2 = jnp.exp(v2 - v1)
    g1 = p_g / (1.0 + e2)
    g2 = p_g * e2 / (1.0 + e2)
    cg = jnp.where(lane == i1 - e_lo, g1, 0.0) + jnp.where(lane == i2 - e_lo, g2, 0.0)

    member = jnp.where(lane == gsel, 1.0, 0.0).T[0:8, :]
    rank = jnp.dot(member.astype(BF16), tri_ref[...], preferred_element_type=F32)
    count = jnp.sum(member, axis=1, keepdims=True)
    cnt_ref[0] = jnp.broadcast_to(count, (8, LANES))
    padded = jnp.ceil(count * (1.0 / SEG_ALIGN)) * SEG_ALIGN
    d_row = jnp.zeros((1, tm), F32)
    seg = jnp.zeros((1, 1), F32)
    for g in range(N_GROUPS):
        d_row = d_row + member[g:g + 1, :] * (rank[g:g + 1, :] + seg)
        seg = seg + padded[g:g + 1, :]
    d_row = d_row.astype(I32)
    d_ref[0] = d_row

    perm = _perm_matrix(d_row, xs_rows)
    xs_ref[...] = jnp.dot(perm, h_hi, preferred_element_type=F32).astype(BF16)
    cg_hi, cg_lo = _split_bf16(cg)
    gs_ref[...] = (jnp.dot(perm, cg_hi, preferred_element_type=F32)
                   + jnp.dot(perm, cg_lo, preferred_element_type=F32))


def _route(x, mod_l, g, w_r, b_r, tri, batch, seq):
    n, d = x.shape
    tm, xs_rows, _, _ = _moe_layout(n)
    nt = n // tm
    tps = seq // tm
    full = lambda i: (0, 0)
    return pl.pallas_call(
        _route_kernel,
        grid=(nt,),
        in_specs=[pl.BlockSpec((tm, d), lambda i: (i, 0)),
                  pl.BlockSpec((1, 6, d), lambda i: (i // tps, 0, 0)),
                  pl.BlockSpec((1, d), full),
                  pl.BlockSpec((d, LANES), full),
                  pl.BlockSpec((d, LANES), full),
                  pl.BlockSpec((1, LANES), full),
                  pl.BlockSpec((tm, tm), full)],
        out_specs=[pl.BlockSpec((xs_rows, d), lambda i: (i, 0)),
                   pl.BlockSpec((xs_rows, LANES), lambda i: (i, 0)),
                   pl.BlockSpec((1, 1, tm), lambda i: (i, 0, 0)),
                   pl.BlockSpec((1, 8, LANES), lambda i: (i, 0, 0))],
        out_shape=[jax.ShapeDtypeStruct((nt * xs_rows, d), BF16),
                   jax.ShapeDtypeStruct((nt * xs_rows, LANES), F32),
                   jax.ShapeDtypeStruct((nt, 1, tm), I32),
                   jax.ShapeDtypeStruct((nt, 8, LANES), F32)],
        compiler_params=_params("parallel"),
        name="route_sort",
    )(x, mod_l, g, *_split_bf16(w_r), b_r, tri)


def _dispatch_kernel(src_ref, dst_ref, len_ref, fill_ref, xt_ref, gt_ref, xs_hbm, gs_hbm,
                     zx_ref, zg_ref, sem_x, sem_g):
    i = pl.program_id(0)
    nt = src_ref.shape[0]

    def chunk_copies(x_src, g_src, src_row, dst_row):
        src_row = pl.multiple_of(src_row, SEG_ALIGN)
        dst_row = pl.multiple_of(dst_row, SEG_ALIGN)
        return (pltpu.make_async_copy(x_src.at[pl.ds(src_row, SEG_ALIGN)],
                                      xs_hbm.at[pl.ds(dst_row, SEG_ALIGN)], sem_x),
                pltpu.make_async_copy(g_src.at[pl.ds(src_row, SEG_ALIGN)],
                                      gs_hbm.at[pl.ds(dst_row, SEG_ALIGN)], sem_g))

    def run(x_src, g_src, jobs):
        total = jnp.int32(0)
        for src, dst, rows in jobs:
            nch = rows >> SEG_SHIFT

            def issue(k, carry, src=src, dst=dst):
                for cp in chunk_copies(x_src, g_src, src + k * SEG_ALIGN, dst + k * SEG_ALIGN):
                    cp.start()
                return carry

            lax.fori_loop(0, nch, issue, 0)
            total = total + nch

        def drain(k, carry):
            for cp in chunk_copies(x_src, g_src, 0, 0):
                cp.wait()
            return carry

        lax.fori_loop(0, total, drain, 0)

    @pl.when(i < nt)
    def _():
        t = jnp.minimum(i, nt - 1)
        run(xt_ref, gt_ref, [(src_ref[t, g], dst_ref[t, g], len_ref[t, g]) for g in range(N_GROUPS)])

    @pl.when(i == nt)
    def _():
        zx_ref[...] = jnp.zeros_like(zx_ref)
        zg_ref[...] = jnp.zeros_like(zg_ref)
        for g in range(N_GROUPS + 1):
            start = fill_ref[0, g]
            nch = fill_ref[1, g] >> SEG_SHIFT

            def issue(k, carry, start=start):
                for cp in chunk_copies(zx_ref, zg_ref, 0, start + k * SEG_ALIGN):
                    cp.start()
                return carry

            lax.fori_loop(0, nch, issue, 0)

            def drain(k, carry):
                for cp in chunk_copies(zx_ref, zg_ref, 0, 0):
                    cp.wait()
                return carry

            lax.fori_loop(0, nch, drain, 0)


def _dispatch(xt, gt, src_off, dst_off, seg_len, fill, rows):
    d = xt.shape[1]
    nt = src_off.shape[0]
    xs_rows = xt.shape[0] // nt
    tile = lambda i, *_: (jnp.minimum(i, nt - 1), 0)
    return pl.pallas_call(
        _dispatch_kernel,
        grid_spec=pltpu.PrefetchScalarGridSpec(
            num_scalar_prefetch=4,
            grid=(nt + 1,),
            in_specs=[pl.BlockSpec((xs_rows, d), tile), pl.BlockSpec((xs_rows, LANES), tile)],
            out_specs=[pl.BlockSpec(memory_space=pl.ANY), pl.BlockSpec(memory_space=pl.ANY)],
            scratch_shapes=[pltpu.VMEM((SEG_ALIGN, d), BF16), pltpu.VMEM((SEG_ALIGN, LANES), F32),
                            pltpu.SemaphoreType.DMA, pltpu.SemaphoreType.DMA]),
        out_shape=[jax.ShapeDtypeStruct((rows, d), BF16),
                   jax.ShapeDtypeStruct((rows, LANES), F32)],
        compiler_params=_params("arbitrary"),
        name="dispatch",
    )(src_off, dst_off, seg_len, fill, xt, gt)


def _experts_kernel(grp_ref, valid_ref, x_ref, gs_ref, wg_ref, wu_ref, wd_ref, o_ref, acc_ref):
    j = pl.program_id(0)
    e = pl.program_id(1)

    @pl.when(e == 0)
    def _():
        acc_ref[...] = jnp.zeros_like(acc_ref)

    @pl.when(valid_ref[j] > 0)
    def _():
        x = x_ref[...]
        hg = jnp.dot(x, wg_ref[0, 0].astype(BF16), preferred_element_type=F32)
        hu = jnp.dot(x, wu_ref[0, 0].astype(BF16), preferred_element_type=F32)
        lane = lax.broadcasted_iota(I32, gs_ref.shape, 1)
        gate = jnp.sum(jnp.where(lane == e, gs_ref[...], 0.0), axis=-1, keepdims=True)
        act = (hg * jax.nn.sigmoid(hg)) * hu * gate
        acc_ref[...] += jnp.dot(act.astype(BF16), wd_ref[0, 0].astype(BF16),
                                preferred_element_type=F32)

    @pl.when(e == pl.num_programs(1) - 1)
    def _():
        o_ref[...] = acc_ref[...]


def _experts(xs, gs, tile_grp, tile_valid, wg, wu, wd, layer, e_tm):
    rows, d = xs.shape
    f = wg.shape[-1]
    ntile = tile_grp.shape[0]
    row = lambda j, e, grp, valid: (j, 0)
    wsel = lambda j, e, grp, valid: (
        layer, grp[j] * EXPERTS_PER_GROUP + jnp.where(valid[j] > 0, e, EXPERTS_PER_GROUP - 1), 0, 0)
    return pl.pallas_call(
        _experts_kernel,
        grid_spec=pltpu.PrefetchScalarGridSpec(
            num_scalar_prefetch=2,
            grid=(ntile, EXPERTS_PER_GROUP),
            in_specs=[pl.BlockSpec((e_tm, d), row),
                      pl.BlockSpec((e_tm, LANES), row),
                      pl.BlockSpec((1, 1, d, f), wsel),
                      pl.BlockSpec((1, 1, d, f), wsel),
                      pl.BlockSpec((1, 1, f, d), wsel)],
            out_specs=pl.BlockSpec((e_tm, d), row),
            scratch_shapes=[pltpu.VMEM((e_tm, d), F32)]),
        out_shape=jax.ShapeDtypeStruct((rows, d), F32),
        compiler_params=_params("arbitrary", "arbitrary"),
        name="experts",
    )(tile_grp, tile_valid, xs, gs, wg, wu, wd)


def _combine_kernel(meta_ref, x_ref, d_ref, mod_ref, gout_ref, ys_hbm, o_ref, yseg, sem, *, final_norm):
    i = pl.program_id(0)
    nt = pl.num_programs(0)
    xs_rows = yseg.shape[1]

    def seg_copy(slot, src_row, dst_row):
        src_row = pl.multiple_of(src_row, SEG_ALIGN)
        dst_row = pl.multiple_of(dst_row, SEG_ALIGN)
        return pltpu.make_async_copy(ys_hbm.at[pl.ds(src_row, SEG_ALIGN)],
                                     yseg.at[slot, pl.ds(dst_row, SEG_ALIGN)], sem.at[slot])

    def fetch(tile, slot):
        yseg[slot] = jnp.zeros(yseg.shape[1:], F32)
        seg = jnp.int32(0)
        for g in range(N_GROUPS):
            base = meta_ref[tile, g]
            c16 = meta_ref[tile, N_GROUPS + g]

            def issue(k, carry, seg=seg, base=base):
                seg_copy(slot, base + k * SEG_ALIGN, seg + k * SEG_ALIGN).start()
                return carry

            lax.fori_loop(0, c16 >> SEG_SHIFT, issue, 0)
            seg = seg + c16

    @pl.when(i == 0)
    def _():
        fetch(0, 0)

    @pl.when(i + 1 < nt)
    def _():
        fetch(jnp.minimum(i + 1, nt - 1), (i + 1) % 2)

    slot = i % 2
    total_chunks = jnp.int32(0)
    for g in range(N_GROUPS):
        total_chunks = total_chunks + (meta_ref[i, N_GROUPS + g] >> SEG_SHIFT)

    def drain(k, carry):
        seg_copy(slot, 0, 0).wait()
        return carry

    lax.fori_loop(0, total_chunks, drain, 0)

    perm = _perm_matrix(d_ref[0], xs_rows)
    y_hi, y_lo = _split_bf16(yseg[slot])
    y = (lax.dot_general(perm, y_hi, _TN, preferred_element_type=F32)
         + lax.dot_general(perm, y_lo, _TN, preferred_element_type=F32))
    out = x_ref[...] + mod_ref[0, 5:6, :] * y
    if final_norm:
        out = out * lax.rsqrt(jnp.mean(out * out, axis=-1, keepdims=True) + EPS) * gout_ref[...]
    o_ref[...] = out


def _combine(x, d_rows, meta, ys, mod_l, g_out, batch, seq, final_norm):
    n, d = x.shape
    tm, xs_rows, _, _ = _moe_layout(n)
    tps = seq // tm
    return pl.pallas_call(
        functools.partial(_combine_kernel, final_norm=final_norm),
        grid_spec=pltpu.PrefetchScalarGridSpec(
            num_scalar_prefetch=1,
            grid=(n // tm,),
            in_specs=[pl.BlockSpec((tm, d), lambda i, meta: (i, 0)),
                      pl.BlockSpec((1, 1, tm), lambda i, meta: (i, 0, 0)),
                      pl.BlockSpec((1, 6, d), lambda i, meta: (i // tps, 0, 0)),
                      pl.BlockSpec((1, d), lambda i, meta: (0, 0)),
                      pl.BlockSpec(memory_space=pl.ANY)],
            out_specs=pl.BlockSpec((tm, d), lambda i, meta: (i, 0)),
            scratch_shapes=[pltpu.VMEM((2, xs_rows, d), F32), pltpu.SemaphoreType.DMA((2,))]),
        out_shape=jax.ShapeDtypeStruct((n, d), F32),
        compiler_params=_params("arbitrary"),
        name="combine",
    )(meta, x, d_rows, mod_l, g_out, ys)


def _moe_plan(cnt, n):
    _, _, e_tm, rows = _moe_layout(n)
    seg_len = ((cnt + (SEG_ALIGN - 1)) // SEG_ALIGN) * SEG_ALIGN
    totals = jnp.sum(seg_len, axis=0)
    region = ((totals + (e_tm - 1)) // e_tm) * e_tm
    region_end = jnp.cumsum(region)
    region_start = region_end - region
    src_off = jnp.cumsum(seg_len, axis=1) - seg_len
    dst_off = region_start[None, :] + jnp.cumsum(seg_len, axis=0) - seg_len
    fill_start = jnp.concatenate([region_start + totals, region_end[-1:]])
    fill_len = jnp.concatenate([region - totals, rows - region_end[-1:]])
    fill = jnp.stack([fill_start, fill_len]).astype(I32)
    j = jnp.arange(rows // e_tm, dtype=I32) * e_tm
    valid = j < region_end[-1]
    grp = jnp.minimum(jnp.sum(j[:, None] >= region_end[None, :], axis=1), N_GROUPS - 1)
    last_grp = jnp.max(jnp.where(region > 0, jnp.arange(N_GROUPS), 0))
    grp = jnp.where(valid, grp, last_grp).astype(I32)
    return (src_off.astype(I32), dst_off.astype(I32), seg_len.astype(I32), fill, grp,
            valid.astype(I32))


def _moe(x, mod_l, g_ffn, w_r, b_r, tri, wg, wu, wd, layer, g_out, batch, seq, final_norm):
    n = x.shape[0]
    _, _, e_tm, rows = _moe_layout(n)
    xt, gt, d_rows, cnt = _route(x, mod_l, g_ffn, w_r, b_r, tri, batch, seq)
    cnt = cnt[:, 0:N_GROUPS, 0].astype(I32)
    src_off, dst_off, seg_len, fill, grp, valid = _moe_plan(cnt, n)
    xs, gs = _dispatch(xt, gt, src_off, dst_off, seg_len, fill, rows)
    ys = _experts(xs, gs, grp, valid, wg, wu, wd, layer, e_tm)
    meta = jnp.concatenate([dst_off, seg_len], axis=1)
    return _combine(x, d_rows, meta, ys, mod_l, g_out, batch, seq, final_norm)


def kernel(x, c, w_mod, b_mod, norm_mix, norm_ffn, w_in, kv_norm, w_kv_up, sgu_norm, w_s, b_s, w_o,
           w_pool, pool_scale, w_gr, b_gr, w_er, b_er, w_gate, w_up, w_down, norm_out):
    batch, seq, d = x.shape
    depth = w_mod.shape[0]
    n = batch * seq
    xs = x.reshape(n, d)
    mod = _modulation(c, w_mod, b_mod).reshape(depth, batch, 6, d)

    pos = np.arange(SGU_CHUNK)
    sgu_mask = jnp.asarray((pos[None, :] // CHUNK) <= (pos[:, None] // CHUNK))
    sizes = (A_WIDTH, KV_LATENT, IDX_HEADS * IDX_DIM, IDX_DIM, IDX_HEADS, B_WIDTH, B_WIDTH)
    offs = np.concatenate([[0], np.cumsum(sizes)])
    pad = LANES - IDX_DIM - IDX_HEADS
    rt_tm = _moe_layout(n)[0]
    tri = jnp.asarray(np.triu(np.ones((rt_tm, rt_tm), np.float32), 1), BF16)

    for i in range(depth):
        j = i // 2
        mod_l = mod[i]
        g_mix = norm_mix[i].reshape(1, d)
        if i % 2 == 0:
            wi = w_in[j]
            w_in_r = jnp.concatenate(
                [wi[:, offs[0]:offs[3]], wi[:, offs[3]:offs[5]], jnp.zeros((d, pad), wi.dtype),
                 wi[:, offs[5]:offs[7]]], axis=1).astype(BF16)
            w_s_m = jnp.where(sgu_mask[None], w_s[j], 0.0).astype(BF16)
            b_s_b = jnp.broadcast_to(b_s[j][:, :, None], (B_GROUPS, SGU_CHUNK, B_WIDTH // B_GROUPS))
            q, kv, qi, misc, bo = _in_proj(
                xs, mod_l, g_mix, w_in_r, kv_norm[j].reshape(1, -1), w_kv_up[j].astype(BF16),
                sgu_norm[j].reshape(1, -1), w_s_m, b_s_b, batch, seq)
            a = _attention(q, qi, misc, kv, batch, seq)
            xs = _out_proj(xs, a, bo, w_o[j].astype(BF16), mod_l, batch, seq)
        else:
            xs = _pool_mixer(xs, mod_l, g_mix, w_pool[j].astype(BF16), pool_scale[j].reshape(1, d),
                             batch, seq)
        w_r = jnp.concatenate([w_gr[i], w_er[i],
                               jnp.zeros((d, LANES - N_GROUPS - N_EXPERTS), F32)], axis=1)
        b_r = jnp.concatenate([b_gr[i], b_er[i],
                               jnp.zeros((LANES - N_GROUPS - N_EXPERTS,), F32)]).reshape(1, LANES)
        xs = _moe(xs, mod_l, norm_ffn[i].reshape(1, d), w_r, b_r, tri, w_gate, w_up, w_down, i,
                  norm_out.reshape(1, d), batch, seq, final_norm=(i == depth - 1))
    return xs.reshape(batch, seq, d)
```

```python
import functools

import numpy as np
import jax
import jax.numpy as jnp
from jax import lax
from jax.experimental import pallas as pl
from jax.experimental.pallas import tpu as pltpu

F32 = jnp.float32
BF16 = jnp.bfloat16
I32 = jnp.int32

EPS = 1e-6
CHUNK = 64
CHUNK_SHIFT = 6
Q_BLOCK = 256
A_HEADS = 8
A_HEAD_DIM = 64
A_WIDTH = A_HEADS * A_HEAD_DIM
KV_LATENT = 128
IDX_HEADS = 8
IDX_DIM = 64
TOPK_MAX = 256
B_GROUPS = 4
SGU_CHUNK = 128
POOL_WINDOWS = (2, 4, 8, 16)
POOL_HALO = 16
N_GROUPS = 4
EXPERTS_PER_GROUP = 8
N_EXPERTS = N_GROUPS * EXPERTS_PER_GROUP
LANES = 128
NEG_BIG = -1e30
INT_MIN = np.int32(-2 ** 31)
KEY_NEG_INF = np.int32(np.int32(-8388608) ^ np.int32(0x7FFFFFFF))
VMEM_LIMIT = 56 * 1024 * 1024

_NT = (((1,), (1,)), ((), ()))


def _params(*sem):
    return pltpu.CompilerParams(dimension_semantics=sem, vmem_limit_bytes=VMEM_LIMIT)


def _norm_mod(x, g, shift, scale):
    y = x * lax.rsqrt(jnp.mean(x * x, axis=-1, keepdims=True) + EPS)
    return (y * g) * (1.0 + scale) + shift


def _mod_kernel(c_ref, w_ref, b_ref, o_ref):
    c = c_ref[...]
    cs = c * jax.nn.sigmoid(c)
    o_ref[0] = jnp.dot(cs.astype(BF16), w_ref[0].astype(BF16),
                       preferred_element_type=F32) + b_ref[0]


def _modulation(c, w_mod, b_mod):
    depth, d, d6 = w_mod.shape
    b = c.shape[0]
    tn = 1024
    return pl.pallas_call(
        _mod_kernel,
        grid=(depth, d6 // tn),
        in_specs=[pl.BlockSpec((b, d), lambda i, j: (0, 0)),
                  pl.BlockSpec((1, d, tn), lambda i, j: (i, 0, j)),
                  pl.BlockSpec((1, 1, tn), lambda i, j: (i, 0, j))],
        out_specs=pl.BlockSpec((1, b, tn), lambda i, j: (i, 0, j)),
        out_shape=jax.ShapeDtypeStruct((depth, b, d6), F32),
        compiler_params=_params("parallel", "parallel"),
        name="modulation",
    )(c, w_mod, b_mod.reshape(depth, 1, d6))


_C_Q = 0
_C_KV = _C_Q + A_WIDTH
_C_QI = _C_KV + KV_LATENT
_C_MISC = _C_QI + IDX_HEADS * IDX_DIM
_C_U = _C_MISC + LANES
B_WIDTH = 512
_C_V = _C_U + B_WIDTH
_C_END = _C_V + B_WIDTH


def _in_proj_kernel(x_ref, mod_ref, g_ref, w_ref, kvn_ref, wkv_ref, lng_ref, ws_ref, bs_ref,
                    q_ref, kv_ref, qi_ref, misc_ref, b_ref):
    tm = x_ref.shape[0]
    h = _norm_mod(x_ref[...], g_ref[...], mod_ref[0, 0:1, :], mod_ref[0, 1:2, :]).astype(BF16)

    def proj(lo, hi):
        return jnp.dot(h, w_ref[:, lo:hi], preferred_element_type=F32)

    q_ref[...] = proj(_C_Q, _C_KV).astype(BF16)
    qi_ref[...] = proj(_C_QI, _C_MISC).astype(BF16)

    kvl = proj(_C_KV, _C_QI)
    kvn = kvl * lax.rsqrt(jnp.mean(kvl * kvl, axis=-1, keepdims=True) + EPS) * kvn_ref[...]
    kv_ref[...] = jnp.dot(kvn.astype(BF16), wkv_ref[...], preferred_element_type=F32).astype(BF16)

    lane = lax.broadcasted_iota(I32, (1, LANES), 1)
    idx_scale = (IDX_HEADS ** -0.5) * (IDX_DIM ** -0.5)
    w_lane = (lane >= IDX_DIM) & (lane < IDX_DIM + IDX_HEADS)
    misc_ref[...] = proj(_C_MISC, _C_U) * jnp.where(w_lane, idx_scale, 1.0)

    gu = jax.nn.gelu(proj(_C_U, _C_V))
    gv = jax.nn.gelu(proj(_C_V, _C_END))
    mu = jnp.mean(gv, axis=-1, keepdims=True)
    cen = gv - mu
    var = jnp.mean(cen * cen, axis=-1, keepdims=True)
    vn = (cen * lax.rsqrt(var + EPS) * lng_ref[...]).astype(BF16)
    gd = B_WIDTH // B_GROUPS
    for r in range(tm // SGU_CHUNK):
        rows = slice(r * SGU_CHUNK, (r + 1) * SGU_CHUNK)
        for g in range(B_GROUPS):
            cols = slice(g * gd, (g + 1) * gd)
            mixed = jnp.dot(ws_ref[g], vn[rows, cols], preferred_element_type=F32) + bs_ref[g]
            b_ref[rows, cols] = (gu[rows, cols] * mixed).astype(BF16)


def _in_proj(x, mod_l, g, w_in_r, kv_norm, w_kv_up, sgu_norm, w_s_m, b_s_b, batch, seq):
    n, d = x.shape
    tm = 512
    tps = seq // tm
    row = lambda b, t: (b * tps + t, 0)
    full2 = lambda b, t: (0, 0)
    full3 = lambda b, t: (0, 0, 0)
    outs = [(A_WIDTH, BF16), (2 * A_HEAD_DIM, BF16), (IDX_HEADS * IDX_DIM, BF16), (LANES, F32),
            (B_WIDTH, BF16)]
    return pl.pallas_call(
        _in_proj_kernel,
        grid=(batch, tps),
        in_specs=[pl.BlockSpec((tm, d), row),
                  pl.BlockSpec((1, 6, d), lambda b, t: (b, 0, 0)),
                  pl.BlockSpec((1, d), full2),
                  pl.BlockSpec(w_in_r.shape, full2),
                  pl.BlockSpec((1, KV_LATENT), full2),
                  pl.BlockSpec(w_kv_up.shape, full2),
                  pl.BlockSpec((1, B_WIDTH), full2),
                  pl.BlockSpec(w_s_m.shape, full3),
                  pl.BlockSpec(b_s_b.shape, full3)],
        out_specs=[pl.BlockSpec((tm, w), row) for w, _ in outs],
        out_shape=[jax.ShapeDtypeStruct((n, w), dt) for w, dt in outs],
        compiler_params=_params("parallel", "parallel"),
        name="in_proj_sgu",
    )(x, mod_l, g, w_in_r, kv_norm, w_kv_up, sgu_norm, w_s_m, b_s_b)


SEL_FIELD_BITS = 10
SUBLANES = 8
ATT_SCALE = A_HEAD_DIM ** -0.5
assert ATT_SCALE == 2.0 ** round(np.log2(ATT_SCALE)), "folded into bf16 q, must be a power of two"


def _fold_rows(x, op=jnp.add):
    acc = x[0:SUBLANES]
    for r in range(1, x.shape[0] // SUBLANES):
        acc = op(acc, x[r * SUBLANES:(r + 1) * SUBLANES])
    return acc


def _radix4_select(key_ref, nkt, kf):
    _, tk, qb = key_ref.shape
    c1 = 1
    c2 = c1 + (1 << SEL_FIELD_BITS)
    c3 = c2 + (1 << (2 * SEL_FIELD_BITS))
    fmask = (1 << SEL_FIELD_BITS) - 1

    def step(i, prefix):
        unit = lax.shift_left(np.int32(1), 30 - 2 * i)
        ts = [(prefix + j * unit) ^ INT_MIN for j in (1, 2, 3)]

        def count_tile(kt, acc):
            key = key_ref[kt]
            hits = jnp.where(key >= ts[2], c3, jnp.where(key >= ts[1], c2, jnp.where(key >= ts[0], c1, 0)))
            return acc + _fold_rows(hits)

        acc = lax.fori_loop(0, nkt, count_tile, jnp.zeros((SUBLANES, qb), I32))
        cnt = [jnp.sum(((acc >> (j * SEL_FIELD_BITS)) & fmask).astype(F32), axis=0, keepdims=True)
               for j in range(3)]
        digit = jnp.where(cnt[2] >= kf, 3, jnp.where(cnt[1] >= kf, 2, jnp.where(cnt[0] >= kf, 1, 0)))
        return prefix + digit * unit

    return lax.fori_loop(0, 16, step, jnp.zeros((1, qb), I32)) ^ INT_MIN


def _attn_kernel(q_ref, qi_ref, mq_ref, mk_ref, kv_ref, o_ref,
                 key_ref, pen_ref, nd_ref, s_ref, m_ref, l_ref, acc_ref, *, topk):
    _, tk, qb = key_ref.shape
    t = pl.program_id(1)
    nkt = t + 1
    kf = float(topk)
    krow = lax.broadcasted_iota(I32, (tk, qb), 0)
    qlane = lax.broadcasted_iota(I32, (tk, qb), 1)
    qchunk = (t * qb + qlane) >> CHUNK_SHIFT
    rel = qlane - krow

    def tile_geom(kt):
        col = kt * tk + krow
        visible = (col >> CHUNK_SHIFT) <= qchunk
        neg_dist = -jnp.abs(rel + (t - kt) * tk).astype(F32)
        return col, visible, neg_dist

    def row_off(kt):
        return pl.multiple_of(kt * tk, tk)

    @pl.when(nkt * tk <= topk)
    def _():
        def fill(kt, carry):
            _, visible, neg_dist = tile_geom(kt)
            pen_ref[kt] = jnp.where(visible, 0.0, NEG_BIG)
            nd_ref[kt] = neg_dist
            return carry

        lax.fori_loop(0, nkt, fill, 0)

    @pl.when(nkt * tk > topk)
    def _():
        qi = qi_ref[...]
        qi_all = jnp.concatenate([qi[:, h * IDX_DIM:(h + 1) * IDX_DIM] for h in range(IDX_HEADS)],
                                 axis=0)
        mq_t = mq_ref[...].T
        w_rows = [mq_t[IDX_DIM + h:IDX_DIM + h + 1, :] for h in range(IDX_HEADS)]

        def score_tile(kt, carry):
            k_idx = mk_ref[pl.ds(row_off(kt), tk), 0:IDX_DIM].astype(BF16)
            logits = lax.dot_general(k_idx, qi_all, _NT, preferred_element_type=F32)
            score = jnp.zeros((tk, qb), F32)
            for h in range(IDX_HEADS):
                score = score + w_rows[h] * jnp.maximum(logits[:, h * qb:(h + 1) * qb], 0.0)
            _, visible, neg_dist = tile_geom(kt)
            score = jnp.where(visible, score, -jnp.inf)
            bits = pltpu.bitcast(score, I32)
            key_ref[kt] = bits ^ ((bits >> 31) & np.int32(0x7FFFFFFF))
            nd_ref[kt] = neg_dist
            return carry

        lax.fori_loop(0, nkt, score_tile, 0)

        thr = _radix4_select(key_ref, nkt, kf)

        def pen_tile(kt, carry):
            n_gt, n_eq = carry
            key = key_ref[kt]
            _, visible, _ = tile_geom(kt)
            pen_ref[kt] = jnp.where((key >= thr) & visible, 0.0, NEG_BIG)
            n_gt = n_gt + _fold_rows(jnp.where(key > thr, 1.0, 0.0))
            n_eq = n_eq + _fold_rows(jnp.where(key == thr, 1.0, 0.0))
            return n_gt, n_eq

        zero = jnp.zeros((SUBLANES, qb), F32)
        n_gt, n_eq = lax.fori_loop(0, nkt, pen_tile, (zero, zero))
        need = kf - jnp.sum(n_gt, axis=0, keepdims=True)
        n_eq = jnp.sum(n_eq, axis=0, keepdims=True)

        tie_rows = jnp.where((n_eq > need) & (thr > KEY_NEG_INF), 1.0, 0.0)

        @pl.when(jnp.max(tie_rows) > 0.0)
        def _():
            nbits = int(key_ref.shape[0] * tk - 1).bit_length()

            def idx_step(i, p):
                trial = p | lax.shift_left(np.int32(1), nbits - 1 - i)

                def count_tile(kt, acc):
                    col, _, _ = tile_geom(kt)
                    hit = (key_ref[kt] == thr) & (col < trial)
                    return acc + _fold_rows(jnp.where(hit, 1.0, 0.0))

                acc = lax.fori_loop(0, nkt, count_tile, jnp.zeros((SUBLANES, qb), F32))
                cnt = jnp.sum(acc, axis=0, keepdims=True)
                return jnp.where(cnt < need, trial, p)

            last = lax.fori_loop(0, nbits, idx_step, jnp.zeros((1, qb), I32))

            def repen_tile(kt, carry):
                key = key_ref[kt]
                col, visible, _ = tile_geom(kt)
                sel = (key > thr) | ((key == thr) & (col <= last))
                pen_ref[kt] = jnp.where(sel & visible, 0.0, NEG_BIG)
                return carry

            lax.fori_loop(0, nkt, repen_tile, 0)

    qs = (q_ref[...].astype(F32) * ATT_SCALE).astype(BF16)
    q_all = jnp.concatenate([qs[:, h * A_HEAD_DIM:(h + 1) * A_HEAD_DIM] for h in range(A_HEADS)],
                            axis=0)
    for h in range(A_HEADS):
        m_ref[h] = jnp.full((SUBLANES, qb), NEG_BIG, F32)
        l_ref[h] = jnp.zeros((SUBLANES, qb), F32)
        acc_ref[h] = jnp.zeros((A_HEAD_DIM, qb), F32)

    def logit_tile(kt, carry):
        kk = kv_ref[pl.ds(row_off(kt), tk), 0:A_HEAD_DIM]
        pen = pen_ref[kt]
        nd = nd_ref[kt]
        s_all = lax.dot_general(kk, q_all, _NT, preferred_element_type=F32)
        for h in range(A_HEADS):
            slope = float(2.0 ** (-8.0 * (h + 1) / A_HEADS))
            s = s_all[:, h * qb:(h + 1) * qb] + (slope * nd + pen)
            s_ref[h, kt] = s
            m_ref[h] = jnp.maximum(m_ref[h], _fold_rows(s, jnp.maximum))
        return carry

    lax.fori_loop(0, nkt, logit_tile, 0)
    m_fin = [jnp.max(m_ref[h], axis=0, keepdims=True) for h in range(A_HEADS)]

    def prob_tile(kt, carry):
        v_t = kv_ref[pl.ds(row_off(kt), tk), A_HEAD_DIM:2 * A_HEAD_DIM].T
        probs = []
        for h in range(A_HEADS):
            p = jnp.exp(s_ref[h, kt] - m_fin[h])
            l_ref[h] = l_ref[h] + _fold_rows(p)
            probs.append(p.astype(BF16))
        pv = jnp.dot(v_t, jnp.concatenate(probs, axis=1), preferred_element_type=F32)
        for h in range(A_HEADS):
            acc_ref[h] = acc_ref[h] + pv[:, h * qb:(h + 1) * qb]
        return carry

    lax.fori_loop(0, nkt, prob_tile, 0)
    outs = [(acc_ref[h] / jnp.sum(l_ref[h], axis=0, keepdims=True)).T for h in range(A_HEADS)]
    o_ref[...] = jnp.concatenate(outs, axis=-1).astype(BF16)


def _attention(q, qi, misc, kv, batch, seq):
    n = q.shape[0]
    qb = min(Q_BLOCK, seq)
    nb = seq // qb
    topk = min(TOPK_MAX, seq // 4)
    assert seq // SUBLANES < (1 << SEL_FIELD_BITS)
    rowq = lambda b, t: (b * nb + t, 0)
    rowk = lambda b, t: (b, 0)
    tiles = (nb, qb, qb)
    return pl.pallas_call(
        functools.partial(_attn_kernel, topk=topk),
        grid=(batch, nb),
        in_specs=[pl.BlockSpec((qb, A_WIDTH), rowq),
                  pl.BlockSpec((qb, IDX_HEADS * IDX_DIM), rowq),
                  pl.BlockSpec((qb, LANES), rowq),
                  pl.BlockSpec((seq, LANES), rowk),
                  pl.BlockSpec((seq, 2 * A_HEAD_DIM), rowk)],
        out_specs=pl.BlockSpec((qb, A_WIDTH), rowq),
        out_shape=jax.ShapeDtypeStruct((n, A_WIDTH), BF16),
        scratch_shapes=[pltpu.VMEM(tiles, I32), pltpu.VMEM(tiles, F32), pltpu.VMEM(tiles, F32),
                        pltpu.VMEM((A_HEADS,) + tiles, F32),
                        pltpu.VMEM((A_HEADS, SUBLANES, qb), F32),
                        pltpu.VMEM((A_HEADS, SUBLANES, qb), F32),
                        pltpu.VMEM((A_HEADS, A_HEAD_DIM, qb), F32)],
        compiler_params=_params("parallel", "parallel"),
        name="dsa_attention",
    )(q, qi, misc, misc, kv)


def _out_proj_kernel(x_ref, a_ref, b_ref, w_ref, mod_ref, o_ref):
    y = jnp.dot(a_ref[...], w_ref[0:A_WIDTH, :], preferred_element_type=F32)
    y = y + jnp.dot(b_ref[...], w_ref[A_WIDTH:, :], preferred_element_type=F32)
    o_ref[...] = x_ref[...] + mod_ref[0, 2:3, :] * y


def _out_proj(x, a, bo, w_o, mod_l, batch, seq):
    n, d = x.shape
    tm = 512
    tps = seq // tm
    row = lambda b, t: (b * tps + t, 0)
    return pl.pallas_call(
        _out_proj_kernel,
        grid=(batch, tps),
        in_specs=[pl.BlockSpec((tm, d), row),
                  pl.BlockSpec((tm, A_WIDTH), row),
                  pl.BlockSpec((tm, B_WIDTH), row),
                  pl.BlockSpec(w_o.shape, lambda b, t: (0, 0)),
                  pl.BlockSpec((1, 6, d), lambda b, t: (b, 0, 0))],
        out_specs=pl.BlockSpec((tm, d), row),
        out_shape=jax.ShapeDtypeStruct((n, d), F32),
        compiler_params=_params("parallel", "parallel"),
        name="out_proj",
    )(x, a, bo, w_o, mod_l)


def _pool_kernel(x_ref, halo_ref, mod_ref, g_ref, w_ref, sc_ref, o_ref):
    tm, d = x_ref.shape
    t = pl.program_id(1)
    g = g_ref[...]
    shift = mod_ref[0, 0:1, :]
    scale = mod_ref[0, 1:2, :]
    x = x_ref[...]
    h = _norm_mod(x, g, shift, scale)
    hh = _norm_mod(halo_ref[...], g, shift, scale)
    hh = jnp.where(t > 0, hh, 0.0)
    ext = jnp.concatenate([hh, h], axis=0)
    pos = t * tm + lax.broadcasted_iota(I32, (tm, 1), 0)
    gd = d // len(POOL_WINDOWS)
    ys = []
    for gi, win in enumerate(POOL_WINDOWS):
        cols = slice(gi * gd, (gi + 1) * gd)
        cur = ext[:, cols]
        k = 1
        while k < win:
            cur = cur + pltpu.roll(cur, k, axis=0)
            k *= 2
        cnt = jnp.minimum(pos + 1, win).astype(F32)
        y = (cur[POOL_HALO:, :] / cnt - h[:, cols]).astype(BF16)
        ys.append(jnp.dot(y, w_ref[gi], preferred_element_type=F32))
    y = jnp.concatenate(ys, axis=-1) * sc_ref[...]
    o_ref[...] = x + mod_ref[0, 2:3, :] * y


def _pool_mixer(x, mod_l, g, w_pool, scale, batch, seq):
    n, d = x.shape
    tm = 512
    tps = seq // tm
    row = lambda b, t: (b * tps + t, 0)
    hpt = tm // POOL_HALO
    halo = lambda b, t: (jnp.maximum((b * tps + t) * hpt - 1, 0), 0)
    return pl.pallas_call(
        _pool_kernel,
        grid=(batch, tps),
        in_specs=[pl.BlockSpec((tm, d), row),
                  pl.BlockSpec((POOL_HALO, d), halo),
                  pl.BlockSpec((1, 6, d), lambda b, t: (b, 0, 0)),
                  pl.BlockSpec((1, d), lambda b, t: (0, 0)),
                  pl.BlockSpec(w_pool.shape, lambda b, t: (0, 0, 0)),
                  pl.BlockSpec((1, d), lambda b, t: (0, 0))],
        out_specs=pl.BlockSpec((tm, d), row),
        out_shape=jax.ShapeDtypeStruct((n, d), F32),
        compiler_params=_params("parallel", "parallel"),
        name="pool_mixer",
    )(x, x, mod_l, g, w_pool, scale)


_R_EXP0 = N_GROUPS
RT_TM = 512
SEG_ALIGN = 16
SEG_SHIFT = 4
E_TM = 1152

_TN = (((0,), (0,)), ((), ()))


def _moe_layout(n):
    tm = min(RT_TM, n)
    xs_rows = tm + N_GROUPS * SEG_ALIGN
    e_tm = E_TM if n >= 4 * E_TM else 128
    worst = n + (n // tm) * N_GROUPS * SEG_ALIGN + N_GROUPS * e_tm
    rows = -(-worst // e_tm) * e_tm
    return tm, xs_rows, e_tm, rows


def _perm_matrix(d_row, rows):
    tm = d_row.shape[1]
    return jnp.where(lax.broadcasted_iota(I32, (rows, tm), 0) == d_row, 1.0, 0.0).astype(BF16)


def _split_bf16(a):
    hi = a.astype(BF16)
    return hi, (a - hi.astype(F32)).astype(BF16)


def _route_kernel(x_ref, mod_ref, g_ref, whi_ref, wlo_ref, b_ref, tri_ref,
                  xs_ref, gs_ref, d_ref, cnt_ref):
    tm = x_ref.shape[0]
    xs_rows = xs_ref.shape[0]
    h = _norm_mod(x_ref[...], g_ref[...], mod_ref[0, 3:4, :], mod_ref[0, 4:5, :])
    h_hi, h_lo = _split_bf16(h)
    logits = (jnp.dot(h_hi, wlo_ref[...], preferred_element_type=F32)
              + jnp.dot(h_lo, whi_ref[...], preferred_element_type=F32)
              + jnp.dot(h_hi, whi_ref[...], preferred_element_type=F32)) + b_ref[...]
    lane = lax.broadcasted_iota(I32, (tm, LANES), 1)
    big = np.int32(LANES)

    def top1(vals):
        m = jnp.max(vals, axis=-1, keepdims=True)
        idx = jnp.min(jnp.where(vals == m, lane, big), axis=-1, keepdims=True)
        return m, idx

    gl = jnp.where(lane < N_GROUPS, logits, -jnp.inf)
    gmax, gsel = top1(gl)
    p_g = 1.0 / jnp.sum(jnp.exp(gl - gmax), axis=-1, keepdims=True)
    e_lo = _R_EXP0 + gsel * EXPERTS_PER_GROUP
    el = jnp.where((lane >= e_lo) & (lane < e_lo + EXPERTS_PER_GROUP), logits, -jnp.inf)
    v1, i1 = top1(el)
    v2, i2 = top1(jnp.where(lane == i1, -jnp.inf, el))
    e2 = jnp.exp(v2 - v1)
    g1 = p_g / (1.0 + e2)
    g2 = p_g * e2 / (1.0 + e2)
    cg = jnp.where(lane == i1 - e_lo, g1, 0.0) + jnp.where(lane == i2 - e_lo, g2, 0.0)

    member = jnp.where(lane == gsel, 1.0, 0.0).T[0:8, :]
    rank = jnp.dot(member.astype(BF16), tri_ref[...], preferred_element_type=F32)
    count = jnp.sum(member, axis=1, keepdims=True)
    cnt_ref[0] = jnp.broadcast_to(count, (8, LANES))
    padded = jnp.ceil(count * (1.0 / SEG_ALIGN)) * SEG_ALIGN
    d_row = jnp.zeros((1, tm), F32)
    seg = jnp.zeros((1, 1), F32)
    for g in range(N_GROUPS):
        d_row = d_row + member[g:g + 1, :] * (rank[g:g + 1, :] + seg)
        seg = seg + padded[g:g + 1, :]
    d_row = d_row.astype(I32)
    d_ref[0] = d_row

    perm = _perm_matrix(d_row, xs_rows)
    xs_ref[...] = jnp.dot(perm, h_hi, preferred_element_type=F32).astype(BF16)
    cg_hi, cg_lo = _split_bf16(cg)
    gs_ref[...] = (jnp.dot(perm, cg_hi, preferred_element_type=F32)
                   + jnp.dot(perm, cg_lo, preferred_element_type=F32))


def _route(x, mod_l, g, w_r, b_r, tri, batch, seq):
    n, d = x.shape
    tm, xs_rows, _, _ = _moe_layout(n)
    nt = n // tm
    tps = seq // tm
    full = lambda i: (0, 0)
    return pl.pallas_call(
        _route_kernel,
        grid=(nt,),
        in_specs=[pl.BlockSpec((tm, d), lambda i: (i, 0)),
                  pl.BlockSpec((1, 6, d), lambda i: (i // tps, 0, 0)),
                  pl.BlockSpec((1, d), full),
                  pl.BlockSpec((d, LANES), full),
                  pl.BlockSpec((d, LANES), full),
                  pl.BlockSpec((1, LANES), full),
                  pl.BlockSpec((tm, tm), full)],
        out_specs=[pl.BlockSpec((xs_rows, d), lambda i: (i, 0)),
                   pl.BlockSpec((xs_rows, LANES), lambda i: (i, 0)),
                   pl.BlockSpec((1, 1, tm), lambda i: (i, 0, 0)),
                   pl.BlockSpec((1, 8, LANES), lambda i: (i, 0, 0))],
        out_shape=[jax.ShapeDtypeStruct((nt * xs_rows, d), BF16),
                   jax.ShapeDtypeStruct((nt * xs_rows, LANES), F32),
                   jax.ShapeDtypeStruct((nt, 1, tm), I32),
                   jax.ShapeDtypeStruct((nt, 8, LANES), F32)],
        compiler_params=_params("parallel"),
        name="route_sort",
    )(x, mod_l, g, *_split_bf16(w_r), b_r, tri)


def _dispatch_kernel(src_ref, dst_ref, len_ref, fill_ref, xt_ref, gt_ref, xs_hbm, gs_hbm,
                     zx_ref, zg_ref, sem_x, sem_g):
    i = pl.program_id(0)
    nt = src_ref.shape[0]

    def chunk_copies(x_src, g_src, src_row, dst_row):
        src_row = pl.multiple_of(src_row, SEG_ALIGN)
        dst_row = pl.multiple_of(dst_row, SEG_ALIGN)
        return (pltpu.make_async_copy(x_src.at[pl.ds(src_row, SEG_ALIGN)],
                                      xs_hbm.at[pl.ds(dst_row, SEG_ALIGN)], sem_x),
                pltpu.make_async_copy(g_src.at[pl.ds(src_row, SEG_ALIGN)],
                                      gs_hbm.at[pl.ds(dst_row, SEG_ALIGN)], sem_g))

    def run(x_src, g_src, jobs):
        total = jnp.int32(0)
        for src, dst, rows in jobs:
            nch = rows >> SEG_SHIFT

            def issue(k, carry, src=src, dst=dst):
                for cp in chunk_copies(x_src, g_src, src + k * SEG_ALIGN, dst + k * SEG_ALIGN):
                    cp.start()
                return carry

            lax.fori_loop(0, nch, issue, 0)
            total = total + nch

        def drain(k, carry):
            for cp in chunk_copies(x_src, g_src, 0, 0):
                cp.wait()
            return carry

        lax.fori_loop(0, total, drain, 0)

    @pl.when(i < nt)
    def _():
        t = jnp.minimum(i, nt - 1)
        run(xt_ref, gt_ref, [(src_ref[t, g], dst_ref[t, g], len_ref[t, g]) for g in range(N_GROUPS)])

    @pl.when(i == nt)
    def _():
        zx_ref[...] = jnp.zeros_like(zx_ref)
        zg_ref[...] = jnp.zeros_like(zg_ref)
        for g in range(N_GROUPS + 1):
            start = fill_ref[0, g]
            nch = fill_ref[1, g] >> SEG_SHIFT

            def issue(k, carry, start=start):
                for cp in chunk_copies(zx_ref, zg_ref, 0, start + k * SEG_ALIGN):
                    cp.start()
                return carry

            lax.fori_loop(0, nch, issue, 0)

            def drain(k, carry):
                for cp in chunk_copies(zx_ref, zg_ref, 0, 0):
                    cp.wait()
                return carry

            lax.fori_loop(0, nch, drain, 0)


def _dispatch(xt, gt, src_off, dst_off, seg_len, fill, rows):
    d = xt.shape[1]
    nt = src_off.shape[0]
    xs_rows = xt.shape[0] // nt
    tile = lambda i, *_: (jnp.minimum(i, nt - 1), 0)
    return pl.pallas_call(
        _dispatch_kernel,
        grid_spec=pltpu.PrefetchScalarGridSpec(
            num_scalar_prefetch=4,
            grid=(nt + 1,),
            in_specs=[pl.BlockSpec((xs_rows, d), tile), pl.BlockSpec((xs_rows, LANES), tile)],
            out_specs=[pl.BlockSpec(memory_space=pl.ANY), pl.BlockSpec(memory_space=pl.ANY)],
            scratch_shapes=[pltpu.VMEM((SEG_ALIGN, d), BF16), pltpu.VMEM((SEG_ALIGN, LANES), F32),
                            pltpu.SemaphoreType.DMA, pltpu.SemaphoreType.DMA]),
        out_shape=[jax.ShapeDtypeStruct((rows, d), BF16),
                   jax.ShapeDtypeStruct((rows, LANES), F32)],
        compiler_params=_params("arbitrary"),
        name="dispatch",
    )(src_off, dst_off, seg_len, fill, xt, gt)


def _experts_kernel(grp_ref, valid_ref, x_ref, gs_ref, wg_ref, wu_ref, wd_ref, o_ref, acc_ref):
    j = pl.program_id(0)
    e = pl.program_id(1)

    @pl.when(e == 0)
    def _():
        acc_ref[...] = jnp.zeros_like(acc_ref)

    @pl.when(valid_ref[j] > 0)
    def _():
        x = x_ref[...]
        hg = jnp.dot(x, wg_ref[0, 0].astype(BF16), preferred_element_type=F32)
        hu = jnp.dot(x, wu_ref[0, 0].astype(BF16), preferred_element_type=F32)
        lane = lax.broadcasted_iota(I32, gs_ref.shape, 1)
        gate = jnp.sum(jnp.where(lane == e, gs_ref[...], 0.0), axis=-1, keepdims=True)
        act = (hg * jax.nn.sigmoid(hg)) * hu * gate
        acc_ref[...] += jnp.dot(act.astype(BF16), wd_ref[0, 0].astype(BF16),
                                preferred_element_type=F32)

    @pl.when(e == pl.num_programs(1) - 1)
    def _():
        o_ref[...] = acc_ref[...]


def _experts(xs, gs, tile_grp, tile_valid, wg, wu, wd, layer, e_tm):
    rows, d = xs.shape
    f = wg.shape[-1]
    ntile = tile_grp.shape[0]
    row = lambda j, e, grp, valid: (j, 0)
    wsel = lambda j, e, grp, valid: (
        layer, grp[j] * EXPERTS_PER_GROUP + jnp.where(valid[j] > 0, e, EXPERTS_PER_GROUP - 1), 0, 0)
    return pl.pallas_call(
        _experts_kernel,
        grid_spec=pltpu.PrefetchScalarGridSpec(
            num_scalar_prefetch=2,
            grid=(ntile, EXPERTS_PER_GROUP),
            in_specs=[pl.BlockSpec((e_tm, d), row),
                      pl.BlockSpec((e_tm, LANES), row),
                      pl.BlockSpec((1, 1, d, f), wsel),
                      pl.BlockSpec((1, 1, d, f), wsel),
                      pl.BlockSpec((1, 1, f, d), wsel)],
            out_specs=pl.BlockSpec((e_tm, d), row),
            scratch_shapes=[pltpu.VMEM((e_tm, d), F32)]),
        out_shape=jax.ShapeDtypeStruct((rows, d), F32),
        compiler_params=_params("arbitrary", "arbitrary"),
        name="experts",
    )(tile_grp, tile_valid, xs, gs, wg, wu, wd)


def _combine_kernel(meta_ref, x_ref, d_ref, mod_ref, gout_ref, ys_hbm, o_ref, yseg, sem, *, final_norm):
    i = pl.program_id(0)
    nt = pl.num_programs(0)
    xs_rows = yseg.shape[1]

    def seg_copy(slot, src_row, dst_row):
        src_row = pl.multiple_of(src_row, SEG_ALIGN)
        dst_row = pl.multiple_of(dst_row, SEG_ALIGN)
        return pltpu.make_async_copy(ys_hbm.at[pl.ds(src_row, SEG_ALIGN)],
                                     yseg.at[slot, pl.ds(dst_row, SEG_ALIGN)], sem.at[slot])

    def fetch(tile, slot):
        yseg[slot] = jnp.zeros(yseg.shape[1:], F32)
        seg = jnp.int32(0)
        for g in range(N_GROUPS):
            base = meta_ref[tile, g]
            c16 = meta_ref[tile, N_GROUPS + g]

            def issue(k, carry, seg=seg, base=base):
                seg_copy(slot, base + k * SEG_ALIGN, seg + k * SEG_ALIGN).start()
                return carry

            lax.fori_loop(0, c16 >> SEG_SHIFT, issue, 0)
            seg = seg + c16

    @pl.when(i == 0)
    def _():
        fetch(0, 0)

    @pl.when(i + 1 < nt)
    def _():
        fetch(jnp.minimum(i + 1, nt - 1), (i + 1) % 2)

    slot = i % 2
    total_chunks = jnp.int32(0)
    for g in range(N_GROUPS):
        total_chunks = total_chunks + (meta_ref[i, N_GROUPS + g] >> SEG_SHIFT)

    def drain(k, carry):
        seg_copy(slot, 0, 0).wait()
        return carry

    lax.fori_loop(0, total_chunks, drain, 0)

    perm = _perm_matrix(d_ref[0], xs_rows)
    y_hi, y_lo = _split_bf16(yseg[slot])
    y = (lax.dot_general(perm, y_hi, _TN, preferred_element_type=F32)
         + lax.dot_general(perm, y_lo, _TN, preferred_element_type=F32))
    out = x_ref[...] + mod_ref[0, 5:6, :] * y
    if final_norm:
        out = out * lax.rsqrt(jnp.mean(out * out, axis=-1, keepdims=True) + EPS) * gout_ref[...]
    o_ref[...] = out


def _combine(x, d_rows, meta, ys, mod_l, g_out, batch, seq, final_norm):
    n, d = x.shape
    tm, xs_rows, _, _ = _moe_layout(n)
    tps = seq // tm
    return pl.pallas_call(
        functools.partial(_combine_kernel, final_norm=final_norm),
        grid_spec=pltpu.PrefetchScalarGridSpec(
            num_scalar_prefetch=1,
            grid=(n // tm,),
            in_specs=[pl.BlockSpec((tm, d), lambda i, meta: (i, 0)),
                      pl.BlockSpec((1, 1, tm), lambda i, meta: (i, 0, 0)),
                      pl.BlockSpec((1, 6, d), lambda i, meta: (i // tps, 0, 0)),
                      pl.BlockSpec((1, d), lambda i, meta: (0, 0)),
                      pl.BlockSpec(memory_space=pl.ANY)],
            out_specs=pl.BlockSpec((tm, d), lambda i, meta: (i, 0)),
            scratch_shapes=[pltpu.VMEM((2, xs_rows, d), F32), pltpu.SemaphoreType.DMA((2,))]),
        out_shape=jax.ShapeDtypeStruct((n, d), F32),
        compiler_params=_params("arbitrary"),
        name="combine",
    )(meta, x, d_rows, mod_l, g_out, ys)


def _moe_plan(cnt, n):
    _, _, e_tm, rows = _moe_layout(n)
    seg_len = ((cnt + (SEG_ALIGN - 1)) // SEG_ALIGN) * SEG_ALIGN
    totals = jnp.sum(seg_len, axis=0)
    region = ((totals + (e_tm - 1)) // e_tm) * e_tm
    region_end = jnp.cumsum(region)
    region_start = region_end - region
    src_off = jnp.cumsum(seg_len, axis=1) - seg_len
    dst_off = region_start[None, :] + jnp.cumsum(seg_len, axis=0) - seg_len
    fill_start = jnp.concatenate([region_start + totals, region_end[-1:]])
    fill_len = jnp.concatenate([region - totals, rows - region_end[-1:]])
    fill = jnp.stack([fill_start, fill_len]).astype(I32)
    j = jnp.arange(rows // e_tm, dtype=I32) * e_tm
    valid = j < region_end[-1]
    grp = jnp.minimum(jnp.sum(j[:, None] >= region_end[None, :], axis=1), N_GROUPS - 1)
    last_grp = jnp.max(jnp.where(region > 0, jnp.arange(N_GROUPS), 0))
    grp = jnp.where(valid, grp, last_grp).astype(I32)
    return (src_off.astype(I32), dst_off.astype(I32), seg_len.astype(I32), fill, grp,
            valid.astype(I32))


def _moe(x, mod_l, g_ffn, w_r, b_r, tri, wg, wu, wd, layer, g_out, batch, seq, final_norm):
    n = x.shape[0]
    _, _, e_tm, rows = _moe_layout(n)
    xt, gt, d_rows, cnt = _route(x, mod_l, g_ffn, w_r, b_r, tri, batch, seq)
    cnt = cnt[:, 0:N_GROUPS, 0].astype(I32)
    src_off, dst_off, seg_len, fill, grp, valid = _moe_plan(cnt, n)
    xs, gs = _dispatch(xt, gt, src_off, dst_off, seg_len, fill, rows)
    ys = _experts(xs, gs, grp, valid, wg, wu, wd, layer, e_tm)
    meta = jnp.concatenate([dst_off, seg_len], axis=1)
    return _combine(x, d_rows, meta, ys, mod_l, g_out, batch, seq, final_norm)


def kernel(x, c, w_mod, b_mod, norm_mix, norm_ffn, w_in, kv_norm, w_kv_up, sgu_norm, w_s, b_s, w_o,
           w_pool, pool_scale, w_gr, b_gr, w_er, b_er, w_gate, w_up, w_down, norm_out):
    batch, seq, d = x.shape
    depth = w_mod.shape[0]
    n = batch * seq
    xs = x.reshape(n, d)
    mod = _modulation(c, w_mod, b_mod).reshape(depth, batch, 6, d)

    pos = np.arange(SGU_CHUNK)
    sgu_mask = jnp.asarray((pos[None, :] // CHUNK) <= (pos[:, None] // CHUNK))
    sizes = (A_WIDTH, KV_LATENT, IDX_HEADS * IDX_DIM, IDX_DIM, IDX_HEADS, B_WIDTH, B_WIDTH)
    offs = np.concatenate([[0], np.cumsum(sizes)])
    pad = LANES - IDX_DIM - IDX_HEADS
    rt_tm = _moe_layout(n)[0]
    tri = jnp.asarray(np.triu(np.ones((rt_tm, rt_tm), np.float32), 1), BF16)

    for i in range(depth):
        j = i // 2
        mod_l = mod[i]
        g_mix = norm_mix[i].reshape(1, d)
        if i % 2 == 0:
            wi = w_in[j]
            w_in_r = jnp.concatenate(
                [wi[:, offs[0]:offs[3]], wi[:, offs[3]:offs[5]], jnp.zeros((d, pad), wi.dtype),
                 wi[:, offs[5]:offs[7]]], axis=1).astype(BF16)
            w_s_m = jnp.where(sgu_mask[None], w_s[j], 0.0).astype(BF16)
            b_s_b = jnp.broadcast_to(b_s[j][:, :, None], (B_GROUPS, SGU_CHUNK, B_WIDTH // B_GROUPS))
            q, kv, qi, misc, bo = _in_proj(
                xs, mod_l, g_mix, w_in_r, kv_norm[j].reshape(1, -1), w_kv_up[j].astype(BF16),
                sgu_norm[j].reshape(1, -1), w_s_m, b_s_b, batch, seq)
            a = _attention(q, qi, misc, kv, batch, seq)
            xs = _out_proj(xs, a, bo, w_o[j].astype(BF16), mod_l, batch, seq)
        else:
            xs = _pool_mixer(xs, mod_l, g_mix, w_pool[j].astype(BF16), pool_scale[j].reshape(1, d),
                             batch, seq)
        w_r = jnp.concatenate([w_gr[i], w_er[i],
                               jnp.zeros((d, LANES - N_GROUPS - N_EXPERTS), F32)], axis=1)
        b_r = jnp.concatenate([b_gr[i], b_er[i],
                               jnp.zeros((LANES - N_GROUPS - N_EXPERTS,), F32)]).reshape(1, LANES)
        xs = _moe(xs, mod_l, norm_ffn[i].reshape(1, d), w_r, b_r, tri, w_gate, w_up, w_down, i,
                  norm_out.reshape(1, d), batch, seq, final_norm=(i == depth - 1))
    return xs.reshape(batch, seq, d)
```

```python
import functools

import numpy as np
import jax
import jax.numpy as jnp
from jax import lax
from jax.experimental import pallas as pl
from jax.experimental.pallas import tpu as pltpu

F32 = jnp.float32
BF16 = jnp.bfloat16
I32 = jnp.int32

EPS = 1e-6
CHUNK = 64
CHUNK_SHIFT = 6
Q_BLOCK = 256
PROJ_TM = 1024
POOL_TM = 2048
A_HEADS = 8
A_HEAD_DIM = 64
A_WIDTH = A_HEADS * A_HEAD_DIM
KV_LATENT = 128
IDX_HEADS = 8
IDX_DIM = 64
TOPK_MAX = 256
B_GROUPS = 4
SGU_CHUNK = 128
POOL_WINDOWS = (2, 4, 8, 16)
POOL_HALO = 16
N_GROUPS = 4
EXPERTS_PER_GROUP = 8
N_EXPERTS = N_GROUPS * EXPERTS_PER_GROUP
LANES = 128
NEG_BIG = -1e30
INT_MIN = np.int32(-2 ** 31)
KEY_NEG_INF = np.int32(np.int32(-8388608) ^ np.int32(0x7FFFFFFF))
VMEM_LIMIT = 56 * 1024 * 1024

_NT = (((1,), (1,)), ((), ()))


def _params(*sem):
    return pltpu.CompilerParams(dimension_semantics=sem, vmem_limit_bytes=VMEM_LIMIT)


def _norm_mod(x, g, shift, scale):
    y = x * lax.rsqrt(jnp.mean(x * x, axis=-1, keepdims=True) + EPS)
    return (y * g) * (1.0 + scale) + shift


def _mod_kernel(c_ref, w_ref, b_ref, o_ref):
    c = c_ref[...]
    cs = c * jax.nn.sigmoid(c)
    o_ref[0] = jnp.dot(cs.astype(BF16), w_ref[0].astype(BF16),
                       preferred_element_type=F32) + b_ref[0]


def _modulation(c, w_mod, b_mod):
    depth, d, d6 = w_mod.shape
    b = c.shape[0]
    tn = 1024
    return pl.pallas_call(
        _mod_kernel,
        grid=(depth, d6 // tn),
        in_specs=[pl.BlockSpec((b, d), lambda i, j: (0, 0)),
                  pl.BlockSpec((1, d, tn), lambda i, j: (i, 0, j)),
                  pl.BlockSpec((1, 1, tn), lambda i, j: (i, 0, j))],
        out_specs=pl.BlockSpec((1, b, tn), lambda i, j: (i, 0, j)),
        out_shape=jax.ShapeDtypeStruct((depth, b, d6), F32),
        compiler_params=_params("parallel", "parallel"),
        name="modulation",
    )(c, w_mod, b_mod.reshape(depth, 1, d6))


_C_Q = 0
_C_KV = _C_Q + A_WIDTH
_C_QI = _C_KV + KV_LATENT
_C_MISC = _C_QI + IDX_HEADS * IDX_DIM
_C_U = _C_MISC + LANES
B_WIDTH = 512
_C_V = _C_U + B_WIDTH
_C_END = _C_V + B_WIDTH


def _in_proj_kernel(x_ref, mod_ref, g_ref, w_ref, kvn_ref, wkv_ref, lng_ref, ws_ref, bs_ref,
                    q_ref, kv_ref, qi_ref, misc_ref, b_ref):
    tm = x_ref.shape[0]
    h = _norm_mod(x_ref[...], g_ref[...], mod_ref[0, 0:1, :], mod_ref[0, 1:2, :]).astype(BF16)

    def proj(lo, hi):
        return jnp.dot(h, w_ref[:, lo:hi], preferred_element_type=F32)

    q_ref[...] = proj(_C_Q, _C_KV).astype(BF16)
    qi_ref[...] = proj(_C_QI, _C_MISC).astype(BF16)

    kvl = proj(_C_KV, _C_QI)
    kvn = kvl * lax.rsqrt(jnp.mean(kvl * kvl, axis=-1, keepdims=True) + EPS) * kvn_ref[...]
    kv_ref[...] = jnp.dot(kvn.astype(BF16), wkv_ref[...], preferred_element_type=F32).astype(BF16)

    lane = lax.broadcasted_iota(I32, (1, LANES), 1)
    idx_scale = (IDX_HEADS ** -0.5) * (IDX_DIM ** -0.5)
    w_lane = (lane >= IDX_DIM) & (lane < IDX_DIM + IDX_HEADS)
    misc_ref[...] = proj(_C_MISC, _C_U) * jnp.where(w_lane, idx_scale, 1.0)

    gu = jax.nn.gelu(proj(_C_U, _C_V))
    gv = jax.nn.gelu(proj(_C_V, _C_END))
    mu = jnp.mean(gv, axis=-1, keepdims=True)
    cen = gv - mu
    var = jnp.mean(cen * cen, axis=-1, keepdims=True)
    vn = (cen * lax.rsqrt(var + EPS) * lng_ref[...]).astype(BF16)
    gd = B_WIDTH // B_GROUPS
    for r in range(tm // SGU_CHUNK):
        rows = slice(r * SGU_CHUNK, (r + 1) * SGU_CHUNK)
        for g in range(B_GROUPS):
            cols = slice(g * gd, (g + 1) * gd)
            mixed = jnp.dot(ws_ref[g], vn[rows, cols], preferred_element_type=F32) + bs_ref[g]
            b_ref[rows, cols] = (gu[rows, cols] * mixed).astype(BF16)


def _in_proj(x, mod_l, g, w_in_r, kv_norm, w_kv_up, sgu_norm, w_s_m, b_s_b, batch, seq):
    n, d = x.shape
    tm = min(PROJ_TM, seq)
    tps = seq // tm
    row = lambda b, t: (b * tps + t, 0)
    full2 = lambda b, t: (0, 0)
    full3 = lambda b, t: (0, 0, 0)
    outs = [(A_WIDTH, BF16), (2 * A_HEAD_DIM, BF16), (IDX_HEADS * IDX_DIM, BF16), (LANES, F32),
            (B_WIDTH, BF16)]
    return pl.pallas_call(
        _in_proj_kernel,
        grid=(batch, tps),
        in_specs=[pl.BlockSpec((tm, d), row),
                  pl.BlockSpec((1, 6, d), lambda b, t: (b, 0, 0)),
                  pl.BlockSpec((1, d), full2),
                  pl.BlockSpec(w_in_r.shape, full2),
                  pl.BlockSpec((1, KV_LATENT), full2),
                  pl.BlockSpec(w_kv_up.shape, full2),
                  pl.BlockSpec((1, B_WIDTH), full2),
                  pl.BlockSpec(w_s_m.shape, full3),
                  pl.BlockSpec(b_s_b.shape, full3)],
        out_specs=[pl.BlockSpec((tm, w), row) for w, _ in outs],
        out_shape=[jax.ShapeDtypeStruct((n, w), dt) for w, dt in outs],
        compiler_params=_params("parallel", "parallel"),
        name="in_proj_sgu",
    )(x, mod_l, g, w_in_r, kv_norm, w_kv_up, sgu_norm, w_s_m, b_s_b)


SEL_FIELD_BITS = 10
SUBLANES = 8
ATT_SCALE = A_HEAD_DIM ** -0.5
assert ATT_SCALE == 2.0 ** round(np.log2(ATT_SCALE)), "folded into bf16 q, must be a power of two"


def _fold_rows(x, op=jnp.add):
    acc = x[0:SUBLANES]
    for r in range(1, x.shape[0] // SUBLANES):
        acc = op(acc, x[r * SUBLANES:(r + 1) * SUBLANES])
    return acc


def _radix4_select(key_ref, nkt, kf):
    _, tk, qb = key_ref.shape
    c1 = 1
    c2 = c1 + (1 << SEL_FIELD_BITS)
    c3 = c2 + (1 << (2 * SEL_FIELD_BITS))
    fmask = (1 << SEL_FIELD_BITS) - 1

    def step(i, prefix):
        unit = lax.shift_left(np.int32(1), 30 - 2 * i)
        ts = [(prefix + j * unit) ^ INT_MIN for j in (1, 2, 3)]

        def count_tile(kt, acc):
            key = key_ref[kt]
            hits = jnp.where(key >= ts[2], c3, jnp.where(key >= ts[1], c2, jnp.where(key >= ts[0], c1, 0)))
            return acc + _fold_rows(hits)

        acc = lax.fori_loop(0, nkt, count_tile, jnp.zeros((SUBLANES, qb), I32))
        cnt = [jnp.sum(((acc >> (j * SEL_FIELD_BITS)) & fmask).astype(F32), axis=0, keepdims=True)
               for j in range(3)]
        digit = jnp.where(cnt[2] >= kf, 3, jnp.where(cnt[1] >= kf, 2, jnp.where(cnt[0] >= kf, 1, 0)))
        return prefix + digit * unit

    return lax.fori_loop(0, 16, step, jnp.zeros((1, qb), I32)) ^ INT_MIN


def _attn_kernel(q_ref, qi_ref, mq_ref, mk_ref, kv_ref, o_ref,
                 key_ref, pen_ref, nd_ref, s_ref, m_ref, l_ref, acc_ref, *, topk):
    _, tk, qb = key_ref.shape
    t = pl.program_id(1)
    nkt = t + 1
    kf = float(topk)
    krow = lax.broadcasted_iota(I32, (tk, qb), 0)
    qlane = lax.broadcasted_iota(I32, (tk, qb), 1)
    qchunk = (t * qb + qlane) >> CHUNK_SHIFT
    rel = qlane - krow

    def tile_geom(kt):
        col = kt * tk + krow
        visible = (col >> CHUNK_SHIFT) <= qchunk
        neg_dist = -jnp.abs(rel + (t - kt) * tk).astype(F32)
        return col, visible, neg_dist

    def row_off(kt):
        return pl.multiple_of(kt * tk, tk)

    @pl.when(nkt * tk <= topk)
    def _():
        def fill(kt, carry):
            _, visible, neg_dist = tile_geom(kt)
            pen_ref[kt] = jnp.where(visible, 0.0, NEG_BIG)
            nd_ref[kt] = neg_dist
            return carry

        lax.fori_loop(0, nkt, fill, 0)

    @pl.when(nkt * tk > topk)
    def _():
        qi = qi_ref[...]
        qi_all = jnp.concatenate([qi[:, h * IDX_DIM:(h + 1) * IDX_DIM] for h in range(IDX_HEADS)],
                                 axis=0)
        mq_t = mq_ref[...].T
        w_rows = [mq_t[IDX_DIM + h:IDX_DIM + h + 1, :] for h in range(IDX_HEADS)]

        def score_tile(kt, carry):
            k_idx = mk_ref[pl.ds(row_off(kt), tk), 0:IDX_DIM].astype(BF16)
            logits = lax.dot_general(k_idx, qi_all, _NT, preferred_element_type=F32)
            score = jnp.zeros((tk, qb), F32)
            for h in range(IDX_HEADS):
                score = score + w_rows[h] * jnp.maximum(logits[:, h * qb:(h + 1) * qb], 0.0)
            _, visible, neg_dist = tile_geom(kt)
            score = jnp.where(visible, score, -jnp.inf)
            bits = pltpu.bitcast(score, I32)
            key_ref[kt] = bits ^ ((bits >> 31) & np.int32(0x7FFFFFFF))
            nd_ref[kt] = neg_dist
            return carry

        lax.fori_loop(0, nkt, score_tile, 0)

        thr = _radix4_select(key_ref, nkt, kf)

        def pen_tile(kt, carry):
            n_gt, n_eq = carry
            key = key_ref[kt]
            _, visible, _ = tile_geom(kt)
            pen_ref[kt] = jnp.where((key >= thr) & visible, 0.0, NEG_BIG)
            n_gt = n_gt + _fold_rows(jnp.where(key > thr, 1.0, 0.0))
            n_eq = n_eq + _fold_rows(jnp.where(key == thr, 1.0, 0.0))
            return n_gt, n_eq

        zero = jnp.zeros((SUBLANES, qb), F32)
        n_gt, n_eq = lax.fori_loop(0, nkt, pen_tile, (zero, zero))
        need = kf - jnp.sum(n_gt, axis=0, keepdims=True)
        n_eq = jnp.sum(n_eq, axis=0, keepdims=True)

        tie_rows = jnp.where((n_eq > need) & (thr > KEY_NEG_INF), 1.0, 0.0)

        @pl.when(jnp.max(tie_rows) > 0.0)
        def _():
            nbits = int(key_ref.shape[0] * tk - 1).bit_length()

            def idx_step(i, p):
                trial = p | lax.shift_left(np.int32(1), nbits - 1 - i)

                def count_tile(kt, acc):
                    col, _, _ = tile_geom(kt)
                    hit = (key_ref[kt] == thr) & (col < trial)
                    return acc + _fold_rows(jnp.where(hit, 1.0, 0.0))

                acc = lax.fori_loop(0, nkt, count_tile, jnp.zeros((SUBLANES, qb), F32))
                cnt = jnp.sum(acc, axis=0, keepdims=True)
                return jnp.where(cnt < need, trial, p)

            last = lax.fori_loop(0, nbits, idx_step, jnp.zeros((1, qb), I32))

            def repen_tile(kt, carry):
                key = key_ref[kt]
                col, visible, _ = tile_geom(kt)
                sel = (key > thr) | ((key == thr) & (col <= last))
                pen_ref[kt] = jnp.where(sel & visible, 0.0, NEG_BIG)
                return carry

            lax.fori_loop(0, nkt, repen_tile, 0)

    qs = (q_ref[...].astype(F32) * ATT_SCALE).astype(BF16)
    q_all = jnp.concatenate([qs[:, h * A_HEAD_DIM:(h + 1) * A_HEAD_DIM] for h in range(A_HEADS)],
                            axis=0)
    for h in range(A_HEADS):
        m_ref[h] = jnp.full((SUBLANES, qb), NEG_BIG, F32)
        l_ref[h] = jnp.zeros((SUBLANES, qb), F32)
        acc_ref[h] = jnp.zeros((A_HEAD_DIM, qb), F32)

    def logit_tile(kt, carry):
        kk = kv_ref[pl.ds(row_off(kt), tk), 0:A_HEAD_DIM]
        pen = pen_ref[kt]
        nd = nd_ref[kt]
        s_all = lax.dot_general(kk, q_all, _NT, preferred_element_type=F32)
        for h in range(A_HEADS):
            slope = float(2.0 ** (-8.0 * (h + 1) / A_HEADS))
            s = s_all[:, h * qb:(h + 1) * qb] + (slope * nd + pen)
            s_ref[h, kt] = s
            m_ref[h] = jnp.maximum(m_ref[h], _fold_rows(s, jnp.maximum))
        return carry

    lax.fori_loop(0, nkt, logit_tile, 0)
    m_fin = [jnp.max(m_ref[h], axis=0, keepdims=True) for h in range(A_HEADS)]

    def prob_tile(kt, carry):
        v_t = kv_ref[pl.ds(row_off(kt), tk), A_HEAD_DIM:2 * A_HEAD_DIM].T
        probs = []
        for h in range(A_HEADS):
            p = jnp.exp(s_ref[h, kt] - m_fin[h])
            l_ref[h] = l_ref[h] + _fold_rows(p)
            probs.append(p.astype(BF16))
        pv = jnp.dot(v_t, jnp.concatenate(probs, axis=1), preferred_element_type=F32)
        for h in range(A_HEADS):
            acc_ref[h] = acc_ref[h] + pv[:, h * qb:(h + 1) * qb]
        return carry

    lax.fori_loop(0, nkt, prob_tile, 0)
    outs = [(acc_ref[h] / jnp.sum(l_ref[h], axis=0, keepdims=True)).T for h in range(A_HEADS)]
    o_ref[...] = jnp.concatenate(outs, axis=-1).astype(BF16)


def _attention(q, qi, misc, kv, batch, seq):
    n = q.shape[0]
    qb = min(Q_BLOCK, seq)
    nb = seq // qb
    topk = min(TOPK_MAX, seq // 4)
    assert seq // SUBLANES < (1 << SEL_FIELD_BITS)
    rowq = lambda b, t: (b * nb + t, 0)
    rowk = lambda b, t: (b, 0)
    tiles = (nb, qb, qb)
    return pl.pallas_call(
        functools.partial(_attn_kernel, topk=topk),
        grid=(batch, nb),
        in_specs=[pl.BlockSpec((qb, A_WIDTH), rowq),
                  pl.BlockSpec((qb, IDX_HEADS * IDX_DIM), rowq),
                  pl.BlockSpec((qb, LANES), rowq),
                  pl.BlockSpec((seq, LANES), rowk),
                  pl.BlockSpec((seq, 2 * A_HEAD_DIM), rowk)],
        out_specs=pl.BlockSpec((qb, A_WIDTH), rowq),
        out_shape=jax.ShapeDtypeStruct((n, A_WIDTH), BF16),
        scratch_shapes=[pltpu.VMEM(tiles, I32), pltpu.VMEM(tiles, F32), pltpu.VMEM(tiles, F32),
                        pltpu.VMEM((A_HEADS,) + tiles, F32),
                        pltpu.VMEM((A_HEADS, SUBLANES, qb), F32),
                        pltpu.VMEM((A_HEADS, SUBLANES, qb), F32),
                        pltpu.VMEM((A_HEADS, A_HEAD_DIM, qb), F32)],
        compiler_params=_params("parallel", "parallel"),
        name="dsa_attention",
    )(q, qi, misc, misc, kv)


def _out_proj_kernel(x_ref, a_ref, b_ref, w_ref, mod_ref, o_ref):
    y = jnp.dot(a_ref[...], w_ref[0:A_WIDTH, :], preferred_element_type=F32)
    y = y + jnp.dot(b_ref[...], w_ref[A_WIDTH:, :], preferred_element_type=F32)
    o_ref[...] = x_ref[...] + mod_ref[0, 2:3, :] * y


def _out_proj(x, a, bo, w_o, mod_l, batch, seq):
    n, d = x.shape
    tm = min(PROJ_TM, seq)
    tps = seq // tm
    row = lambda b, t: (b * tps + t, 0)
    return pl.pallas_call(
        _out_proj_kernel,
        grid=(batch, tps),
        in_specs=[pl.BlockSpec((tm, d), row),
                  pl.BlockSpec((tm, A_WIDTH), row),
                  pl.BlockSpec((tm, B_WIDTH), row),
                  pl.BlockSpec(w_o.shape, lambda b, t: (0, 0)),
                  pl.BlockSpec((1, 6, d), lambda b, t: (b, 0, 0))],
        out_specs=pl.BlockSpec((tm, d), row),
        out_shape=jax.ShapeDtypeStruct((n, d), F32),
        compiler_params=_params("parallel", "parallel"),
        name="out_proj",
    )(x, a, bo, w_o, mod_l)


def _pool_kernel(x_ref, halo_ref, mod_ref, g_ref, w_ref, sc_ref, o_ref):
    tm, d = x_ref.shape
    t = pl.program_id(1)
    g = g_ref[...]
    shift = mod_ref[0, 0:1, :]
    scale = mod_ref[0, 1:2, :]
    x = x_ref[...]
    h = _norm_mod(x, g, shift, scale)
    hh = _norm_mod(halo_ref[...], g, shift, scale)
    hh = jnp.where(t > 0, hh, 0.0)
    ext = jnp.concatenate([hh, h], axis=0)
    pos = t * tm + lax.broadcasted_iota(I32, (tm, 1), 0)
    gd = d // len(POOL_WINDOWS)
    ys = []
    for gi, win in enumerate(POOL_WINDOWS):
        cols = slice(gi * gd, (gi + 1) * gd)
        cur = ext[:, cols]
        k = 1
        while k < win:
            cur = cur + pltpu.roll(cur, k, axis=0)
            k *= 2
        cnt = jnp.minimum(pos + 1, win).astype(F32)
        y = (cur[POOL_HALO:, :] / cnt - h[:, cols]).astype(BF16)
        ys.append(jnp.dot(y, w_ref[gi], preferred_element_type=F32))
    y = jnp.concatenate(ys, axis=-1) * sc_ref[...]
    o_ref[...] = x + mod_ref[0, 2:3, :] * y


def _pool_mixer(x, mod_l, g, w_pool, scale, batch, seq):
    n, d = x.shape
    tm = min(POOL_TM, seq)
    tps = seq // tm
    row = lambda b, t: (b * tps + t, 0)
    hpt = tm // POOL_HALO
    halo = lambda b, t: (jnp.maximum((b * tps + t) * hpt - 1, 0), 0)
    return pl.pallas_call(
        _pool_kernel,
        grid=(batch, tps),
        in_specs=[pl.BlockSpec((tm, d), row),
                  pl.BlockSpec((POOL_HALO, d), halo),
                  pl.BlockSpec((1, 6, d), lambda b, t: (b, 0, 0)),
                  pl.BlockSpec((1, d), lambda b, t: (0, 0)),
                  pl.BlockSpec(w_pool.shape, lambda b, t: (0, 0, 0)),
                  pl.BlockSpec((1, d), lambda b, t: (0, 0))],
        out_specs=pl.BlockSpec((tm, d), row),
        out_shape=jax.ShapeDtypeStruct((n, d), F32),
        compiler_params=_params("parallel", "parallel"),
        name="pool_mixer",
    )(x, x, mod_l, g, w_pool, scale)


_R_EXP0 = N_GROUPS
RT_TM = 512
SEG_ALIGN = 16
SEG_SHIFT = 4
E_TM = 1152

_TN = (((0,), (0,)), ((), ()))


def _moe_layout(n):
    tm = min(RT_TM, n)
    xs_rows = tm + N_GROUPS * SEG_ALIGN
    e_tm = E_TM if n >= 4 * E_TM else 128
    worst = n + (n // tm) * N_GROUPS * SEG_ALIGN + N_GROUPS * e_tm
    rows = -(-worst // e_tm) * e_tm
    return tm, xs_rows, e_tm, rows


def _perm_matrix(d_row, rows):
    tm = d_row.shape[1]
    return jnp.where(lax.broadcasted_iota(I32, (rows, tm), 0) == d_row, 1.0, 0.0).astype(BF16)


def _split_bf16(a):
    hi = a.astype(BF16)
    return hi, (a - hi.astype(F32)).astype(BF16)


def _route_kernel(x_ref, mod_ref, g_ref, whi_ref, wlo_ref, b_ref, tri_ref,
                  xs_ref, gs_ref, d_ref, cnt_ref):
    tm = x_ref.shape[0]
    xs_rows = xs_ref.shape[0]
    h = _norm_mod(x_ref[...], g_ref[...], mod_ref[0, 3:4, :], mod_ref[0, 4:5, :])
    h_hi, h_lo = _split_bf16(h)
    logits = (jnp.dot(h_hi, wlo_ref[...], preferred_element_type=F32)
              + jnp.dot(h_lo, whi_ref[...], preferred_element_type=F32)
              + jnp.dot(h_hi, whi_ref[...], preferred_element_type=F32)) + b_ref[...]
    lane = lax.broadcasted_iota(I32, (tm, LANES), 1)
    big = np.int32(LANES)

    def top1(vals):
        m = jnp.max(vals, axis=-1, keepdims=True)
        idx = jnp.min(jnp.where(vals == m, lane, big), axis=-1, keepdims=True)
        return m, idx

    gl = jnp.where(lane < N_GROUPS, logits, -jnp.inf)
    gmax, gsel = top1(gl)
    p_g = 1.0 / jnp.sum(jnp.exp(gl - gmax), axis=-1, keepdims=True)
    e_lo = _R_EXP0 + gsel * EXPERTS_PER_GROUP
    el = jnp.where((lane >= e_lo) & (lane < e_lo + EXPERTS_PER_GROUP), logits, -jnp.inf)
    v1, i1 = top1(el)
    v2, i2 = top1(jnp.where(lane == i1, -jnp.inf, el))
    e2 = jnp.exp(v2 - v1)
    g1 = p_g / (1.0 + e2)
    g2 = p_g * e2 / (1.0 + e2)
    cg = jnp.where(lane == i1 - e_lo, g1, 0.0) + jnp.where(lane == i2 - e_lo, g2, 0.0)

    member = jnp.where(lane == gsel, 1.0, 0.0).T[0:8, :]
    rank = jnp.dot(member.astype(BF16), tri_ref[...], preferred_element_type=F32)
    count = jnp.sum(member, axis=1, keepdims=True)
    cnt_ref[0] = jnp.broadcast_to(count, (8, LANES))
    padded = jnp.ceil(count * (1.0 / SEG_ALIGN)) * SEG_ALIGN
    d_row = jnp.zeros((1, tm), F32)
    seg = jnp.zeros((1, 1), F32)
    for g in range(N_GROUPS):
        d_row = d_row + member[g:g + 1, :] * (rank[g:g + 1, :] + seg)
        seg = seg + padded[g:g + 1, :]
    d_row = d_row.astype(I32)
    d_ref[0] = d_row

    perm = _perm_matrix(d_row, xs_rows)
    xs_ref[...] = jnp.dot(perm, h_hi, preferred_element_type=F32).astype(BF16)
    cg_hi, cg_lo = _split_bf16(cg)
    gs_ref[...] = (jnp.dot(perm, cg_hi, preferred_element_type=F32)
                   + jnp.dot(perm, cg_lo, preferred_element_type=F32))


def _route(x, mod_l, g, w_r, b_r, tri, batch, seq):
    n, d = x.shape
    tm, xs_rows, _, _ = _moe_layout(n)
    nt = n // tm
    tps = seq // tm
    full = lambda i: (0, 0)
    return pl.pallas_call(
        _route_kernel,
        grid=(nt,),
        in_specs=[pl.BlockSpec((tm, d), lambda i: (i, 0)),
                  pl.BlockSpec((1, 6, d), lambda i: (i // tps, 0, 0)),
                  pl.BlockSpec((1, d), full),
                  pl.BlockSpec((d, LANES), full),
                  pl.BlockSpec((d, LANES), full),
                  pl.BlockSpec((1, LANES), full),
                  pl.BlockSpec((tm, tm), full)],
        out_specs=[pl.BlockSpec((xs_rows, d), lambda i: (i, 0)),
                   pl.BlockSpec((xs_rows, LANES), lambda i: (i, 0)),
                   pl.BlockSpec((1, 1, tm), lambda i: (i, 0, 0)),
                   pl.BlockSpec((1, 8, LANES), lambda i: (i, 0, 0))],
        out_shape=[jax.ShapeDtypeStruct((nt * xs_rows, d), BF16),
                   jax.ShapeDtypeStruct((nt * xs_rows, LANES), F32),
                   jax.ShapeDtypeStruct((nt, 1, tm), I32),
                   jax.ShapeDtypeStruct((nt, 8, LANES), F32)],
        compiler_params=_params("parallel"),
        name="route_sort",
    )(x, mod_l, g, *_split_bf16(w_r), b_r, tri)


def _dispatch_kernel(src_ref, dst_ref, len_ref, fill_ref, xt_ref, gt_ref, xs_hbm, gs_hbm,
                     zx_ref, zg_ref, sem_x, sem_g):
    i = pl.program_id(0)
    nt = src_ref.shape[0]

    def chunk_copies(x_src, g_src, src_row, dst_row):
        src_row = pl.multiple_of(src_row, SEG_ALIGN)
        dst_row = pl.multiple_of(dst_row, SEG_ALIGN)
        return (pltpu.make_async_copy(x_src.at[pl.ds(src_row, SEG_ALIGN)],
                                      xs_hbm.at[pl.ds(dst_row, SEG_ALIGN)], sem_x),
                pltpu.make_async_copy(g_src.at[pl.ds(src_row, SEG_ALIGN)],
                                      gs_hbm.at[pl.ds(dst_row, SEG_ALIGN)], sem_g))

    def run(x_src, g_src, jobs):
        total = jnp.int32(0)
        for src, dst, rows in jobs:
            nch = rows >> SEG_SHIFT

            def issue(k, carry, src=src, dst=dst):
                for cp in chunk_copies(x_src, g_src, src + k * SEG_ALIGN, dst + k * SEG_ALIGN):
                    cp.start()
                return carry

            lax.fori_loop(0, nch, issue, 0)
            total = total + nch

        def drain(k, carry):
            for cp in chunk_copies(x_src, g_src, 0, 0):
                cp.wait()
            return carry

        lax.fori_loop(0, total, drain, 0)

    @pl.when(i < nt)
    def _():
        t = jnp.minimum(i, nt - 1)
        run(xt_ref, gt_ref, [(src_ref[t, g], dst_ref[t, g], len_ref[t, g]) for g in range(N_GROUPS)])

    @pl.when(i == nt)
    def _():
        zx_ref[...] = jnp.zeros_like(zx_ref)
        zg_ref[...] = jnp.zeros_like(zg_ref)
        for g in range(N_GROUPS + 1):
            start = fill_ref[0, g]
            nch = fill_ref[1, g] >> SEG_SHIFT

            def issue(k, carry, start=start):
                for cp in chunk_copies(zx_ref, zg_ref, 0, start + k * SEG_ALIGN):
                    cp.start()
                return carry

            lax.fori_loop(0, nch, issue, 0)

            def drain(k, carry):
                for cp in chunk_copies(zx_ref, zg_ref, 0, 0):
                    cp.wait()
                return carry

            lax.fori_loop(0, nch, drain, 0)


def _dispatch(xt, gt, src_off, dst_off, seg_len, fill, rows):
    d = xt.shape[1]
    nt = src_off.shape[0]
    xs_rows = xt.shape[0] // nt
    tile = lambda i, *_: (jnp.minimum(i, nt - 1), 0)
    return pl.pallas_call(
        _dispatch_kernel,
        grid_spec=pltpu.PrefetchScalarGridSpec(
            num_scalar_prefetch=4,
            grid=(nt + 1,),
            in_specs=[pl.BlockSpec((xs_rows, d), tile), pl.BlockSpec((xs_rows, LANES), tile)],
            out_specs=[pl.BlockSpec(memory_space=pl.ANY), pl.BlockSpec(memory_space=pl.ANY)],
            scratch_shapes=[pltpu.VMEM((SEG_ALIGN, d), BF16), pltpu.VMEM((SEG_ALIGN, LANES), F32),
                            pltpu.SemaphoreType.DMA, pltpu.SemaphoreType.DMA]),
        out_shape=[jax.ShapeDtypeStruct((rows, d), BF16),
                   jax.ShapeDtypeStruct((rows, LANES), F32)],
        compiler_params=_params("arbitrary"),
        name="dispatch",
    )(src_off, dst_off, seg_len, fill, xt, gt)


def _experts_kernel(grp_ref, valid_ref, x_ref, gs_ref, wg_ref, wu_ref, wd_ref, o_ref):
    j = pl.program_id(0)
    e = pl.program_id(1)

    @pl.when(e == 0)
    def _():
        o_ref[...] = jnp.zeros_like(o_ref)

    @pl.when(valid_ref[j] > 0)
    def _():
        x = x_ref[...]
        hg = jnp.dot(x, wg_ref[0, 0].astype(BF16), preferred_element_type=F32)
        hu = jnp.dot(x, wu_ref[0, 0].astype(BF16), preferred_element_type=F32)
        lane = lax.broadcasted_iota(I32, gs_ref.shape, 1)
        gate = jnp.sum(jnp.where(lane == e, gs_ref[...], 0.0), axis=-1, keepdims=True)
        act = (hg * jax.nn.sigmoid(hg)) * hu * gate
        o_ref[...] += jnp.dot(act.astype(BF16), wd_ref[0, 0].astype(BF16),
                              preferred_element_type=F32)


def _experts(xs, gs, tile_grp, tile_valid, wg, wu, wd, layer, e_tm):
    rows, d = xs.shape
    f = wg.shape[-1]
    ntile = tile_grp.shape[0]
    row = lambda j, e, grp, valid: (j, 0)
    wsel = lambda j, e, grp, valid: (
        layer, grp[j] * EXPERTS_PER_GROUP + jnp.where(valid[j] > 0, e, EXPERTS_PER_GROUP - 1), 0, 0)
    return pl.pallas_call(
        _experts_kernel,
        grid_spec=pltpu.PrefetchScalarGridSpec(
            num_scalar_prefetch=2,
            grid=(ntile, EXPERTS_PER_GROUP),
            in_specs=[pl.BlockSpec((e_tm, d), row),
                      pl.BlockSpec((e_tm, LANES), row),
                      pl.BlockSpec((1, 1, d, f), wsel),
                      pl.BlockSpec((1, 1, d, f), wsel),
                      pl.BlockSpec((1, 1, f, d), wsel)],
            out_specs=pl.BlockSpec((e_tm, d), row)),
        out_shape=jax.ShapeDtypeStruct((rows, d), F32),
        compiler_params=_params("arbitrary", "arbitrary"),
        name="experts",
    )(tile_grp, tile_valid, xs, gs, wg, wu, wd)


def _combine_kernel(meta_ref, x_ref, d_ref, mod_ref, gout_ref, ys_hbm, o_ref, yseg, sem, *, final_norm):
    i = pl.program_id(0)
    nt = pl.num_programs(0)
    xs_rows = yseg.shape[1]

    def seg_copy(slot, src_row, dst_row):
        src_row = pl.multiple_of(src_row, SEG_ALIGN)
        dst_row = pl.multiple_of(dst_row, SEG_ALIGN)
        return pltpu.make_async_copy(ys_hbm.at[pl.ds(src_row, SEG_ALIGN)],
                                     yseg.at[slot, pl.ds(dst_row, SEG_ALIGN)], sem.at[slot])

    def fetch(tile, slot):
        yseg[slot] = jnp.zeros(yseg.shape[1:], F32)
        seg = jnp.int32(0)
        for g in range(N_GROUPS):
            base = meta_ref[tile, g]
            c16 = meta_ref[tile, N_GROUPS + g]

            def issue(k, carry, seg=seg, base=base):
                seg_copy(slot, base + k * SEG_ALIGN, seg + k * SEG_ALIGN).start()
                return carry

            lax.fori_loop(0, c16 >> SEG_SHIFT, issue, 0)
            seg = seg + c16

    @pl.when(i == 0)
    def _():
        fetch(0, 0)

    @pl.when(i + 1 < nt)
    def _():
        fetch(jnp.minimum(i + 1, nt - 1), (i + 1) % 2)

    slot = i % 2
    total_chunks = jnp.int32(0)
    for g in range(N_GROUPS):
        total_chunks = total_chunks + (meta_ref[i, N_GROUPS + g] >> SEG_SHIFT)

    def drain(k, carry):
        seg_copy(slot, 0, 0).wait()
        return carry

    lax.fori_loop(0, total_chunks, drain, 0)

    perm = _perm_matrix(d_ref[0], xs_rows)
    y_hi, y_lo = _split_bf16(yseg[slot])
    y = (lax.dot_general(perm, y_hi, _TN, preferred_element_type=F32)
         + lax.dot_general(perm, y_lo, _TN, preferred_element_type=F32))
    out = x_ref[...] + mod_ref[0, 5:6, :] * y
    if final_norm:
        out = out * lax.rsqrt(jnp.mean(out * out, axis=-1, keepdims=True) + EPS) * gout_ref[...]
    o_ref[...] = out


def _combine(x, d_rows, meta, ys, mod_l, g_out, batch, seq, final_norm):
    n, d = x.shape
    tm, xs_rows, _, _ = _moe_layout(n)
    tps = seq // tm
    return pl.pallas_call(
        functools.partial(_combine_kernel, final_norm=final_norm),
        grid_spec=pltpu.PrefetchScalarGridSpec(
            num_scalar_prefetch=1,
            grid=(n // tm,),
            in_specs=[pl.BlockSpec((tm, d), lambda i, meta: (i, 0)),
                      pl.BlockSpec((1, 1, tm), lambda i, meta: (i, 0, 0)),
                      pl.BlockSpec((1, 6, d), lambda i, meta: (i // tps, 0, 0)),
                      pl.BlockSpec((1, d), lambda i, meta: (0, 0)),
                      pl.BlockSpec(memory_space=pl.ANY)],
            out_specs=pl.BlockSpec((tm, d), lambda i, meta: (i, 0)),
            scratch_shapes=[pltpu.VMEM((2, xs_rows, d), F32), pltpu.SemaphoreType.DMA((2,))]),
        out_shape=jax.ShapeDtypeStruct((n, d), F32),
        compiler_params=_params("arbitrary"),
        name="combine",
    )(meta, x, d_rows, mod_l, g_out, ys)


def _moe_plan(cnt, n):
    _, _, e_tm, rows = _moe_layout(n)
    seg_len = ((cnt + (SEG_ALIGN - 1)) // SEG_ALIGN) * SEG_ALIGN
    totals = jnp.sum(seg_len, axis=0)
    region = ((totals + (e_tm - 1)) // e_tm) * e_tm
    region_end = jnp.cumsum(region)
    region_start = region_end - region
    src_off = jnp.cumsum(seg_len, axis=1) - seg_len
    dst_off = region_start[None, :] + jnp.cumsum(seg_len, axis=0) - seg_len
    fill_start = jnp.concatenate([region_start + totals, region_end[-1:]])
    fill_len = jnp.concatenate([region - totals, rows - region_end[-1:]])
    fill = jnp.stack([fill_start, fill_len]).astype(I32)
    j = jnp.arange(rows // e_tm, dtype=I32) * e_tm
    valid = j < region_end[-1]
    grp = jnp.minimum(jnp.sum(j[:, None] >= region_end[None, :], axis=1), N_GROUPS - 1)
    last_grp = jnp.max(jnp.where(region > 0, jnp.arange(N_GROUPS), 0))
    grp = jnp.where(valid, grp, last_grp).astype(I32)
    return (src_off.astype(I32), dst_off.astype(I32), seg_len.astype(I32), fill, grp,
            valid.astype(I32))


def _moe(x, mod_l, g_ffn, w_r, b_r, tri, wg, wu, wd, layer, g_out, batch, seq, final_norm):
    n = x.shape[0]
    _, _, e_tm, rows = _moe_layout(n)
    xt, gt, d_rows, cnt = _route(x, mod_l, g_ffn, w_r, b_r, tri, batch, seq)
    cnt = cnt[:, 0:N_GROUPS, 0].astype(I32)
    src_off, dst_off, seg_len, fill, grp, valid = _moe_plan(cnt, n)
    xs, gs = _dispatch(xt, gt, src_off, dst_off, seg_len, fill, rows)
    ys = _experts(xs, gs, grp, valid, wg, wu, wd, layer, e_tm)
    meta = jnp.concatenate([dst_off, seg_len], axis=1)
    return _combine(x, d_rows, meta, ys, mod_l, g_out, batch, seq, final_norm)


def kernel(x, c, w_mod, b_mod, norm_mix, norm_ffn, w_in, kv_norm, w_kv_up, sgu_norm, w_s, b_s, w_o,
           w_pool, pool_scale, w_gr, b_gr, w_er, b_er, w_gate, w_up, w_down, norm_out):
    batch, seq, d = x.shape
    depth = w_mod.shape[0]
    n = batch * seq
    xs = x.reshape(n, d)
    mod = _modulation(c, w_mod, b_mod).reshape(depth, batch, 6, d)

    pos = np.arange(SGU_CHUNK)
    sgu_mask = jnp.asarray((pos[None, :] // CHUNK) <= (pos[:, None] // CHUNK))
    sizes = (A_WIDTH, KV_LATENT, IDX_HEADS * IDX_DIM, IDX_DIM, IDX_HEADS, B_WIDTH, B_WIDTH)
    offs = np.concatenate([[0], np.cumsum(sizes)])
    pad = LANES - IDX_DIM - IDX_HEADS
    rt_tm = _moe_layout(n)[0]
    tri = jnp.asarray(np.triu(np.ones((rt_tm, rt_tm), np.float32), 1), BF16)

    for i in range(depth):
        j = i // 2
        mod_l = mod[i]
        g_mix = norm_mix[i].reshape(1, d)
        if i % 2 == 0:
            wi = w_in[j]
            w_in_r = jnp.concatenate(
                [wi[:, offs[0]:offs[3]], wi[:, offs[3]:offs[5]], jnp.zeros((d, pad), wi.dtype),
                 wi[:, offs[5]:offs[7]]], axis=1).astype(BF16)
            w_s_m = jnp.where(sgu_mask[None], w_s[j], 0.0).astype(BF16)
            b_s_b = jnp.broadcast_to(b_s[j][:, :, None], (B_GROUPS, SGU_CHUNK, B_WIDTH // B_GROUPS))
            q, kv, qi, misc, bo = _in_proj(
                xs, mod_l, g_mix, w_in_r, kv_norm[j].reshape(1, -1), w_kv_up[j].astype(BF16),
                sgu_norm[j].reshape(1, -1), w_s_m, b_s_b, batch, seq)
            a = _attention(q, qi, misc, kv, batch, seq)
            xs = _out_proj(xs, a, bo, w_o[j].astype(BF16), mod_l, batch, seq)
        else:
            xs = _pool_mixer(xs, mod_l, g_mix, w_pool[j].astype(BF16), pool_scale[j].reshape(1, d),
                             batch, seq)
        w_r = jnp.concatenate([w_gr[i], w_er[i],
                               jnp.zeros((d, LANES - N_GROUPS - N_EXPERTS), F32)], axis=1)
        b_r = jnp.concatenate([b_gr[i], b_er[i],
                               jnp.zeros((LANES - N_GROUPS - N_EXPERTS,), F32)]).reshape(1, LANES)
        xs = _moe(xs, mod_l, norm_ffn[i].reshape(1, d), w_r, b_r, tri, w_gate, w_up, w_down, i,
                  norm_out.reshape(1, d), batch, seq, final_norm=(i == depth - 1))
    return xs.reshape(batch, seq, d)
```

```python
import functools

import numpy as np
import jax
import jax.numpy as jnp
from jax import lax
from jax.experimental import pallas as pl
from jax.experimental.pallas import tpu as pltpu

F32 = jnp.float32
BF16 = jnp.bfloat16
I32 = jnp.int32

EPS = 1e-6
CHUNK = 64
CHUNK_SHIFT = 6
Q_BLOCK = 256
PROJ_TM = 1024
POOL_TM = 2048
A_HEADS = 8
A_HEAD_DIM = 64
A_WIDTH = A_HEADS * A_HEAD_DIM
KV_LATENT = 128
IDX_HEADS = 8
IDX_DIM = 64
TOPK_MAX = 256
B_GROUPS = 4
SGU_CHUNK = 128
POOL_WINDOWS = (2, 4, 8, 16)
POOL_HALO = 16
N_GROUPS = 4
EXPERTS_PER_GROUP = 8
N_EXPERTS = N_GROUPS * EXPERTS_PER_GROUP
LANES = 128
NEG_BIG = -1e30
INT_MIN = np.int32(-2 ** 31)
KEY_NEG_INF = np.int32(np.int32(-8388608) ^ np.int32(0x7FFFFFFF))
VMEM_LIMIT = 56 * 1024 * 1024

_NT = (((1,), (1,)), ((), ()))


def _params(*sem):
    return pltpu.CompilerParams(dimension_semantics=sem, vmem_limit_bytes=VMEM_LIMIT)


def _norm_mod(x, g, shift, scale):
    y = x * lax.rsqrt(jnp.mean(x * x, axis=-1, keepdims=True) + EPS)
    return (y * g) * (1.0 + scale) + shift


def _mod_kernel(c_ref, w_ref, b_ref, o_ref):
    c = c_ref[...]
    cs = c * jax.nn.sigmoid(c)
    o_ref[0] = jnp.dot(cs.astype(BF16), w_ref[0].astype(BF16),
                       preferred_element_type=F32) + b_ref[0]


def _modulation(c, w_mod, b_mod):
    depth, d, d6 = w_mod.shape
    b = c.shape[0]
    tn = 1024
    return pl.pallas_call(
        _mod_kernel,
        grid=(depth, d6 // tn),
        in_specs=[pl.BlockSpec((b, d), lambda i, j: (0, 0)),
                  pl.BlockSpec((1, d, tn), lambda i, j: (i, 0, j)),
                  pl.BlockSpec((1, 1, tn), lambda i, j: (i, 0, j))],
        out_specs=pl.BlockSpec((1, b, tn), lambda i, j: (i, 0, j)),
        out_shape=jax.ShapeDtypeStruct((depth, b, d6), F32),
        compiler_params=_params("parallel", "parallel"),
        name="modulation",
    )(c, w_mod, b_mod.reshape(depth, 1, d6))


_C_Q = 0
_C_KV = _C_Q + A_WIDTH
_C_QI = _C_KV + KV_LATENT
_C_MISC = _C_QI + IDX_HEADS * IDX_DIM
_C_U = _C_MISC + LANES
B_WIDTH = 512
_C_V = _C_U + B_WIDTH
_C_END = _C_V + B_WIDTH


def _in_proj_kernel(x_ref, mod_ref, g_ref, w_ref, kvn_ref, wkv_ref, lng_ref, ws_ref, bs_ref,
                    q_ref, kv_ref, qi_ref, misc_ref, b_ref):
    tm = x_ref.shape[0]
    h = _norm_mod(x_ref[...], g_ref[...], mod_ref[0, 0:1, :], mod_ref[0, 1:2, :]).astype(BF16)

    def proj(lo, hi):
        return jnp.dot(h, w_ref[:, lo:hi], preferred_element_type=F32)

    q_ref[...] = proj(_C_Q, _C_KV).astype(BF16)
    qi_ref[...] = proj(_C_QI, _C_MISC).astype(BF16)

    kvl = proj(_C_KV, _C_QI)
    kvn = kvl * lax.rsqrt(jnp.mean(kvl * kvl, axis=-1, keepdims=True) + EPS) * kvn_ref[...]
    kv_ref[...] = jnp.dot(kvn.astype(BF16), wkv_ref[...], preferred_element_type=F32).astype(BF16)

    lane = lax.broadcasted_iota(I32, (1, LANES), 1)
    idx_scale = (IDX_HEADS ** -0.5) * (IDX_DIM ** -0.5)
    w_lane = (lane >= IDX_DIM) & (lane < IDX_DIM + IDX_HEADS)
    misc_ref[...] = proj(_C_MISC, _C_U) * jnp.where(w_lane, idx_scale, 1.0)

    gu = jax.nn.gelu(proj(_C_U, _C_V))
    gv = jax.nn.gelu(proj(_C_V, _C_END))
    mu = jnp.mean(gv, axis=-1, keepdims=True)
    cen = gv - mu
    var = jnp.mean(cen * cen, axis=-1, keepdims=True)
    vn = (cen * lax.rsqrt(var + EPS) * lng_ref[...]).astype(BF16)
    gd = B_WIDTH // B_GROUPS
    for r in range(tm // SGU_CHUNK):
        rows = slice(r * SGU_CHUNK, (r + 1) * SGU_CHUNK)
        for g in range(B_GROUPS):
            cols = slice(g * gd, (g + 1) * gd)
            mixed = jnp.dot(ws_ref[g], vn[rows, cols], preferred_element_type=F32) + bs_ref[g]
            b_ref[rows, cols] = (gu[rows, cols] * mixed).astype(BF16)


def _in_proj(x, mod_l, g, w_in_r, kv_norm, w_kv_up, sgu_norm, w_s_m, b_s_b, batch, seq):
    n, d = x.shape
    tm = min(PROJ_TM, seq)
    tps = seq // tm
    row = lambda b, t: (b * tps + t, 0)
    full2 = lambda b, t: (0, 0)
    full3 = lambda b, t: (0, 0, 0)
    outs = [(A_WIDTH, BF16), (2 * A_HEAD_DIM, BF16), (IDX_HEADS * IDX_DIM, BF16), (LANES, F32),
            (B_WIDTH, BF16)]
    return pl.pallas_call(
        _in_proj_kernel,
        grid=(batch, tps),
        in_specs=[pl.BlockSpec((tm, d), row),
                  pl.BlockSpec((1, 6, d), lambda b, t: (b, 0, 0)),
                  pl.BlockSpec((1, d), full2),
                  pl.BlockSpec(w_in_r.shape, full2),
                  pl.BlockSpec((1, KV_LATENT), full2),
                  pl.BlockSpec(w_kv_up.shape, full2),
                  pl.BlockSpec((1, B_WIDTH), full2),
                  pl.BlockSpec(w_s_m.shape, full3),
                  pl.BlockSpec(b_s_b.shape, full3)],
        out_specs=[pl.BlockSpec((tm, w), row) for w, _ in outs],
        out_shape=[jax.ShapeDtypeStruct((n, w), dt) for w, dt in outs],
        compiler_params=_params("parallel", "parallel"),
        name="in_proj_sgu",
    )(x, mod_l, g, w_in_r, kv_norm, w_kv_up, sgu_norm, w_s_m, b_s_b)


SEL_FIELD_BITS = 10
SUBLANES = 8
ATT_SCALE = A_HEAD_DIM ** -0.5
assert ATT_SCALE == 2.0 ** round(np.log2(ATT_SCALE)), "folded into bf16 q, must be a power of two"


def _fold_rows(x, op=jnp.add):
    acc = x[0:SUBLANES]
    for r in range(1, x.shape[0] // SUBLANES):
        acc = op(acc, x[r * SUBLANES:(r + 1) * SUBLANES])
    return acc


def _radix4_select(key_ref, nkt, kf):
    _, tk, qb = key_ref.shape
    c1 = 1
    c2 = c1 + (1 << SEL_FIELD_BITS)
    c3 = c2 + (1 << (2 * SEL_FIELD_BITS))
    fmask = (1 << SEL_FIELD_BITS) - 1

    def step(i, prefix):
        unit = lax.shift_left(np.int32(1), 30 - 2 * i)
        ts = [(prefix + j * unit) ^ INT_MIN for j in (1, 2, 3)]

        def count_tile(kt, acc):
            key = key_ref[kt]
            hits = jnp.where(key >= ts[2], c3, jnp.where(key >= ts[1], c2, jnp.where(key >= ts[0], c1, 0)))
            return acc + _fold_rows(hits)

        acc = lax.fori_loop(0, nkt, count_tile, jnp.zeros((SUBLANES, qb), I32))
        cnt = [jnp.sum(((acc >> (j * SEL_FIELD_BITS)) & fmask).astype(F32), axis=0, keepdims=True)
               for j in range(3)]
        digit = jnp.where(cnt[2] >= kf, 3, jnp.where(cnt[1] >= kf, 2, jnp.where(cnt[0] >= kf, 1, 0)))
        return prefix + digit * unit

    return lax.fori_loop(0, 16, step, jnp.zeros((1, qb), I32)) ^ INT_MIN


def _attn_kernel(q_ref, qi_ref, mq_ref, mk_ref, kv_ref, o_ref,
                 key_ref, pen_ref, nd_ref, s_ref, m_ref, l_ref, acc_ref, *, topk):
    _, tk, qb = key_ref.shape
    t = pl.program_id(1)
    nkt = t + 1
    kf = float(topk)
    krow = lax.broadcasted_iota(I32, (tk, qb), 0)
    qlane = lax.broadcasted_iota(I32, (tk, qb), 1)
    qchunk = (t * qb + qlane) >> CHUNK_SHIFT
    rel = qlane - krow

    def tile_geom(kt):
        col = kt * tk + krow
        visible = (col >> CHUNK_SHIFT) <= qchunk
        neg_dist = -jnp.abs(rel + (t - kt) * tk).astype(F32)
        return col, visible, neg_dist

    def row_off(kt):
        return pl.multiple_of(kt * tk, tk)

    @pl.when(nkt * tk <= topk)
    def _():
        def fill(kt, carry):
            _, visible, neg_dist = tile_geom(kt)
            pen_ref[kt] = jnp.where(visible, 0.0, NEG_BIG)
            nd_ref[kt] = neg_dist
            return carry

        lax.fori_loop(0, nkt, fill, 0)

    @pl.when(nkt * tk > topk)
    def _():
        qi = qi_ref[...]
        qi_all = jnp.concatenate([qi[:, h * IDX_DIM:(h + 1) * IDX_DIM] for h in range(IDX_HEADS)],
                                 axis=0)
        mq_t = mq_ref[...].T
        w_rows = [mq_t[IDX_DIM + h:IDX_DIM + h + 1, :] for h in range(IDX_HEADS)]

        def score_tile(kt, carry):
            k_idx = mk_ref[pl.ds(row_off(kt), tk), 0:IDX_DIM].astype(BF16)
            logits = lax.dot_general(k_idx, qi_all, _NT, preferred_element_type=F32)
            score = jnp.zeros((tk, qb), F32)
            for h in range(IDX_HEADS):
                score = score + w_rows[h] * jnp.maximum(logits[:, h * qb:(h + 1) * qb], 0.0)
            _, visible, neg_dist = tile_geom(kt)
            score = jnp.where(visible, score, -jnp.inf)
            bits = pltpu.bitcast(score, I32)
            key_ref[kt] = bits ^ ((bits >> 31) & np.int32(0x7FFFFFFF))
            nd_ref[kt] = neg_dist
            return carry

        lax.fori_loop(0, nkt, score_tile, 0)

        thr = _radix4_select(key_ref, nkt, kf)

        def pen_tile(kt, carry):
            n_gt, n_eq = carry
            key = key_ref[kt]
            _, visible, _ = tile_geom(kt)
            pen_ref[kt] = jnp.where((key >= thr) & visible, 0.0, NEG_BIG)
            n_gt = n_gt + _fold_rows(jnp.where(key > thr, 1.0, 0.0))
            n_eq = n_eq + _fold_rows(jnp.where(key == thr, 1.0, 0.0))
            return n_gt, n_eq

        zero = jnp.zeros((SUBLANES, qb), F32)
        n_gt, n_eq = lax.fori_loop(0, nkt, pen_tile, (zero, zero))
        need = kf - jnp.sum(n_gt, axis=0, keepdims=True)
        n_eq = jnp.sum(n_eq, axis=0, keepdims=True)

        tie_rows = jnp.where((n_eq > need) & (thr > KEY_NEG_INF), 1.0, 0.0)

        @pl.when(jnp.max(tie_rows) > 0.0)
        def _():
            nbits = int(key_ref.shape[0] * tk - 1).bit_length()

            def idx_step(i, p):
                trial = p | lax.shift_left(np.int32(1), nbits - 1 - i)

                def count_tile(kt, acc):
                    col, _, _ = tile_geom(kt)
                    hit = (key_ref[kt] == thr) & (col < trial)
                    return acc + _fold_rows(jnp.where(hit, 1.0, 0.0))

                acc = lax.fori_loop(0, nkt, count_tile, jnp.zeros((SUBLANES, qb), F32))
                cnt = jnp.sum(acc, axis=0, keepdims=True)
                return jnp.where(cnt < need, trial, p)

            last = lax.fori_loop(0, nbits, idx_step, jnp.zeros((1, qb), I32))

            def repen_tile(kt, carry):
                key = key_ref[kt]
                col, visible, _ = tile_geom(kt)
                sel = (key > thr) | ((key == thr) & (col <= last))
                pen_ref[kt] = jnp.where(sel & visible, 0.0, NEG_BIG)
                return carry

            lax.fori_loop(0, nkt, repen_tile, 0)

    qs = (q_ref[...].astype(F32) * ATT_SCALE).astype(BF16)
    q_all = jnp.concatenate([qs[:, h * A_HEAD_DIM:(h + 1) * A_HEAD_DIM] for h in range(A_HEADS)],
                            axis=0)
    for h in range(A_HEADS):
        m_ref[h] = jnp.full((SUBLANES, qb), NEG_BIG, F32)
        l_ref[h] = jnp.zeros((SUBLANES, qb), F32)
        acc_ref[h] = jnp.zeros((A_HEAD_DIM, qb), F32)

    def logit_tile(kt, carry):
        kk = kv_ref[pl.ds(row_off(kt), tk), 0:A_HEAD_DIM]
        pen = pen_ref[kt]
        nd = nd_ref[kt]
        s_all = lax.dot_general(kk, q_all, _NT, preferred_element_type=F32)
        for h in range(A_HEADS):
            slope = float(2.0 ** (-8.0 * (h + 1) / A_HEADS))
            s = s_all[:, h * qb:(h + 1) * qb] + (slope * nd + pen)
            s_ref[h, kt] = s
            m_ref[h] = jnp.maximum(m_ref[h], _fold_rows(s, jnp.maximum))
        return carry

    lax.fori_loop(0, nkt, logit_tile, 0)
    m_fin = [jnp.max(m_ref[h], axis=0, keepdims=True) for h in range(A_HEADS)]

    def prob_tile(kt, carry):
        v_t = kv_ref[pl.ds(row_off(kt), tk), A_HEAD_DIM:2 * A_HEAD_DIM].T
        probs = []
        for h in range(A_HEADS):
            p = jnp.exp(s_ref[h, kt] - m_fin[h])
            l_ref[h] = l_ref[h] + _fold_rows(p)
            probs.append(p.astype(BF16))
        pv = jnp.dot(v_t, jnp.concatenate(probs, axis=1), preferred_element_type=F32)
        for h in range(A_HEADS):
            acc_ref[h] = acc_ref[h] + pv[:, h * qb:(h + 1) * qb]
        return carry

    lax.fori_loop(0, nkt, prob_tile, 0)
    outs = [(acc_ref[h] / jnp.sum(l_ref[h], axis=0, keepdims=True)).T for h in range(A_HEADS)]
    o_ref[...] = jnp.concatenate(outs, axis=-1).astype(BF16)


def _attention(q, qi, misc, kv, batch, seq):
    n = q.shape[0]
    qb = min(Q_BLOCK, seq)
    nb = seq // qb
    topk = min(TOPK_MAX, seq // 4)
    assert seq // SUBLANES < (1 << SEL_FIELD_BITS)
    rowq = lambda b, t: (b * nb + t, 0)
    rowk = lambda b, t: (b, 0)
    tiles = (nb, qb, qb)
    return pl.pallas_call(
        functools.partial(_attn_kernel, topk=topk),
        grid=(batch, nb),
        in_specs=[pl.BlockSpec((qb, A_WIDTH), rowq),
                  pl.BlockSpec((qb, IDX_HEADS * IDX_DIM), rowq),
                  pl.BlockSpec((qb, LANES), rowq),
                  pl.BlockSpec((seq, LANES), rowk),
                  pl.BlockSpec((seq, 2 * A_HEAD_DIM), rowk)],
        out_specs=pl.BlockSpec((qb, A_WIDTH), rowq),
        out_shape=jax.ShapeDtypeStruct((n, A_WIDTH), BF16),
        scratch_shapes=[pltpu.VMEM(tiles, I32), pltpu.VMEM(tiles, F32), pltpu.VMEM(tiles, F32),
                        pltpu.VMEM((A_HEADS,) + tiles, F32),
                        pltpu.VMEM((A_HEADS, SUBLANES, qb), F32),
                        pltpu.VMEM((A_HEADS, SUBLANES, qb), F32),
                        pltpu.VMEM((A_HEADS, A_HEAD_DIM, qb), F32)],
        compiler_params=_params("parallel", "parallel"),
        name="dsa_attention",
    )(q, qi, misc, misc, kv)


def _out_proj_kernel(x_ref, a_ref, b_ref, w_ref, mod_ref, o_ref):
    y = jnp.dot(a_ref[...], w_ref[0:A_WIDTH, :], preferred_element_type=F32)
    y = y + jnp.dot(b_ref[...], w_ref[A_WIDTH:, :], preferred_element_type=F32)
    o_ref[...] = x_ref[...] + mod_ref[0, 2:3, :] * y


def _out_proj(x, a, bo, w_o, mod_l, batch, seq):
    n, d = x.shape
    tm = min(PROJ_TM, seq)
    tps = seq // tm
    row = lambda b, t: (b * tps + t, 0)
    return pl.pallas_call(
        _out_proj_kernel,
        grid=(batch, tps),
        in_specs=[pl.BlockSpec((tm, d), row),
                  pl.BlockSpec((tm, A_WIDTH), row),
                  pl.BlockSpec((tm, B_WIDTH), row),
                  pl.BlockSpec(w_o.shape, lambda b, t: (0, 0)),
                  pl.BlockSpec((1, 6, d), lambda b, t: (b, 0, 0))],
        out_specs=pl.BlockSpec((tm, d), row),
        out_shape=jax.ShapeDtypeStruct((n, d), F32),
        compiler_params=_params("parallel", "parallel"),
        name="out_proj",
    )(x, a, bo, w_o, mod_l)


def _pool_kernel(x_ref, halo_ref, mod_ref, g_ref, w_ref, sc_ref, o_ref):
    tm, d = x_ref.shape
    t = pl.program_id(1)
    g = g_ref[...]
    shift = mod_ref[0, 0:1, :]
    scale = mod_ref[0, 1:2, :]
    x = x_ref[...]
    h = _norm_mod(x, g, shift, scale)
    hh = _norm_mod(halo_ref[...], g, shift, scale)
    hh = jnp.where(t > 0, hh, 0.0)
    ext = jnp.concatenate([hh, h], axis=0)
    pos = t * tm + lax.broadcasted_iota(I32, (tm, 1), 0)
    gd = d // len(POOL_WINDOWS)
    ys = []
    for gi, win in enumerate(POOL_WINDOWS):
        cols = slice(gi * gd, (gi + 1) * gd)
        cur = ext[:, cols]
        k = 1
        while k < win:
            cur = cur + pltpu.roll(cur, k, axis=0)
            k *= 2
        cnt = jnp.minimum(pos + 1, win).astype(F32)
        y = (cur[POOL_HALO:, :] / cnt - h[:, cols]).astype(BF16)
        ys.append(jnp.dot(y, w_ref[gi], preferred_element_type=F32))
    y = jnp.concatenate(ys, axis=-1) * sc_ref[...]
    o_ref[...] = x + mod_ref[0, 2:3, :] * y


def _pool_mixer(x, mod_l, g, w_pool, scale, batch, seq):
    n, d = x.shape
    tm = min(POOL_TM, seq)
    tps = seq // tm
    row = lambda b, t: (b * tps + t, 0)
    hpt = tm // POOL_HALO
    halo = lambda b, t: (jnp.maximum((b * tps + t) * hpt - 1, 0), 0)
    return pl.pallas_call(
        _pool_kernel,
        grid=(batch, tps),
        in_specs=[pl.BlockSpec((tm, d), row),
                  pl.BlockSpec((POOL_HALO, d), halo),
                  pl.BlockSpec((1, 6, d), lambda b, t: (b, 0, 0)),
                  pl.BlockSpec((1, d), lambda b, t: (0, 0)),
                  pl.BlockSpec(w_pool.shape, lambda b, t: (0, 0, 0)),
                  pl.BlockSpec((1, d), lambda b, t: (0, 0))],
        out_specs=pl.BlockSpec((tm, d), row),
        out_shape=jax.ShapeDtypeStruct((n, d), F32),
        compiler_params=_params("parallel", "parallel"),
        name="pool_mixer",
    )(x, x, mod_l, g, w_pool, scale)


_R_EXP0 = N_GROUPS
RT_TM = 512
SEG_ALIGN = 16
SEG_SHIFT = 4
E_TM = 1152
EXPERT_STEP = 4

_TN = (((0,), (0,)), ((), ()))


def _moe_layout(n):
    tm = min(RT_TM, n)
    xs_rows = tm + N_GROUPS * SEG_ALIGN
    e_tm = E_TM if n >= 4 * E_TM else 128
    worst = n + (n // tm) * N_GROUPS * SEG_ALIGN + N_GROUPS * e_tm
    rows = -(-worst // e_tm) * e_tm
    return tm, xs_rows, e_tm, rows


def _perm_matrix(d_row, rows):
    tm = d_row.shape[1]
    return jnp.where(lax.broadcasted_iota(I32, (rows, tm), 0) == d_row, 1.0, 0.0).astype(BF16)


def _split_bf16(a):
    hi = a.astype(BF16)
    return hi, (a - hi.astype(F32)).astype(BF16)


def _route_kernel(x_ref, mod_ref, g_ref, whi_ref, wlo_ref, b_ref, tri_ref,
                  xs_ref, gs_ref, d_ref, cnt_ref):
    tm = x_ref.shape[0]
    xs_rows = xs_ref.shape[0]
    h = _norm_mod(x_ref[...], g_ref[...], mod_ref[0, 3:4, :], mod_ref[0, 4:5, :])
    h_hi, h_lo = _split_bf16(h)
    logits = (jnp.dot(h_hi, wlo_ref[...], preferred_element_type=F32)
              + jnp.dot(h_lo, whi_ref[...], preferred_element_type=F32)
              + jnp.dot(h_hi, whi_ref[...], preferred_element_type=F32)) + b_ref[...]
    lane = lax.broadcasted_iota(I32, (tm, LANES), 1)
    big = np.int32(LANES)

    def top1(vals):
        m = jnp.max(vals, axis=-1, keepdims=True)
        idx = jnp.min(jnp.where(vals == m, lane, big), axis=-1, keepdims=True)
        return m, idx

    gl = jnp.where(lane < N_GROUPS, logits, -jnp.inf)
    gmax, gsel = top1(gl)
    p_g = 1.0 / jnp.sum(jnp.exp(gl - gmax), axis=-1, keepdims=True)
    e_lo = _R_EXP0 + gsel * EXPERTS_PER_GROUP
    el = jnp.where((lane >= e_lo) & (lane < e_lo + EXPERTS_PER_GROUP), logits, -jnp.inf)
    v1, i1 = top1(el)
    v2, i2 = top1(jnp.where(lane == i1, -jnp.inf, el))
    e2 = jnp.exp(v2 - v1)
    g1 = p_g / (1.0 + e2)
    g2 = p_g * e2 / (1.0 + e2)
    cg = jnp.where(lane == i1 - e_lo, g1, 0.0) + jnp.where(lane == i2 - e_lo, g2, 0.0)

    member = jnp.where(lane == gsel, 1.0, 0.0).T[0:8, :]
    rank = jnp.dot(member.astype(BF16), tri_ref[...], preferred_element_type=F32)
    count = jnp.sum(member, axis=1, keepdims=True)
    cnt_ref[0] = jnp.broadcast_to(count, (8, LANES))
    padded = jnp.ceil(count * (1.0 / SEG_ALIGN)) * SEG_ALIGN
    d_row = jnp.zeros((1, tm), F32)
    seg = jnp.zeros((1, 1), F32)
    for g in range(N_GROUPS):
        d_row = d_row + member[g:g + 1, :] * (rank[g:g + 1, :] + seg)
        seg = seg + padded[g:g + 1, :]
    d_row = d_row.astype(I32)
    d_ref[0] = d_row

    perm = _perm_matrix(d_row, xs_rows)
    xs_ref[...] = jnp.dot(perm, h_hi, preferred_element_type=F32).astype(BF16)
    cg_hi, cg_lo = _split_bf16(cg)
    gs_ref[...] = (jnp.dot(perm, cg_hi, preferred_element_type=F32)
                   + jnp.dot(perm, cg_lo, preferred_element_type=F32))


def _route(x, mod_l, g, w_r, b_r, tri, batch, seq):
    n, d = x.shape
    tm, xs_rows, _, _ = _moe_layout(n)
    nt = n // tm
    tps = seq // tm
    full = lambda i: (0, 0)
    return pl.pallas_call(
        _route_kernel,
        grid=(nt,),
        in_specs=[pl.BlockSpec((tm, d), lambda i: (i, 0)),
                  pl.BlockSpec((1, 6, d), lambda i: (i // tps, 0, 0)),
                  pl.BlockSpec((1, d), full),
                  pl.BlockSpec((d, LANES), full),
                  pl.BlockSpec((d, LANES), full),
                  pl.BlockSpec((1, LANES), full),
                  pl.BlockSpec((tm, tm), full)],
        out_specs=[pl.BlockSpec((xs_rows, d), lambda i: (i, 0)),
                   pl.BlockSpec((xs_rows, LANES), lambda i: (i, 0)),
                   pl.BlockSpec((1, 1, tm), lambda i: (i, 0, 0)),
                   pl.BlockSpec((1, 8, LANES), lambda i: (i, 0, 0))],
        out_shape=[jax.ShapeDtypeStruct((nt * xs_rows, d), BF16),
                   jax.ShapeDtypeStruct((nt * xs_rows, LANES), F32),
                   jax.ShapeDtypeStruct((nt, 1, tm), I32),
                   jax.ShapeDtypeStruct((nt, 8, LANES), F32)],
        compiler_params=_params("parallel"),
        name="route_sort",
    )(x, mod_l, g, *_split_bf16(w_r), b_r, tri)


def _dispatch_kernel(src_ref, dst_ref, len_ref, fill_ref, xt_ref, gt_ref, xs_hbm, gs_hbm,
                     zx_ref, zg_ref, sem_x, sem_g):
    i = pl.program_id(0)
    nt = src_ref.shape[0]

    def chunk_copies(x_src, g_src, src_row, dst_row):
        src_row = pl.multiple_of(src_row, SEG_ALIGN)
        dst_row = pl.multiple_of(dst_row, SEG_ALIGN)
        return (pltpu.make_async_copy(x_src.at[pl.ds(src_row, SEG_ALIGN)],
                                      xs_hbm.at[pl.ds(dst_row, SEG_ALIGN)], sem_x),
                pltpu.make_async_copy(g_src.at[pl.ds(src_row, SEG_ALIGN)],
                                      gs_hbm.at[pl.ds(dst_row, SEG_ALIGN)], sem_g))

    def run(x_src, g_src, jobs):
        total = jnp.int32(0)
        for src, dst, rows in jobs:
            nch = rows >> SEG_SHIFT

            def issue(k, carry, src=src, dst=dst):
                for cp in chunk_copies(x_src, g_src, src + k * SEG_ALIGN, dst + k * SEG_ALIGN):
                    cp.start()
                return carry

            lax.fori_loop(0, nch, issue, 0)
            total = total + nch

        def drain(k, carry):
            for cp in chunk_copies(x_src, g_src, 0, 0):
                cp.wait()
            return carry

        lax.fori_loop(0, total, drain, 0)

    @pl.when(i < nt)
    def _():
        t = jnp.minimum(i, nt - 1)
        run(xt_ref, gt_ref, [(src_ref[t, g], dst_ref[t, g], len_ref[t, g]) for g in range(N_GROUPS)])

    @pl.when(i == nt)
    def _():
        zx_ref[...] = jnp.zeros_like(zx_ref)
        zg_ref[...] = jnp.zeros_like(zg_ref)
        for g in range(N_GROUPS + 1):
            start = fill_ref[0, g]
            nch = fill_ref[1, g] >> SEG_SHIFT

            def issue(k, carry, start=start):
                for cp in chunk_copies(zx_ref, zg_ref, 0, start + k * SEG_ALIGN):
                    cp.start()
                return carry

            lax.fori_loop(0, nch, issue, 0)

            def drain(k, carry):
                for cp in chunk_copies(zx_ref, zg_ref, 0, 0):
                    cp.wait()
                return carry

            lax.fori_loop(0, nch, drain, 0)


def _dispatch(xt, gt, src_off, dst_off, seg_len, fill, rows):
    d = xt.shape[1]
    nt = src_off.shape[0]
    xs_rows = xt.shape[0] // nt
    tile = lambda i, *_: (jnp.minimum(i, nt - 1), 0)
    return pl.pallas_call(
        _dispatch_kernel,
        grid_spec=pltpu.PrefetchScalarGridSpec(
            num_scalar_prefetch=4,
            grid=(nt + 1,),
            in_specs=[pl.BlockSpec((xs_rows, d), tile), pl.BlockSpec((xs_rows, LANES), tile)],
            out_specs=[pl.BlockSpec(memory_space=pl.ANY), pl.BlockSpec(memory_space=pl.ANY)],
            scratch_shapes=[pltpu.VMEM((SEG_ALIGN, d), BF16), pltpu.VMEM((SEG_ALIGN, LANES), F32),
                            pltpu.SemaphoreType.DMA, pltpu.SemaphoreType.DMA]),
        out_shape=[jax.ShapeDtypeStruct((rows, d), BF16),
                   jax.ShapeDtypeStruct((rows, LANES), F32)],
        compiler_params=_params("arbitrary"),
        name="dispatch",
    )(src_off, dst_off, seg_len, fill, xt, gt)


def _experts_kernel(grp_ref, valid_ref, x_ref, gs_ref, wg_ref, wu_ref, wd_ref, o_ref):
    j = pl.program_id(0)
    e = pl.program_id(1)

    @pl.when(e == 0)
    def _():
        o_ref[...] = jnp.zeros_like(o_ref)

    @pl.when(valid_ref[j] > 0)
    def _():
        x = x_ref[...]
        es, f, d = wd_ref.shape[1:]
        lane = lax.broadcasted_iota(I32, gs_ref.shape, 1)
        acts = []
        for k in range(es):
            hg = jnp.dot(x, wg_ref[0, k].astype(BF16), preferred_element_type=F32)
            hu = jnp.dot(x, wu_ref[0, k].astype(BF16), preferred_element_type=F32)
            gate = jnp.sum(jnp.where(lane == e * es + k, gs_ref[...], 0.0), axis=-1, keepdims=True)
            acts.append(((hg * jax.nn.sigmoid(hg)) * hu * gate).astype(BF16))
        wd = wd_ref[0].astype(BF16).reshape(es * f, d)
        o_ref[...] += jnp.dot(jnp.concatenate(acts, axis=1), wd, preferred_element_type=F32)


def _experts(xs, gs, tile_grp, tile_valid, wg, wu, wd, layer, e_tm):
    rows, d = xs.shape
    f = wg.shape[-1]
    ntile = tile_grp.shape[0]
    row = lambda j, e, grp, valid: (j, 0)
    steps = EXPERTS_PER_GROUP // EXPERT_STEP
    wsel = lambda j, e, grp, valid: (
        layer, grp[j] * steps + jnp.where(valid[j] > 0, e, steps - 1), 0, 0)
    return pl.pallas_call(
        _experts_kernel,
        grid_spec=pltpu.PrefetchScalarGridSpec(
            num_scalar_prefetch=2,
            grid=(ntile, steps),
            in_specs=[pl.BlockSpec((e_tm, d), row),
                      pl.BlockSpec((e_tm, LANES), row),
                      pl.BlockSpec((1, EXPERT_STEP, d, f), wsel),
                      pl.BlockSpec((1, EXPERT_STEP, d, f), wsel),
                      pl.BlockSpec((1, EXPERT_STEP, f, d), wsel)],
            out_specs=pl.BlockSpec((e_tm, d), row)),
        out_shape=jax.ShapeDtypeStruct((rows, d), F32),
        compiler_params=_params("arbitrary", "arbitrary"),
        name="experts",
    )(tile_grp, tile_valid, xs, gs, wg, wu, wd)


def _combine_kernel(meta_ref, x_ref, d_ref, mod_ref, gout_ref, ys_hbm, o_ref, yseg, sem, *, final_norm):
    i = pl.program_id(0)
    nt = pl.num_programs(0)
    xs_rows = yseg.shape[1]

    def seg_copy(slot, src_row, dst_row):
        src_row = pl.multiple_of(src_row, SEG_ALIGN)
        dst_row = pl.multiple_of(dst_row, SEG_ALIGN)
        return pltpu.make_async_copy(ys_hbm.at[pl.ds(src_row, SEG_ALIGN)],
                                     yseg.at[slot, pl.ds(dst_row, SEG_ALIGN)], sem.at[slot])

    def fetch(tile, slot):
        yseg[slot] = jnp.zeros(yseg.shape[1:], F32)
        seg = jnp.int32(0)
        for g in range(N_GROUPS):
            base = meta_ref[tile, g]
            c16 = meta_ref[tile, N_GROUPS + g]

            def issue(k, carry, seg=seg, base=base):
                seg_copy(slot, base + k * SEG_ALIGN, seg + k * SEG_ALIGN).start()
                return carry

            lax.fori_loop(0, c16 >> SEG_SHIFT, issue, 0)
            seg = seg + c16

    @pl.when(i == 0)
    def _():
        fetch(0, 0)

    @pl.when(i + 1 < nt)
    def _():
        fetch(jnp.minimum(i + 1, nt - 1), (i + 1) % 2)

    slot = i % 2
    total_chunks = jnp.int32(0)
    for g in range(N_GROUPS):
        total_chunks = total_chunks + (meta_ref[i, N_GROUPS + g] >> SEG_SHIFT)

    def drain(k, carry):
        seg_copy(slot, 0, 0).wait()
        return carry

    lax.fori_loop(0, total_chunks, drain, 0)

    perm = _perm_matrix(d_ref[0], xs_rows)
    y_hi, y_lo = _split_bf16(yseg[slot])
    y = (lax.dot_general(perm, y_hi, _TN, preferred_element_type=F32)
         + lax.dot_general(perm, y_lo, _TN, preferred_element_type=F32))
    out = x_ref[...] + mod_ref[0, 5:6, :] * y
    if final_norm:
        out = out * lax.rsqrt(jnp.mean(out * out, axis=-1, keepdims=True) + EPS) * gout_ref[...]
    o_ref[...] = out


def _combine(x, d_rows, meta, ys, mod_l, g_out, batch, seq, final_norm):
    n, d = x.shape
    tm, xs_rows, _, _ = _moe_layout(n)
    tps = seq // tm
    return pl.pallas_call(
        functools.partial(_combine_kernel, final_norm=final_norm),
        grid_spec=pltpu.PrefetchScalarGridSpec(
            num_scalar_prefetch=1,
            grid=(n // tm,),
            in_specs=[pl.BlockSpec((tm, d), lambda i, meta: (i, 0)),
                      pl.BlockSpec((1, 1, tm), lambda i, meta: (i, 0, 0)),
                      pl.BlockSpec((1, 6, d), lambda i, meta: (i // tps, 0, 0)),
                      pl.BlockSpec((1, d), lambda i, meta: (0, 0)),
                      pl.BlockSpec(memory_space=pl.ANY)],
            out_specs=pl.BlockSpec((tm, d), lambda i, meta: (i, 0)),
            scratch_shapes=[pltpu.VMEM((2, xs_rows, d), F32), pltpu.SemaphoreType.DMA((2,))]),
        out_shape=jax.ShapeDtypeStruct((n, d), F32),
        compiler_params=_params("arbitrary"),
        name="combine",
    )(meta, x, d_rows, mod_l, g_out, ys)


def _moe_plan(cnt, n):
    _, _, e_tm, rows = _moe_layout(n)
    seg_len = ((cnt + (SEG_ALIGN - 1)) // SEG_ALIGN) * SEG_ALIGN
    totals = jnp.sum(seg_len, axis=0)
    region = ((totals + (e_tm - 1)) // e_tm) * e_tm
    region_end = jnp.cumsum(region)
    region_start = region_end - region
    src_off = jnp.cumsum(seg_len, axis=1) - seg_len
    dst_off = region_start[None, :] + jnp.cumsum(seg_len, axis=0) - seg_len
    fill_start = jnp.concatenate([region_start + totals, region_end[-1:]])
    fill_len = jnp.concatenate([region - totals, rows - region_end[-1:]])
    fill = jnp.stack([fill_start, fill_len]).astype(I32)
    j = jnp.arange(rows // e_tm, dtype=I32) * e_tm
    valid = j < region_end[-1]
    grp = jnp.minimum(jnp.sum(j[:, None] >= region_end[None, :], axis=1), N_GROUPS - 1)
    last_grp = jnp.max(jnp.where(region > 0, jnp.arange(N_GROUPS), 0))
    grp = jnp.where(valid, grp, last_grp).astype(I32)
    return (src_off.astype(I32), dst_off.astype(I32), seg_len.astype(I32), fill, grp,
            valid.astype(I32))


def _moe(x, mod_l, g_ffn, w_r, b_r, tri, wg, wu, wd, layer, g_out, batch, seq, final_norm):
    n = x.shape[0]
    _, _, e_tm, rows = _moe_layout(n)
    xt, gt, d_rows, cnt = _route(x, mod_l, g_ffn, w_r, b_r, tri, batch, seq)
    cnt = cnt[:, 0:N_GROUPS, 0].astype(I32)
    src_off, dst_off, seg_len, fill, grp, valid = _moe_plan(cnt, n)
    xs, gs = _dispatch(xt, gt, src_off, dst_off, seg_len, fill, rows)
    ys = _experts(xs, gs, grp, valid, wg, wu, wd, layer, e_tm)
    meta = jnp.concatenate([dst_off, seg_len], axis=1)
    return _combine(x, d_rows, meta, ys, mod_l, g_out, batch, seq, final_norm)


def kernel(x, c, w_mod, b_mod, norm_mix, norm_ffn, w_in, kv_norm, w_kv_up, sgu_norm, w_s, b_s, w_o,
           w_pool, pool_scale, w_gr, b_gr, w_er, b_er, w_gate, w_up, w_down, norm_out):
    batch, seq, d = x.shape
    depth = w_mod.shape[0]
    n = batch * seq
    xs = x.reshape(n, d)
    mod = _modulation(c, w_mod, b_mod).reshape(depth, batch, 6, d)

    pos = np.arange(SGU_CHUNK)
    sgu_mask = jnp.asarray((pos[None, :] // CHUNK) <= (pos[:, None] // CHUNK))
    sizes = (A_WIDTH, KV_LATENT, IDX_HEADS * IDX_DIM, IDX_DIM, IDX_HEADS, B_WIDTH, B_WIDTH)
    offs = np.concatenate([[0], np.cumsum(sizes)])
    pad = LANES - IDX_DIM - IDX_HEADS
    rt_tm = _moe_layout(n)[0]
    tri = jnp.asarray(np.triu(np.ones((rt_tm, rt_tm), np.float32), 1), BF16)

    for i in range(depth):
        j = i // 2
        mod_l = mod[i]
        g_mix = norm_mix[i].reshape(1, d)
        if i % 2 == 0:
            wi = w_in[j]
            w_in_r = jnp.concatenate(
                [wi[:, offs[0]:offs[3]], wi[:, offs[3]:offs[5]], jnp.zeros((d, pad), wi.dtype),
                 wi[:, offs[5]:offs[7]]], axis=1).astype(BF16)
            w_s_m = jnp.where(sgu_mask[None], w_s[j], 0.0).astype(BF16)
            b_s_b = jnp.broadcast_to(b_s[j][:, :, None], (B_GROUPS, SGU_CHUNK, B_WIDTH // B_GROUPS))
            q, kv, qi, misc, bo = _in_proj(
                xs, mod_l, g_mix, w_in_r, kv_norm[j].reshape(1, -1), w_kv_up[j].astype(BF16),
                sgu_norm[j].reshape(1, -1), w_s_m, b_s_b, batch, seq)
            a = _attention(q, qi, misc, kv, batch, seq)
            xs = _out_proj(xs, a, bo, w_o[j].astype(BF16), mod_l, batch, seq)
        else:
            xs = _pool_mixer(xs, mod_l, g_mix, w_pool[j].astype(BF16), pool_scale[j].reshape(1, d),
                             batch, seq)
        w_r = jnp.concatenate([w_gr[i], w_er[i],
                               jnp.zeros((d, LANES - N_GROUPS - N_EXPERTS), F32)], axis=1)
        b_r = jnp.concatenate([b_gr[i], b_er[i],
                               jnp.zeros((LANES - N_GROUPS - N_EXPERTS,), F32)]).reshape(1, LANES)
        xs = _moe(xs, mod_l, norm_ffn[i].reshape(1, d), w_r, b_r, tri, w_gate, w_up, w_down, i,
                  norm_out.reshape(1, d), batch, seq, final_norm=(i == depth - 1))
    return xs.reshape(batch, seq, d)
```

```python
import functools

import numpy as np
import jax
import jax.numpy as jnp
from jax import lax
from jax.experimental import pallas as pl
from jax.experimental.pallas import tpu as pltpu

F32 = jnp.float32
BF16 = jnp.bfloat16
I32 = jnp.int32

EPS = 1e-6
CHUNK = 64
CHUNK_SHIFT = 6
Q_BLOCK = 256
PROJ_TM = 1024
POOL_TM = 2048
A_HEADS = 8
A_HEAD_DIM = 64
A_WIDTH = A_HEADS * A_HEAD_DIM
KV_LATENT = 128
IDX_HEADS = 8
IDX_DIM = 64
TOPK_MAX = 256
B_GROUPS = 4
SGU_CHUNK = 128
POOL_WINDOWS = (2, 4, 8, 16)
POOL_HALO = 16
N_GROUPS = 4
EXPERTS_PER_GROUP = 8
N_EXPERTS = N_GROUPS * EXPERTS_PER_GROUP
LANES = 128
NEG_BIG = -1e30
INT_MIN = np.int32(-2 ** 31)
KEY_NEG_INF = np.int32(np.int32(-8388608) ^ np.int32(0x7FFFFFFF))
VMEM_LIMIT = 56 * 1024 * 1024

_NT = (((1,), (1,)), ((), ()))


def _params(*sem):
    return pltpu.CompilerParams(dimension_semantics=sem, vmem_limit_bytes=VMEM_LIMIT)


def _norm_mod(x, g, shift, scale):
    y = x * lax.rsqrt(jnp.mean(x * x, axis=-1, keepdims=True) + EPS)
    return (y * g) * (1.0 + scale) + shift


def _mod_kernel(c_ref, w_ref, b_ref, o_ref):
    c = c_ref[...]
    cs = c * jax.nn.sigmoid(c)
    o_ref[0] = jnp.dot(cs.astype(BF16), w_ref[0].astype(BF16),
                       preferred_element_type=F32) + b_ref[0]


def _modulation(c, w_mod, b_mod):
    depth, d, d6 = w_mod.shape
    b = c.shape[0]
    tn = 1024
    return pl.pallas_call(
        _mod_kernel,
        grid=(depth, d6 // tn),
        in_specs=[pl.BlockSpec((b, d), lambda i, j: (0, 0)),
                  pl.BlockSpec((1, d, tn), lambda i, j: (i, 0, j)),
                  pl.BlockSpec((1, 1, tn), lambda i, j: (i, 0, j))],
        out_specs=pl.BlockSpec((1, b, tn), lambda i, j: (i, 0, j)),
        out_shape=jax.ShapeDtypeStruct((depth, b, d6), F32),
        compiler_params=_params("parallel", "parallel"),
        name="modulation",
    )(c, w_mod, b_mod.reshape(depth, 1, d6))


_C_Q = 0
_C_KV = _C_Q + A_WIDTH
_C_QI = _C_KV + KV_LATENT
_C_MISC = _C_QI + IDX_HEADS * IDX_DIM
_C_U = _C_MISC + LANES
B_WIDTH = 512
_C_V = _C_U + B_WIDTH
_C_END = _C_V + B_WIDTH


def _in_proj_kernel(x_ref, mod_ref, g_ref, w_ref, kvn_ref, wkv_ref, lng_ref, ws_ref, bs_ref,
                    q_ref, kv_ref, qi_ref, misc_ref, b_ref):
    tm = x_ref.shape[0]
    h = _norm_mod(x_ref[...], g_ref[...], mod_ref[0, 0:1, :], mod_ref[0, 1:2, :]).astype(BF16)

    def proj(lo, hi):
        return jnp.dot(h, w_ref[:, lo:hi], preferred_element_type=F32)

    q_ref[...] = proj(_C_Q, _C_KV).astype(BF16)
    qi_ref[...] = proj(_C_QI, _C_MISC).astype(BF16)

    kvl = proj(_C_KV, _C_QI)
    kvn = kvl * lax.rsqrt(jnp.mean(kvl * kvl, axis=-1, keepdims=True) + EPS) * kvn_ref[...]
    kv_ref[...] = jnp.dot(kvn.astype(BF16), wkv_ref[...], preferred_element_type=F32).astype(BF16)

    lane = lax.broadcasted_iota(I32, (1, LANES), 1)
    idx_scale = (IDX_HEADS ** -0.5) * (IDX_DIM ** -0.5)
    w_lane = (lane >= IDX_DIM) & (lane < IDX_DIM + IDX_HEADS)
    misc_ref[...] = proj(_C_MISC, _C_U) * jnp.where(w_lane, idx_scale, 1.0)

    gu = jax.nn.gelu(proj(_C_U, _C_V))
    gv = jax.nn.gelu(proj(_C_V, _C_END))
    mu = jnp.mean(gv, axis=-1, keepdims=True)
    cen = gv - mu
    var = jnp.mean(cen * cen, axis=-1, keepdims=True)
    vn = (cen * lax.rsqrt(var + EPS) * lng_ref[...]).astype(BF16)
    gd = B_WIDTH // B_GROUPS
    for r in range(tm // SGU_CHUNK):
        rows = slice(r * SGU_CHUNK, (r + 1) * SGU_CHUNK)
        for g in range(B_GROUPS):
            cols = slice(g * gd, (g + 1) * gd)
            mixed = jnp.dot(ws_ref[g], vn[rows, cols], preferred_element_type=F32) + bs_ref[g]
            b_ref[rows, cols] = (gu[rows, cols] * mixed).astype(BF16)


def _in_proj(x, mod_l, g, w_in_r, kv_norm, w_kv_up, sgu_norm, w_s_m, b_s_b, batch, seq):
    n, d = x.shape
    tm = min(PROJ_TM, seq)
    tps = seq // tm
    row = lambda b, t: (b * tps + t, 0)
    full2 = lambda b, t: (0, 0)
    full3 = lambda b, t: (0, 0, 0)
    outs = [(A_WIDTH, BF16), (2 * A_HEAD_DIM, BF16), (IDX_HEADS * IDX_DIM, BF16), (LANES, F32),
            (B_WIDTH, BF16)]
    return pl.pallas_call(
        _in_proj_kernel,
        grid=(batch, tps),
        in_specs=[pl.BlockSpec((tm, d), row),
                  pl.BlockSpec((1, 6, d), lambda b, t: (b, 0, 0)),
                  pl.BlockSpec((1, d), full2),
                  pl.BlockSpec(w_in_r.shape, full2),
                  pl.BlockSpec((1, KV_LATENT), full2),
                  pl.BlockSpec(w_kv_up.shape, full2),
                  pl.BlockSpec((1, B_WIDTH), full2),
                  pl.BlockSpec(w_s_m.shape, full3),
                  pl.BlockSpec(b_s_b.shape, full3)],
        out_specs=[pl.BlockSpec((tm, w), row) for w, _ in outs],
        out_shape=[jax.ShapeDtypeStruct((n, w), dt) for w, dt in outs],
        compiler_params=_params("parallel", "parallel"),
        name="in_proj_sgu",
    )(x, mod_l, g, w_in_r, kv_norm, w_kv_up, sgu_norm, w_s_m, b_s_b)


SEL_FIELD_BITS = 10
SUBLANES = 8
ATT_SCALE = A_HEAD_DIM ** -0.5
assert ATT_SCALE == 2.0 ** round(np.log2(ATT_SCALE)), "folded into bf16 q, must be a power of two"


def _fold_rows(x, op=jnp.add):
    acc = x[0:SUBLANES]
    for r in range(1, x.shape[0] // SUBLANES):
        acc = op(acc, x[r * SUBLANES:(r + 1) * SUBLANES])
    return acc


def _radix4_select(key_ref, nkt, kf):
    _, tk, qb = key_ref.shape
    c1 = 1
    c2 = c1 + (1 << SEL_FIELD_BITS)
    c3 = c2 + (1 << (2 * SEL_FIELD_BITS))
    fmask = (1 << SEL_FIELD_BITS) - 1

    def step(i, prefix):
        unit = lax.shift_left(np.int32(1), 30 - 2 * i)
        ts = [(prefix + j * unit) ^ INT_MIN for j in (1, 2, 3)]

        def count_tile(kt, acc):
            key = key_ref[kt]
            hits = jnp.where(key >= ts[2], c3, jnp.where(key >= ts[1], c2, jnp.where(key >= ts[0], c1, 0)))
            return acc + _fold_rows(hits)

        acc = lax.fori_loop(0, nkt, count_tile, jnp.zeros((SUBLANES, qb), I32))
        cnt = [jnp.sum(((acc >> (j * SEL_FIELD_BITS)) & fmask).astype(F32), axis=0, keepdims=True)
               for j in range(3)]
        digit = jnp.where(cnt[2] >= kf, 3, jnp.where(cnt[1] >= kf, 2, jnp.where(cnt[0] >= kf, 1, 0)))
        return prefix + digit * unit

    return lax.fori_loop(0, 16, step, jnp.zeros((1, qb), I32)) ^ INT_MIN


def _attn_kernel(q_ref, qi_ref, mq_ref, mk_ref, kv_ref, o_ref,
                 key_ref, pen_ref, nd_ref, s_ref, m_ref, l_ref, acc_ref, *, topk):
    _, tk, qb = key_ref.shape
    t = pl.program_id(1)
    nkt = t + 1
    kf = float(topk)
    krow = lax.broadcasted_iota(I32, (tk, qb), 0)
    qlane = lax.broadcasted_iota(I32, (tk, qb), 1)
    qchunk = (t * qb + qlane) >> CHUNK_SHIFT
    rel = qlane - krow

    def tile_geom(kt):
        col = kt * tk + krow
        visible = (col >> CHUNK_SHIFT) <= qchunk
        neg_dist = -jnp.abs(rel + (t - kt) * tk).astype(F32)
        return col, visible, neg_dist

    def row_off(kt):
        return pl.multiple_of(kt * tk, tk)

    @pl.when(nkt * tk <= topk)
    def _():
        def fill(kt, carry):
            _, visible, neg_dist = tile_geom(kt)
            pen_ref[kt] = jnp.where(visible, 0.0, NEG_BIG)
            nd_ref[kt] = neg_dist
            return carry

        lax.fori_loop(0, nkt, fill, 0)

    @pl.when(nkt * tk > topk)
    def _():
        qi = qi_ref[...]
        qi_all = jnp.concatenate([qi[:, h * IDX_DIM:(h + 1) * IDX_DIM] for h in range(IDX_HEADS)],
                                 axis=0)
        mq_t = mq_ref[...].T
        w_rows = [mq_t[IDX_DIM + h:IDX_DIM + h + 1, :] for h in range(IDX_HEADS)]

        def score_tile(kt, carry):
            k_idx = mk_ref[pl.ds(row_off(kt), tk), 0:IDX_DIM].astype(BF16)
            logits = lax.dot_general(k_idx, qi_all, _NT, preferred_element_type=F32)
            score = jnp.zeros((tk, qb), F32)
            for h in range(IDX_HEADS):
                score = score + w_rows[h] * jnp.maximum(logits[:, h * qb:(h + 1) * qb], 0.0)
            _, visible, neg_dist = tile_geom(kt)
            score = jnp.where(visible, score, -jnp.inf)
            bits = pltpu.bitcast(score, I32)
            key_ref[kt] = bits ^ ((bits >> 31) & np.int32(0x7FFFFFFF))
            nd_ref[kt] = neg_dist
            return carry

        lax.fori_loop(0, nkt, score_tile, 0)

        thr = _radix4_select(key_ref, nkt, kf)

        def pen_tile(kt, carry):
            n_gt, n_eq = carry
            key = key_ref[kt]
            _, visible, _ = tile_geom(kt)
            pen_ref[kt] = jnp.where((key >= thr) & visible, 0.0, NEG_BIG)
            n_gt = n_gt + _fold_rows(jnp.where(key > thr, 1.0, 0.0))
            n_eq = n_eq + _fold_rows(jnp.where(key == thr, 1.0, 0.0))
            return n_gt, n_eq

        zero = jnp.zeros((SUBLANES, qb), F32)
        n_gt, n_eq = lax.fori_loop(0, nkt, pen_tile, (zero, zero))
        need = kf - jnp.sum(n_gt, axis=0, keepdims=True)
        n_eq = jnp.sum(n_eq, axis=0, keepdims=True)

        tie_rows = jnp.where((n_eq > need) & (thr > KEY_NEG_INF), 1.0, 0.0)

        @pl.when(jnp.max(tie_rows) > 0.0)
        def _():
            nbits = int(key_ref.shape[0] * tk - 1).bit_length()

            def idx_step(i, p):
                trial = p | lax.shift_left(np.int32(1), nbits - 1 - i)

                def count_tile(kt, acc):
                    col, _, _ = tile_geom(kt)
                    hit = (key_ref[kt] == thr) & (col < trial)
                    return acc + _fold_rows(jnp.where(hit, 1.0, 0.0))

                acc = lax.fori_loop(0, nkt, count_tile, jnp.zeros((SUBLANES, qb), F32))
                cnt = jnp.sum(acc, axis=0, keepdims=True)
                return jnp.where(cnt < need, trial, p)

            last = lax.fori_loop(0, nbits, idx_step, jnp.zeros((1, qb), I32))

            def repen_tile(kt, carry):
                key = key_ref[kt]
                col, visible, _ = tile_geom(kt)
                sel = (key > thr) | ((key == thr) & (col <= last))
                pen_ref[kt] = jnp.where(sel & visible, 0.0, NEG_BIG)
                return carry

            lax.fori_loop(0, nkt, repen_tile, 0)

    qs = (q_ref[...].astype(F32) * ATT_SCALE).astype(BF16)
    q_all = jnp.concatenate([qs[:, h * A_HEAD_DIM:(h + 1) * A_HEAD_DIM] for h in range(A_HEADS)],
                            axis=0)
    for h in range(A_HEADS):
        m_ref[h] = jnp.full((SUBLANES, qb), NEG_BIG, F32)
        l_ref[h] = jnp.zeros((SUBLANES, qb), F32)
        acc_ref[h] = jnp.zeros((A_HEAD_DIM, qb), F32)

    def logit_tile(kt, carry):
        kk = kv_ref[pl.ds(row_off(kt), tk), 0:A_HEAD_DIM]
        pen = pen_ref[kt]
        nd = nd_ref[kt]
        s_all = lax.dot_general(kk, q_all, _NT, preferred_element_type=F32)
        for h in range(A_HEADS):
            slope = float(2.0 ** (-8.0 * (h + 1) / A_HEADS))
            s = s_all[:, h * qb:(h + 1) * qb] + (slope * nd + pen)
            s_ref[h, kt] = s
            m_ref[h] = jnp.maximum(m_ref[h], _fold_rows(s, jnp.maximum))
        return carry

    lax.fori_loop(0, nkt, logit_tile, 0)
    m_fin = [jnp.max(m_ref[h], axis=0, keepdims=True) for h in range(A_HEADS)]

    def prob_tile(kt, carry):
        v_t = kv_ref[pl.ds(row_off(kt), tk), A_HEAD_DIM:2 * A_HEAD_DIM].T
        probs = []
        for h in range(A_HEADS):
            p = jnp.exp(s_ref[h, kt] - m_fin[h])
            l_ref[h] = l_ref[h] + _fold_rows(p)
            probs.append(p.astype(BF16))
        pv = jnp.dot(v_t, jnp.concatenate(probs, axis=1), preferred_element_type=F32)
        for h in range(A_HEADS):
            acc_ref[h] = acc_ref[h] + pv[:, h * qb:(h + 1) * qb]
        return carry

    lax.fori_loop(0, nkt, prob_tile, 0)
    outs = [(acc_ref[h] / jnp.sum(l_ref[h], axis=0, keepdims=True)).T for h in range(A_HEADS)]
    o_ref[...] = jnp.concatenate(outs, axis=-1).astype(BF16)


def _attention(q, qi, misc, kv, batch, seq):
    n = q.shape[0]
    qb = min(Q_BLOCK, seq)
    nb = seq // qb
    topk = min(TOPK_MAX, seq // 4)
    assert seq // SUBLANES < (1 << SEL_FIELD_BITS)
    rowq = lambda b, t: (b * nb + t, 0)
    rowk = lambda b, t: (b, 0)
    tiles = (nb, qb, qb)
    return pl.pallas_call(
        functools.partial(_attn_kernel, topk=topk),
        grid=(batch, nb),
        in_specs=[pl.BlockSpec((qb, A_WIDTH), rowq),
                  pl.BlockSpec((qb, IDX_HEADS * IDX_DIM), rowq),
                  pl.BlockSpec((qb, LANES), rowq),
                  pl.BlockSpec((seq, LANES), rowk),
                  pl.BlockSpec((seq, 2 * A_HEAD_DIM), rowk)],
        out_specs=pl.BlockSpec((qb, A_WIDTH), rowq),
        out_shape=jax.ShapeDtypeStruct((n, A_WIDTH), BF16),
        scratch_shapes=[pltpu.VMEM(tiles, I32), pltpu.VMEM(tiles, F32), pltpu.VMEM(tiles, F32),
                        pltpu.VMEM((A_HEADS,) + tiles, F32),
                        pltpu.VMEM((A_HEADS, SUBLANES, qb), F32),
                        pltpu.VMEM((A_HEADS, SUBLANES, qb), F32),
                        pltpu.VMEM((A_HEADS, A_HEAD_DIM, qb), F32)],
        compiler_params=_params("parallel", "parallel"),
        name="dsa_attention",
    )(q, qi, misc, misc, kv)


def _out_proj_kernel(x_ref, a_ref, b_ref, w_ref, mod_ref, o_ref):
    y = jnp.dot(a_ref[...], w_ref[0:A_WIDTH, :], preferred_element_type=F32)
    y = y + jnp.dot(b_ref[...], w_ref[A_WIDTH:, :], preferred_element_type=F32)
    o_ref[...] = x_ref[...] + mod_ref[0, 2:3, :] * y


def _out_proj(x, a, bo, w_o, mod_l, batch, seq):
    n, d = x.shape
    tm = min(PROJ_TM, seq)
    tps = seq // tm
    row = lambda b, t: (b * tps + t, 0)
    return pl.pallas_call(
        _out_proj_kernel,
        grid=(batch, tps),
        in_specs=[pl.BlockSpec((tm, d), row),
                  pl.BlockSpec((tm, A_WIDTH), row),
                  pl.BlockSpec((tm, B_WIDTH), row),
                  pl.BlockSpec(w_o.shape, lambda b, t: (0, 0)),
                  pl.BlockSpec((1, 6, d), lambda b, t: (b, 0, 0))],
        out_specs=pl.BlockSpec((tm, d), row),
        out_shape=jax.ShapeDtypeStruct((n, d), F32),
        compiler_params=_params("parallel", "parallel"),
        name="out_proj",
    )(x, a, bo, w_o, mod_l)


def _pool_kernel(x_ref, halo_ref, mod_ref, g_ref, w_ref, sc_ref, o_ref):
    tm, d = x_ref.shape
    t = pl.program_id(1)
    g = g_ref[...]
    shift = mod_ref[0, 0:1, :]
    scale = mod_ref[0, 1:2, :]
    x = x_ref[...]
    h = _norm_mod(x, g, shift, scale)
    hh = _norm_mod(halo_ref[...], g, shift, scale)
    hh = jnp.where(t > 0, hh, 0.0)
    ext = jnp.concatenate([hh, h], axis=0)
    pos = t * tm + lax.broadcasted_iota(I32, (tm, 1), 0)
    gd = d // len(POOL_WINDOWS)
    ys = []
    for gi, win in enumerate(POOL_WINDOWS):
        cols = slice(gi * gd, (gi + 1) * gd)
        cur = ext[:, cols]
        k = 1
        while k < win:
            cur = cur + pltpu.roll(cur, k, axis=0)
            k *= 2
        cnt = jnp.minimum(pos + 1, win).astype(F32)
        y = (cur[POOL_HALO:, :] / cnt - h[:, cols]).astype(BF16)
        ys.append(jnp.dot(y, w_ref[gi], preferred_element_type=F32))
    y = jnp.concatenate(ys, axis=-1) * sc_ref[...]
    o_ref[...] = x + mod_ref[0, 2:3, :] * y


def _pool_mixer(x, mod_l, g, w_pool, scale, batch, seq):
    n, d = x.shape
    tm = min(POOL_TM, seq)
    tps = seq // tm
    row = lambda b, t: (b * tps + t, 0)
    hpt = tm // POOL_HALO
    halo = lambda b, t: (jnp.maximum((b * tps + t) * hpt - 1, 0), 0)
    return pl.pallas_call(
        _pool_kernel,
        grid=(batch, tps),
        in_specs=[pl.BlockSpec((tm, d), row),
                  pl.BlockSpec((POOL_HALO, d), halo),
                  pl.BlockSpec((1, 6, d), lambda b, t: (b, 0, 0)),
                  pl.BlockSpec((1, d), lambda b, t: (0, 0)),
                  pl.BlockSpec(w_pool.shape, lambda b, t: (0, 0, 0)),
                  pl.BlockSpec((1, d), lambda b, t: (0, 0))],
        out_specs=pl.BlockSpec((tm, d), row),
        out_shape=jax.ShapeDtypeStruct((n, d), F32),
        compiler_params=_params("parallel", "parallel"),
        name="pool_mixer",
    )(x, x, mod_l, g, w_pool, scale)


_R_EXP0 = N_GROUPS
RT_TM = 512
SEG_ALIGN = 16
SEG_SHIFT = 4
E_TM = 1152
EXPERT_STEP = 4

_TN = (((0,), (0,)), ((), ()))


def _moe_layout(n):
    tm = min(RT_TM, n)
    xs_rows = tm + N_GROUPS * SEG_ALIGN
    e_tm = E_TM if n >= 4 * E_TM else 128
    worst = n + (n // tm) * N_GROUPS * SEG_ALIGN + N_GROUPS * e_tm
    rows = -(-worst // e_tm) * e_tm
    return tm, xs_rows, e_tm, rows


def _perm_matrix(d_row, rows):
    tm = d_row.shape[1]
    return jnp.where(lax.broadcasted_iota(I32, (rows, tm), 0) == d_row, 1.0, 0.0).astype(BF16)


def _split_bf16(a):
    hi = a.astype(BF16)
    return hi, (a - hi.astype(F32)).astype(BF16)


def _route_kernel(x_ref, mod_ref, g_ref, whi_ref, wlo_ref, b_ref, tri_ref,
                  xs_ref, gs_ref, d_ref, cnt_ref):
    tm = x_ref.shape[0]
    xs_rows = xs_ref.shape[0]
    h = _norm_mod(x_ref[...], g_ref[...], mod_ref[0, 3:4, :], mod_ref[0, 4:5, :])
    h_hi, h_lo = _split_bf16(h)
    logits = (jnp.dot(h_hi, wlo_ref[...], preferred_element_type=F32)
              + jnp.dot(h_lo, whi_ref[...], preferred_element_type=F32)
              + jnp.dot(h_hi, whi_ref[...], preferred_element_type=F32)) + b_ref[...]
    lane = lax.broadcasted_iota(I32, (tm, LANES), 1)
    big = np.int32(LANES)

    def top1(vals):
        m = jnp.max(vals, axis=-1, keepdims=True)
        idx = jnp.min(jnp.where(vals == m, lane, big), axis=-1, keepdims=True)
        return m, idx

    gl = jnp.where(lane < N_GROUPS, logits, -jnp.inf)
    gmax, gsel = top1(gl)
    p_g = 1.0 / jnp.sum(jnp.exp(gl - gmax), axis=-1, keepdims=True)
    e_lo = _R_EXP0 + gsel * EXPERTS_PER_GROUP
    el = jnp.where((lane >= e_lo) & (lane < e_lo + EXPERTS_PER_GROUP), logits, -jnp.inf)
    v1, i1 = top1(el)
    v2, i2 = top1(jnp.where(lane == i1, -jnp.inf, el))
    e2 = jnp.exp(v2 - v1)
    g1 = p_g / (1.0 + e2)
    g2 = p_g * e2 / (1.0 + e2)
    cg = jnp.where(lane == i1 - e_lo, g1, 0.0) + jnp.where(lane == i2 - e_lo, g2, 0.0)

    member = jnp.where(lane == gsel, 1.0, 0.0).T[0:8, :]
    rank = jnp.dot(member.astype(BF16), tri_ref[...], preferred_element_type=F32)
    count = jnp.sum(member, axis=1, keepdims=True)
    cnt_ref[0] = jnp.broadcast_to(count, (8, LANES))
    padded = jnp.ceil(count * (1.0 / SEG_ALIGN)) * SEG_ALIGN
    d_row = jnp.zeros((1, tm), F32)
    seg = jnp.zeros((1, 1), F32)
    for g in range(N_GROUPS):
        d_row = d_row + member[g:g + 1, :] * (rank[g:g + 1, :] + seg)
        seg = seg + padded[g:g + 1, :]
    d_row = d_row.astype(I32)
    d_ref[0] = d_row

    perm = _perm_matrix(d_row, xs_rows)
    d = h_hi.shape[1]
    cg_hi, cg_lo = _split_bf16(cg)
    srt = jnp.dot(perm, jnp.concatenate([h_hi, cg_hi, cg_lo], axis=1), preferred_element_type=F32)
    xs_ref[...] = srt[:, 0:d].astype(BF16)
    gs_ref[...] = srt[:, d:d + LANES] + srt[:, d + LANES:d + 2 * LANES]


def _route(x, mod_l, g, w_r, b_r, tri, batch, seq):
    n, d = x.shape
    tm, xs_rows, _, _ = _moe_layout(n)
    nt = n // tm
    tps = seq // tm
    full = lambda i: (0, 0)
    return pl.pallas_call(
        _route_kernel,
        grid=(nt,),
        in_specs=[pl.BlockSpec((tm, d), lambda i: (i, 0)),
                  pl.BlockSpec((1, 6, d), lambda i: (i // tps, 0, 0)),
                  pl.BlockSpec((1, d), full),
                  pl.BlockSpec((d, LANES), full),
                  pl.BlockSpec((d, LANES), full),
                  pl.BlockSpec((1, LANES), full),
                  pl.BlockSpec((tm, tm), full)],
        out_specs=[pl.BlockSpec((xs_rows, d), lambda i: (i, 0)),
                   pl.BlockSpec((xs_rows, LANES), lambda i: (i, 0)),
                   pl.BlockSpec((1, 1, tm), lambda i: (i, 0, 0)),
                   pl.BlockSpec((1, 8, LANES), lambda i: (i, 0, 0))],
        out_shape=[jax.ShapeDtypeStruct((nt * xs_rows, d), BF16),
                   jax.ShapeDtypeStruct((nt * xs_rows, LANES), F32),
                   jax.ShapeDtypeStruct((nt, 1, tm), I32),
                   jax.ShapeDtypeStruct((nt, 8, LANES), F32)],
        compiler_params=_params("parallel"),
        name="route_sort",
    )(x, mod_l, g, *_split_bf16(w_r), b_r, tri)


def _dispatch_kernel(src_ref, dst_ref, len_ref, fill_ref, xt_ref, gt_ref, xs_hbm, gs_hbm,
                     zx_ref, zg_ref, sem_x, sem_g):
    i = pl.program_id(0)
    nt = src_ref.shape[0]

    def chunk_copies(x_src, g_src, src_row, dst_row):
        src_row = pl.multiple_of(src_row, SEG_ALIGN)
        dst_row = pl.multiple_of(dst_row, SEG_ALIGN)
        return (pltpu.make_async_copy(x_src.at[pl.ds(src_row, SEG_ALIGN)],
                                      xs_hbm.at[pl.ds(dst_row, SEG_ALIGN)], sem_x),
                pltpu.make_async_copy(g_src.at[pl.ds(src_row, SEG_ALIGN)],
                                      gs_hbm.at[pl.ds(dst_row, SEG_ALIGN)], sem_g))

    def run(x_src, g_src, jobs):
        total = jnp.int32(0)
        for src, dst, rows in jobs:
            nch = rows >> SEG_SHIFT

            def issue(k, carry, src=src, dst=dst):
                for cp in chunk_copies(x_src, g_src, src + k * SEG_ALIGN, dst + k * SEG_ALIGN):
                    cp.start()
                return carry

            lax.fori_loop(0, nch, issue, 0)
            total = total + nch

        def drain(k, carry):
            for cp in chunk_copies(x_src, g_src, 0, 0):
                cp.wait()
            return carry

        lax.fori_loop(0, total, drain, 0)

    @pl.when(i < nt)
    def _():
        t = jnp.minimum(i, nt - 1)
        run(xt_ref, gt_ref, [(src_ref[t, g], dst_ref[t, g], len_ref[t, g]) for g in range(N_GROUPS)])

    @pl.when(i == nt)
    def _():
        zx_ref[...] = jnp.zeros_like(zx_ref)
        zg_ref[...] = jnp.zeros_like(zg_ref)
        for g in range(N_GROUPS + 1):
            start = fill_ref[0, g]
            nch = fill_ref[1, g] >> SEG_SHIFT

            def issue(k, carry, start=start):
                for cp in chunk_copies(zx_ref, zg_ref, 0, start + k * SEG_ALIGN):
                    cp.start()
                return carry

            lax.fori_loop(0, nch, issue, 0)

            def drain(k, carry):
                for cp in chunk_copies(zx_ref, zg_ref, 0, 0):
                    cp.wait()
                return carry

            lax.fori_loop(0, nch, drain, 0)


def _dispatch(xt, gt, src_off, dst_off, seg_len, fill, rows):
    d = xt.shape[1]
    nt = src_off.shape[0]
    xs_rows = xt.shape[0] // nt
    tile = lambda i, *_: (jnp.minimum(i, nt - 1), 0)
    return pl.pallas_call(
        _dispatch_kernel,
        grid_spec=pltpu.PrefetchScalarGridSpec(
            num_scalar_prefetch=4,
            grid=(nt + 1,),
            in_specs=[pl.BlockSpec((xs_rows, d), tile), pl.BlockSpec((xs_rows, LANES), tile)],
            out_specs=[pl.BlockSpec(memory_space=pl.ANY), pl.BlockSpec(memory_space=pl.ANY)],
            scratch_shapes=[pltpu.VMEM((SEG_ALIGN, d), BF16), pltpu.VMEM((SEG_ALIGN, LANES), F32),
                            pltpu.SemaphoreType.DMA, pltpu.SemaphoreType.DMA]),
        out_shape=[jax.ShapeDtypeStruct((rows, d), BF16),
                   jax.ShapeDtypeStruct((rows, LANES), F32)],
        compiler_params=_params("arbitrary"),
        name="dispatch",
    )(src_off, dst_off, seg_len, fill, xt, gt)


def _experts_kernel(grp_ref, valid_ref, x_ref, gs_ref, wg_ref, wu_ref, wd_ref, o_ref):
    j = pl.program_id(0)
    e = pl.program_id(1)

    @pl.when(e == 0)
    def _():
        o_ref[...] = jnp.zeros_like(o_ref)

    @pl.when(valid_ref[j] > 0)
    def _():
        x = x_ref[...]
        es, f, d = wd_ref.shape[1:]
        lane = lax.broadcasted_iota(I32, gs_ref.shape, 1)
        acts = []
        for k in range(es):
            hg = jnp.dot(x, wg_ref[0, k].astype(BF16), preferred_element_type=F32)
            hu = jnp.dot(x, wu_ref[0, k].astype(BF16), preferred_element_type=F32)
            gate = jnp.sum(jnp.where(lane == e * es + k, gs_ref[...], 0.0), axis=-1, keepdims=True)
            acts.append(((hg * jax.nn.sigmoid(hg)) * hu * gate).astype(BF16))
        wd = wd_ref[0].astype(BF16).reshape(es * f, d)
        o_ref[...] += jnp.dot(jnp.concatenate(acts, axis=1), wd, preferred_element_type=F32)


def _experts(xs, gs, tile_grp, tile_valid, wg, wu, wd, layer, e_tm):
    rows, d = xs.shape
    f = wg.shape[-1]
    ntile = tile_grp.shape[0]
    row = lambda j, e, grp, valid: (j, 0)
    steps = EXPERTS_PER_GROUP // EXPERT_STEP
    wsel = lambda j, e, grp, valid: (
        layer, grp[j] * steps + jnp.where(valid[j] > 0, e, steps - 1), 0, 0)
    return pl.pallas_call(
        _experts_kernel,
        grid_spec=pltpu.PrefetchScalarGridSpec(
            num_scalar_prefetch=2,
            grid=(ntile, steps),
            in_specs=[pl.BlockSpec((e_tm, d), row),
                      pl.BlockSpec((e_tm, LANES), row),
                      pl.BlockSpec((1, EXPERT_STEP, d, f), wsel),
                      pl.BlockSpec((1, EXPERT_STEP, d, f), wsel),
                      pl.BlockSpec((1, EXPERT_STEP, f, d), wsel)],
            out_specs=pl.BlockSpec((e_tm, d), row)),
        out_shape=jax.ShapeDtypeStruct((rows, d), F32),
        compiler_params=_params("arbitrary", "arbitrary"),
        name="experts",
    )(tile_grp, tile_valid, xs, gs, wg, wu, wd)


def _combine_kernel(meta_ref, x_ref, d_ref, mod_ref, gout_ref, ys_hbm, o_ref, yseg, sem, *, final_norm):
    i = pl.program_id(0)
    nt = pl.num_programs(0)
    xs_rows = yseg.shape[1]

    def seg_copy(slot, src_row, dst_row):
        src_row = pl.multiple_of(src_row, SEG_ALIGN)
        dst_row = pl.multiple_of(dst_row, SEG_ALIGN)
        return pltpu.make_async_copy(ys_hbm.at[pl.ds(src_row, SEG_ALIGN)],
                                     yseg.at[slot, pl.ds(dst_row, SEG_ALIGN)], sem.at[slot])

    def fetch(tile, slot):
        yseg[slot] = jnp.zeros(yseg.shape[1:], F32)
        seg = jnp.int32(0)
        for g in range(N_GROUPS):
            base = meta_ref[tile, g]
            c16 = meta_ref[tile, N_GROUPS + g]

            def issue(k, carry, seg=seg, base=base):
                seg_copy(slot, base + k * SEG_ALIGN, seg + k * SEG_ALIGN).start()
                return carry

            lax.fori_loop(0, c16 >> SEG_SHIFT, issue, 0)
            seg = seg + c16

    @pl.when(i == 0)
    def _():
        fetch(0, 0)

    @pl.when(i + 1 < nt)
    def _():
        fetch(jnp.minimum(i + 1, nt - 1), (i + 1) % 2)

    slot = i % 2
    total_chunks = jnp.int32(0)
    for g in range(N_GROUPS):
        total_chunks = total_chunks + (meta_ref[i, N_GROUPS + g] >> SEG_SHIFT)

    def drain(k, carry):
        seg_copy(slot, 0, 0).wait()
        return carry

    lax.fori_loop(0, total_chunks, drain, 0)

    perm = _perm_matrix(d_ref[0], xs_rows)
    y_hi, y_lo = _split_bf16(yseg[slot])
    y = (lax.dot_general(perm, y_hi, _TN, preferred_element_type=F32)
         + lax.dot_general(perm, y_lo, _TN, preferred_element_type=F32))
    out = x_ref[...] + mod_ref[0, 5:6, :] * y
    if final_norm:
        out = out * lax.rsqrt(jnp.mean(out * out, axis=-1, keepdims=True) + EPS) * gout_ref[...]
    o_ref[...] = out


def _combine(x, d_rows, meta, ys, mod_l, g_out, batch, seq, final_norm):
    n, d = x.shape
    tm, xs_rows, _, _ = _moe_layout(n)
    tps = seq // tm
    return pl.pallas_call(
        functools.partial(_combine_kernel, final_norm=final_norm),
        grid_spec=pltpu.PrefetchScalarGridSpec(
            num_scalar_prefetch=1,
            grid=(n // tm,),
            in_specs=[pl.BlockSpec((tm, d), lambda i, meta: (i, 0)),
                      pl.BlockSpec((1, 1, tm), lambda i, meta: (i, 0, 0)),
                      pl.BlockSpec((1, 6, d), lambda i, meta: (i // tps, 0, 0)),
                      pl.BlockSpec((1, d), lambda i, meta: (0, 0)),
                      pl.BlockSpec(memory_space=pl.ANY)],
            out_specs=pl.BlockSpec((tm, d), lambda i, meta: (i, 0)),
            scratch_shapes=[pltpu.VMEM((2, xs_rows, d), F32), pltpu.SemaphoreType.DMA((2,))]),
        out_shape=jax.ShapeDtypeStruct((n, d), F32),
        compiler_params=_params("arbitrary"),
        name="combine",
    )(meta, x, d_rows, mod_l, g_out, ys)


def _moe_plan(cnt, n):
    _, _, e_tm, rows = _moe_layout(n)
    seg_len = ((cnt + (SEG_ALIGN - 1)) // SEG_ALIGN) * SEG_ALIGN
    totals = jnp.sum(seg_len, axis=0)
    region = ((totals + (e_tm - 1)) // e_tm) * e_tm
    region_end = jnp.cumsum(region)
    region_start = region_end - region
    src_off = jnp.cumsum(seg_len, axis=1) - seg_len
    dst_off = region_start[None, :] + jnp.cumsum(seg_len, axis=0) - seg_len
    fill_start = jnp.concatenate([region_start + totals, region_end[-1:]])
    fill_len = jnp.concatenate([region - totals, rows - region_end[-1:]])
    fill = jnp.stack([fill_start, fill_len]).astype(I32)
    j = jnp.arange(rows // e_tm, dtype=I32) * e_tm
    valid = j < region_end[-1]
    grp = jnp.minimum(jnp.sum(j[:, None] >= region_end[None, :], axis=1), N_GROUPS - 1)
    last_grp = jnp.max(jnp.where(region > 0, jnp.arange(N_GROUPS), 0))
    grp = jnp.where(valid, grp, last_grp).astype(I32)
    return (src_off.astype(I32), dst_off.astype(I32), seg_len.astype(I32), fill, grp,
            valid.astype(I32))


def _moe(x, mod_l, g_ffn, w_r, b_r, tri, wg, wu, wd, layer, g_out, batch, seq, final_norm):
    n = x.shape[0]
    _, _, e_tm, rows = _moe_layout(n)
    xt, gt, d_rows, cnt = _route(x, mod_l, g_ffn, w_r, b_r, tri, batch, seq)
    cnt = cnt[:, 0:N_GROUPS, 0].astype(I32)
    src_off, dst_off, seg_len, fill, grp, valid = _moe_plan(cnt, n)
    xs, gs = _dispatch(xt, gt, src_off, dst_off, seg_len, fill, rows)
    ys = _experts(xs, gs, grp, valid, wg, wu, wd, layer, e_tm)
    meta = jnp.concatenate([dst_off, seg_len], axis=1)
    return _combine(x, d_rows, meta, ys, mod_l, g_out, batch, seq, final_norm)


def kernel(x, c, w_mod, b_mod, norm_mix, norm_ffn, w_in, kv_norm, w_kv_up, sgu_norm, w_s, b_s, w_o,
           w_pool, pool_scale, w_gr, b_gr, w_er, b_er, w_gate, w_up, w_down, norm_out):
    batch, seq, d = x.shape
    depth = w_mod.shape[0]
    n = batch * seq
    xs = x.reshape(n, d)
    mod = _modulation(c, w_mod, b_mod).reshape(depth, batch, 6, d)

    pos = np.arange(SGU_CHUNK)
    sgu_mask = jnp.asarray((pos[None, :] // CHUNK) <= (pos[:, None] // CHUNK))
    sizes = (A_WIDTH, KV_LATENT, IDX_HEADS * IDX_DIM, IDX_DIM, IDX_HEADS, B_WIDTH, B_WIDTH)
    offs = np.concatenate([[0], np.cumsum(sizes)])
    pad = LANES - IDX_DIM - IDX_HEADS
    rt_tm = _moe_layout(n)[0]
    tri = jnp.asarray(np.triu(np.ones((rt_tm, rt_tm), np.float32), 1), BF16)

    for i in range(depth):
        j = i // 2
        mod_l = mod[i]
        g_mix = norm_mix[i].reshape(1, d)
        if i % 2 == 0:
            wi = w_in[j]
            w_in_r = jnp.concatenate(
                [wi[:, offs[0]:offs[3]], wi[:, offs[3]:offs[5]], jnp.zeros((d, pad), wi.dtype),
                 wi[:, offs[5]:offs[7]]], axis=1).astype(BF16)
            w_s_m = jnp.where(sgu_mask[None], w_s[j], 0.0).astype(BF16)
            b_s_b = jnp.broadcast_to(b_s[j][:, :, None], (B_GROUPS, SGU_CHUNK, B_WIDTH // B_GROUPS))
            q, kv, qi, misc, bo = _in_proj(
                xs, mod_l, g_mix, w_in_r, kv_norm[j].reshape(1, -1), w_kv_up[j].astype(BF16),
                sgu_norm[j].reshape(1, -1), w_s_m, b_s_b, batch, seq)
            a = _attention(q, qi, misc, kv, batch, seq)
            xs = _out_proj(xs, a, bo, w_o[j].astype(BF16), mod_l, batch, seq)
        else:
            xs = _pool_mixer(xs, mod_l, g_mix, w_pool[j].astype(BF16), pool_scale[j].reshape(1, d),
                             batch, seq)
        w_r = jnp.concatenate([w_gr[i], w_er[i],
                               jnp.zeros((d, LANES - N_GROUPS - N_EXPERTS), F32)], axis=1)
        b_r = jnp.concatenate([b_gr[i], b_er[i],
                               jnp.zeros((LANES - N_GROUPS - N_EXPERTS,), F32)]).reshape(1, LANES)
        xs = _moe(xs, mod_l, norm_ffn[i].reshape(1, d), w_r, b_r, tri, w_gate, w_up, w_down, i,
                  norm_out.reshape(1, d), batch, seq, final_norm=(i == depth - 1))
    return xs.reshape(batch, seq, d)
```

```python
import functools

import numpy as np
import jax
import jax.numpy as jnp
from jax import lax
from jax.experimental import pallas as pl
from jax.experimental.pallas import tpu as pltpu

F32 = jnp.float32
BF16 = jnp.bfloat16
I32 = jnp.int32

EPS = 1e-6
CHUNK = 64
CHUNK_SHIFT = 6
Q_BLOCK = 256
PROJ_TM = 1024
POOL_TM = 2048
A_HEADS = 8
A_HEAD_DIM = 64
A_WIDTH = A_HEADS * A_HEAD_DIM
KV_LATENT = 128
IDX_HEADS = 8
IDX_DIM = 64
TOPK_MAX = 256
B_GROUPS = 4
SGU_CHUNK = 128
POOL_WINDOWS = (2, 4, 8, 16)
POOL_HALO = 16
N_GROUPS = 4
EXPERTS_PER_GROUP = 8
N_EXPERTS = N_GROUPS * EXPERTS_PER_GROUP
LANES = 128
NEG_BIG = -1e30
INT_MIN = np.int32(-2 ** 31)
KEY_NEG_INF = np.int32(np.int32(-8388608) ^ np.int32(0x7FFFFFFF))
VMEM_LIMIT = 56 * 1024 * 1024

_NT = (((1,), (1,)), ((), ()))


def _params(*sem):
    return pltpu.CompilerParams(dimension_semantics=sem, vmem_limit_bytes=VMEM_LIMIT)


def _norm_mod(x, g, shift, scale):
    y = x * lax.rsqrt(jnp.mean(x * x, axis=-1, keepdims=True) + EPS)
    return (y * g) * (1.0 + scale) + shift


def _mod_kernel(c_ref, w_ref, b_ref, o_ref):
    c = c_ref[...]
    cs = c * jax.nn.sigmoid(c)
    o_ref[0] = jnp.dot(cs.astype(BF16), w_ref[0].astype(BF16),
                       preferred_element_type=F32) + b_ref[0]


def _modulation(c, w_mod, b_mod):
    depth, d, d6 = w_mod.shape
    b = c.shape[0]
    tn = 1024
    return pl.pallas_call(
        _mod_kernel,
        grid=(depth, d6 // tn),
        in_specs=[pl.BlockSpec((b, d), lambda i, j: (0, 0)),
                  pl.BlockSpec((1, d, tn), lambda i, j: (i, 0, j)),
                  pl.BlockSpec((1, 1, tn), lambda i, j: (i, 0, j))],
        out_specs=pl.BlockSpec((1, b, tn), lambda i, j: (i, 0, j)),
        out_shape=jax.ShapeDtypeStruct((depth, b, d6), F32),
        compiler_params=_params("parallel", "parallel"),
        name="modulation",
    )(c, w_mod, b_mod.reshape(depth, 1, d6))


_C_Q = 0
_C_KV = _C_Q + A_WIDTH
_C_QI = _C_KV + KV_LATENT
_C_MISC = _C_QI + IDX_HEADS * IDX_DIM
_C_U = _C_MISC + LANES
B_WIDTH = 512
_C_V = _C_U + B_WIDTH
_C_END = _C_V + B_WIDTH


def _in_proj_kernel(x_ref, mod_ref, g_ref, w_ref, kvn_ref, wkv_ref, lng_ref, ws_ref, bs_ref,
                    q_ref, kv_ref, qi_ref, misc_ref, b_ref):
    tm = x_ref.shape[0]
    h = _norm_mod(x_ref[...], g_ref[...], mod_ref[0, 0:1, :], mod_ref[0, 1:2, :]).astype(BF16)

    def proj(lo, hi):
        return jnp.dot(h, w_ref[:, lo:hi], preferred_element_type=F32)

    q_ref[...] = proj(_C_Q, _C_KV).astype(BF16)
    qi_ref[...] = proj(_C_QI, _C_MISC).astype(BF16)

    kvl = proj(_C_KV, _C_QI)
    kvn = kvl * lax.rsqrt(jnp.mean(kvl * kvl, axis=-1, keepdims=True) + EPS) * kvn_ref[...]
    kv_ref[...] = jnp.dot(kvn.astype(BF16), wkv_ref[...], preferred_element_type=F32).astype(BF16)

    lane = lax.broadcasted_iota(I32, (1, LANES), 1)
    idx_scale = (IDX_HEADS ** -0.5) * (IDX_DIM ** -0.5)
    w_lane = (lane >= IDX_DIM) & (lane < IDX_DIM + IDX_HEADS)
    misc_ref[...] = proj(_C_MISC, _C_U) * jnp.where(w_lane, idx_scale, 1.0)

    gu = jax.nn.gelu(proj(_C_U, _C_V))
    gv = jax.nn.gelu(proj(_C_V, _C_END))
    mu = jnp.mean(gv, axis=-1, keepdims=True)
    cen = gv - mu
    var = jnp.mean(cen * cen, axis=-1, keepdims=True)
    vn = (cen * lax.rsqrt(var + EPS) * lng_ref[...]).astype(BF16)
    gd = B_WIDTH // B_GROUPS
    for r in range(tm // SGU_CHUNK):
        rows = slice(r * SGU_CHUNK, (r + 1) * SGU_CHUNK)
        for g in range(B_GROUPS):
            cols = slice(g * gd, (g + 1) * gd)
            mixed = jnp.dot(ws_ref[g], vn[rows, cols], preferred_element_type=F32) + bs_ref[g]
            b_ref[rows, cols] = (gu[rows, cols] * mixed).astype(BF16)


def _in_proj(x, mod_l, g, w_in_r, kv_norm, w_kv_up, sgu_norm, w_s_m, b_s_b, batch, seq):
    n, d = x.shape
    tm = min(PROJ_TM, seq)
    tps = seq // tm
    row = lambda b, t: (b * tps + t, 0)
    full2 = lambda b, t: (0, 0)
    full3 = lambda b, t: (0, 0, 0)
    outs = [(A_WIDTH, BF16), (2 * A_HEAD_DIM, BF16), (IDX_HEADS * IDX_DIM, BF16), (LANES, F32),
            (B_WIDTH, BF16)]
    return pl.pallas_call(
        _in_proj_kernel,
        grid=(batch, tps),
        in_specs=[pl.BlockSpec((tm, d), row),
                  pl.BlockSpec((1, 6, d), lambda b, t: (b, 0, 0)),
                  pl.BlockSpec((1, d), full2),
                  pl.BlockSpec(w_in_r.shape, full2),
                  pl.BlockSpec((1, KV_LATENT), full2),
                  pl.BlockSpec(w_kv_up.shape, full2),
                  pl.BlockSpec((1, B_WIDTH), full2),
                  pl.BlockSpec(w_s_m.shape, full3),
                  pl.BlockSpec(b_s_b.shape, full3)],
        out_specs=[pl.BlockSpec((tm, w), row) for w, _ in outs],
        out_shape=[jax.ShapeDtypeStruct((n, w), dt) for w, dt in outs],
        compiler_params=_params("parallel", "parallel"),
        name="in_proj_sgu",
    )(x, mod_l, g, w_in_r, kv_norm, w_kv_up, sgu_norm, w_s_m, b_s_b)


SEL_FIELD_BITS = 10
SUBLANES = 8
ATT_SCALE = A_HEAD_DIM ** -0.5
assert ATT_SCALE == 2.0 ** round(np.log2(ATT_SCALE)), "folded into bf16 q, must be a power of two"


def _fold_rows(x, op=jnp.add):
    acc = x[0:SUBLANES]
    for r in range(1, x.shape[0] // SUBLANES):
        acc = op(acc, x[r * SUBLANES:(r + 1) * SUBLANES])
    return acc


def _radix4_select(key_ref, nkt, kf):
    _, tk, qb = key_ref.shape
    c1 = 1
    c2 = c1 + (1 << SEL_FIELD_BITS)
    c3 = c2 + (1 << (2 * SEL_FIELD_BITS))
    fmask = (1 << SEL_FIELD_BITS) - 1

    def step(i, prefix):
        unit = lax.shift_left(np.int32(1), 30 - 2 * i)
        ts = [(prefix + j * unit) ^ INT_MIN for j in (1, 2, 3)]

        def count_tile(kt, acc):
            key = key_ref[kt]
            hits = jnp.where(key >= ts[2], c3, jnp.where(key >= ts[1], c2, jnp.where(key >= ts[0], c1, 0)))
            return acc + _fold_rows(hits)

        acc = lax.fori_loop(0, nkt, count_tile, jnp.zeros((SUBLANES, qb), I32))
        cnt = [jnp.sum(((acc >> (j * SEL_FIELD_BITS)) & fmask).astype(F32), axis=0, keepdims=True)
               for j in range(3)]
        digit = jnp.where(cnt[2] >= kf, 3, jnp.where(cnt[1] >= kf, 2, jnp.where(cnt[0] >= kf, 1, 0)))
        return prefix + digit * unit

    return lax.fori_loop(0, 16, step, jnp.zeros((1, qb), I32)) ^ INT_MIN


def _attn_kernel(q_ref, qi_ref, mq_ref, mk_ref, kv_ref, o_ref,
                 key_ref, pen_ref, nd_ref, s_ref, m_ref, l_ref, acc_ref, *, topk):
    _, tk, qb = key_ref.shape
    t = pl.program_id(1)
    nkt = t + 1
    kf = float(topk)
    krow = lax.broadcasted_iota(I32, (tk, qb), 0)
    qlane = lax.broadcasted_iota(I32, (tk, qb), 1)
    qchunk = (t * qb + qlane) >> CHUNK_SHIFT
    rel = qlane - krow

    def tile_geom(kt):
        col = kt * tk + krow
        visible = (col >> CHUNK_SHIFT) <= qchunk
        neg_dist = -jnp.abs(rel + (t - kt) * tk).astype(F32)
        return col, visible, neg_dist

    def row_off(kt):
        return pl.multiple_of(kt * tk, tk)

    @pl.when(nkt * tk <= topk)
    def _():
        def fill(kt, carry):
            _, visible, neg_dist = tile_geom(kt)
            pen_ref[kt] = jnp.where(visible, 0.0, NEG_BIG)
            nd_ref[kt] = neg_dist
            return carry

        lax.fori_loop(0, nkt, fill, 0)

    @pl.when(nkt * tk > topk)
    def _():
        qi = qi_ref[...]
        qi_all = jnp.concatenate([qi[:, h * IDX_DIM:(h + 1) * IDX_DIM] for h in range(IDX_HEADS)],
                                 axis=0)
        mq_t = mq_ref[...].T
        w_rows = [mq_t[IDX_DIM + h:IDX_DIM + h + 1, :] for h in range(IDX_HEADS)]

        def score_tile(kt, carry):
            k_idx = mk_ref[pl.ds(row_off(kt), tk), 0:IDX_DIM].astype(BF16)
            logits = lax.dot_general(k_idx, qi_all, _NT, preferred_element_type=F32)
            score = jnp.zeros((tk, qb), F32)
            for h in range(IDX_HEADS):
                score = score + w_rows[h] * jnp.maximum(logits[:, h * qb:(h + 1) * qb], 0.0)
            _, visible, neg_dist = tile_geom(kt)
            score = jnp.where(visible, score, -jnp.inf)
            bits = pltpu.bitcast(score, I32)
            key_ref[kt] = bits ^ ((bits >> 31) & np.int32(0x7FFFFFFF))
            nd_ref[kt] = neg_dist
            return carry

        lax.fori_loop(0, nkt, score_tile, 0)

        thr = _radix4_select(key_ref, nkt, kf)

        def pen_tile(kt, carry):
            n_gt, n_eq = carry
            key = key_ref[kt]
            _, visible, _ = tile_geom(kt)
            pen_ref[kt] = jnp.where((key >= thr) & visible, 0.0, NEG_BIG)
            n_gt = n_gt + _fold_rows(jnp.where(key > thr, 1.0, 0.0))
            n_eq = n_eq + _fold_rows(jnp.where(key == thr, 1.0, 0.0))
            return n_gt, n_eq

        zero = jnp.zeros((SUBLANES, qb), F32)
        n_gt, n_eq = lax.fori_loop(0, nkt, pen_tile, (zero, zero))
        need = kf - jnp.sum(n_gt, axis=0, keepdims=True)
        n_eq = jnp.sum(n_eq, axis=0, keepdims=True)

        tie_rows = jnp.where((n_eq > need) & (thr > KEY_NEG_INF), 1.0, 0.0)

        @pl.when(jnp.max(tie_rows) > 0.0)
        def _():
            nbits = int(key_ref.shape[0] * tk - 1).bit_length()

            def idx_step(i, p):
                trial = p | lax.shift_left(np.int32(1), nbits - 1 - i)

                def count_tile(kt, acc):
                    col, _, _ = tile_geom(kt)
                    hit = (key_ref[kt] == thr) & (col < trial)
                    return acc + _fold_rows(jnp.where(hit, 1.0, 0.0))

                acc = lax.fori_loop(0, nkt, count_tile, jnp.zeros((SUBLANES, qb), F32))
                cnt = jnp.sum(acc, axis=0, keepdims=True)
                return jnp.where(cnt < need, trial, p)

            last = lax.fori_loop(0, nbits, idx_step, jnp.zeros((1, qb), I32))

            def repen_tile(kt, carry):
                key = key_ref[kt]
                col, visible, _ = tile_geom(kt)
                sel = (key > thr) | ((key == thr) & (col <= last))
                pen_ref[kt] = jnp.where(sel & visible, 0.0, NEG_BIG)
                return carry

            lax.fori_loop(0, nkt, repen_tile, 0)

    qs = (q_ref[...].astype(F32) * ATT_SCALE).astype(BF16)
    q_all = jnp.concatenate([qs[:, h * A_HEAD_DIM:(h + 1) * A_HEAD_DIM] for h in range(A_HEADS)],
                            axis=0)
    for h in range(A_HEADS):
        m_ref[h] = jnp.full((SUBLANES, qb), NEG_BIG, F32)
        l_ref[h] = jnp.zeros((SUBLANES, qb), F32)
        acc_ref[h] = jnp.zeros((A_HEAD_DIM, qb), F32)

    def logit_tile(kt, carry):
        kk = kv_ref[pl.ds(row_off(kt), tk), 0:A_HEAD_DIM]
        pen = pen_ref[kt]
        nd = nd_ref[kt]
        s_all = lax.dot_general(kk, q_all, _NT, preferred_element_type=F32)
        for h in range(A_HEADS):
            slope = float(2.0 ** (-8.0 * (h + 1) / A_HEADS))
            s = s_all[:, h * qb:(h + 1) * qb] + (slope * nd + pen)
            s_ref[h, kt] = s
            m_ref[h] = jnp.maximum(m_ref[h], _fold_rows(s, jnp.maximum))
        return carry

    lax.fori_loop(0, nkt, logit_tile, 0)
    m_fin = [jnp.max(m_ref[h], axis=0, keepdims=True) for h in range(A_HEADS)]

    def prob_tile(kt, carry):
        v_t = kv_ref[pl.ds(row_off(kt), tk), A_HEAD_DIM:2 * A_HEAD_DIM].T
        probs = []
        for h in range(A_HEADS):
            p = jnp.exp(s_ref[h, kt] - m_fin[h])
            l_ref[h] = l_ref[h] + _fold_rows(p)
            probs.append(p.astype(BF16))
        pv = jnp.dot(v_t, jnp.concatenate(probs, axis=1), preferred_element_type=F32)
        for h in range(A_HEADS):
            acc_ref[h] = acc_ref[h] + pv[:, h * qb:(h + 1) * qb]
        return carry

    lax.fori_loop(0, nkt, prob_tile, 0)
    outs = [(acc_ref[h] / jnp.sum(l_ref[h], axis=0, keepdims=True)).T for h in range(A_HEADS)]
    o_ref[...] = jnp.concatenate(outs, axis=-1).astype(BF16)


def _attention(q, qi, misc, kv, batch, seq):
    n = q.shape[0]
    qb = min(Q_BLOCK, seq)
    nb = seq // qb
    topk = min(TOPK_MAX, seq // 4)
    assert seq // SUBLANES < (1 << SEL_FIELD_BITS)
    rowq = lambda b, t: (b * nb + t, 0)
    rowk = lambda b, t: (b, 0)
    tiles = (nb, qb, qb)
    return pl.pallas_call(
        functools.partial(_attn_kernel, topk=topk),
        grid=(batch, nb),
        in_specs=[pl.BlockSpec((qb, A_WIDTH), rowq),
                  pl.BlockSpec((qb, IDX_HEADS * IDX_DIM), rowq),
                  pl.BlockSpec((qb, LANES), rowq),
                  pl.BlockSpec((seq, LANES), rowk),
                  pl.BlockSpec((seq, 2 * A_HEAD_DIM), rowk)],
        out_specs=pl.BlockSpec((qb, A_WIDTH), rowq),
        out_shape=jax.ShapeDtypeStruct((n, A_WIDTH), BF16),
        scratch_shapes=[pltpu.VMEM(tiles, I32), pltpu.VMEM(tiles, F32), pltpu.VMEM(tiles, F32),
                        pltpu.VMEM((A_HEADS,) + tiles, F32),
                        pltpu.VMEM((A_HEADS, SUBLANES, qb), F32),
                        pltpu.VMEM((A_HEADS, SUBLANES, qb), F32),
                        pltpu.VMEM((A_HEADS, A_HEAD_DIM, qb), F32)],
        compiler_params=_params("parallel", "parallel"),
        name="dsa_attention",
    )(q, qi, misc, misc, kv)


def _out_proj_kernel(x_ref, a_ref, b_ref, w_ref, mod_ref, o_ref):
    y = jnp.dot(a_ref[...], w_ref[0:A_WIDTH, :], preferred_element_type=F32)
    y = y + jnp.dot(b_ref[...], w_ref[A_WIDTH:, :], preferred_element_type=F32)
    o_ref[...] = x_ref[...] + mod_ref[0, 2:3, :] * y


def _out_proj(x, a, bo, w_o, mod_l, batch, seq):
    n, d = x.shape
    tm = min(PROJ_TM, seq)
    tps = seq // tm
    row = lambda b, t: (b * tps + t, 0)
    return pl.pallas_call(
        _out_proj_kernel,
        grid=(batch, tps),
        in_specs=[pl.BlockSpec((tm, d), row),
                  pl.BlockSpec((tm, A_WIDTH), row),
                  pl.BlockSpec((tm, B_WIDTH), row),
                  pl.BlockSpec(w_o.shape, lambda b, t: (0, 0)),
                  pl.BlockSpec((1, 6, d), lambda b, t: (b, 0, 0))],
        out_specs=pl.BlockSpec((tm, d), row),
        out_shape=jax.ShapeDtypeStruct((n, d), F32),
        compiler_params=_params("parallel", "parallel"),
        name="out_proj",
    )(x, a, bo, w_o, mod_l)


def _pool_kernel(x_ref, halo_ref, mod_ref, g_ref, w_ref, sc_ref, o_ref):
    tm, d = x_ref.shape
    t = pl.program_id(1)
    g = g_ref[...]
    shift = mod_ref[0, 0:1, :]
    scale = mod_ref[0, 1:2, :]
    x = x_ref[...]
    h = _norm_mod(x, g, shift, scale)
    hh = _norm_mod(halo_ref[...], g, shift, scale)
    hh = jnp.where(t > 0, hh, 0.0)
    ext = jnp.concatenate([hh, h], axis=0)
    pos = t * tm + lax.broadcasted_iota(I32, (tm, 1), 0)
    gd = d // len(POOL_WINDOWS)
    ys = []
    for gi, win in enumerate(POOL_WINDOWS):
        cols = slice(gi * gd, (gi + 1) * gd)
        cur = ext[:, cols]
        k = 1
        while k < win:
            cur = cur + pltpu.roll(cur, k, axis=0)
            k *= 2
        cnt = jnp.minimum(pos + 1, win).astype(F32)
        y = (cur[POOL_HALO:, :] / cnt - h[:, cols]).astype(BF16)
        ys.append(jnp.dot(y, w_ref[gi], preferred_element_type=F32))
    y = jnp.concatenate(ys, axis=-1) * sc_ref[...]
    o_ref[...] = x + mod_ref[0, 2:3, :] * y


def _pool_mixer(x, mod_l, g, w_pool, scale, batch, seq):
    n, d = x.shape
    tm = min(POOL_TM, seq)
    tps = seq // tm
    row = lambda b, t: (b * tps + t, 0)
    hpt = tm // POOL_HALO
    halo = lambda b, t: (jnp.maximum((b * tps + t) * hpt - 1, 0), 0)
    return pl.pallas_call(
        _pool_kernel,
        grid=(batch, tps),
        in_specs=[pl.BlockSpec((tm, d), row),
                  pl.BlockSpec((POOL_HALO, d), halo),
                  pl.BlockSpec((1, 6, d), lambda b, t: (b, 0, 0)),
                  pl.BlockSpec((1, d), lambda b, t: (0, 0)),
                  pl.BlockSpec(w_pool.shape, lambda b, t: (0, 0, 0)),
                  pl.BlockSpec((1, d), lambda b, t: (0, 0))],
        out_specs=pl.BlockSpec((tm, d), row),
        out_shape=jax.ShapeDtypeStruct((n, d), F32),
        compiler_params=_params("parallel", "parallel"),
        name="pool_mixer",
    )(x, x, mod_l, g, w_pool, scale)


_R_EXP0 = N_GROUPS
RT_TM = 512
SEG_ALIGN = 16
SEG_SHIFT = 4
E_TM = 1152
EXPERT_STEP = 4

_TN = (((0,), (0,)), ((), ()))


def _moe_layout(n):
    tm = min(RT_TM, n)
    xs_rows = tm + N_GROUPS * SEG_ALIGN
    e_tm = E_TM if n >= 4 * E_TM else 128
    worst = n + (n // tm) * N_GROUPS * SEG_ALIGN + N_GROUPS * e_tm
    rows = -(-worst // e_tm) * e_tm
    return tm, xs_rows, e_tm, rows


def _perm_matrix(d_row, rows):
    tm = d_row.shape[1]
    return jnp.where(lax.broadcasted_iota(I32, (rows, tm), 0) == d_row, 1.0, 0.0).astype(BF16)


def _split_bf16(a):
    hi = a.astype(BF16)
    return hi, (a - hi.astype(F32)).astype(BF16)


def _route_kernel(x_ref, mod_ref, g_ref, whi_ref, wlo_ref, b_ref, tri_ref,
                  xs_ref, gs_ref, d_ref, cnt_ref):
    tm = x_ref.shape[0]
    xs_rows = xs_ref.shape[0]
    h = _norm_mod(x_ref[...], g_ref[...], mod_ref[0, 3:4, :], mod_ref[0, 4:5, :])
    h_hi, h_lo = _split_bf16(h)
    logits = (jnp.dot(h_hi, wlo_ref[...], preferred_element_type=F32)
              + jnp.dot(h_lo, whi_ref[...], preferred_element_type=F32)
              + jnp.dot(h_hi, whi_ref[...], preferred_element_type=F32)) + b_ref[...]
    lane = lax.broadcasted_iota(I32, (tm, LANES), 1)
    big = np.int32(LANES)

    def top1(vals):
        m = jnp.max(vals, axis=-1, keepdims=True)
        idx = jnp.min(jnp.where(vals == m, lane, big), axis=-1, keepdims=True)
        return m, idx

    gl = jnp.where(lane < N_GROUPS, logits, -jnp.inf)
    gmax, gsel = top1(gl)
    p_g = 1.0 / jnp.sum(jnp.exp(gl - gmax), axis=-1, keepdims=True)
    e_lo = _R_EXP0 + gsel * EXPERTS_PER_GROUP
    el = jnp.where((lane >= e_lo) & (lane < e_lo + EXPERTS_PER_GROUP), logits, -jnp.inf)
    v1, i1 = top1(el)
    v2, i2 = top1(jnp.where(lane == i1, -jnp.inf, el))
    e2 = jnp.exp(v2 - v1)
    g1 = p_g / (1.0 + e2)
    g2 = p_g * e2 / (1.0 + e2)
    cg = jnp.where(lane == i1 - e_lo, g1, 0.0) + jnp.where(lane == i2 - e_lo, g2, 0.0)

    member = jnp.where(lane == gsel, 1.0, 0.0).T[0:8, :]
    rank = jnp.dot(member.astype(BF16), tri_ref[...], preferred_element_type=F32)
    count = jnp.sum(member, axis=1, keepdims=True)
    cnt_ref[0] = jnp.broadcast_to(count, (8, LANES))
    padded = jnp.ceil(count * (1.0 / SEG_ALIGN)) * SEG_ALIGN
    d_row = jnp.zeros((1, tm), F32)
    seg = jnp.zeros((1, 1), F32)
    for g in range(N_GROUPS):
        d_row = d_row + member[g:g + 1, :] * (rank[g:g + 1, :] + seg)
        seg = seg + padded[g:g + 1, :]
    d_row = d_row.astype(I32)
    d_ref[0] = d_row

    perm = _perm_matrix(d_row, xs_rows)
    d = h_hi.shape[1]
    cg_hi, cg_lo = _split_bf16(cg)
    srt = jnp.dot(perm, jnp.concatenate([h_hi, cg_hi, cg_lo], axis=1), preferred_element_type=F32)
    xs_ref[...] = srt[:, 0:d].astype(BF16)
    gs_ref[...] = srt[:, d:d + LANES] + srt[:, d + LANES:d + 2 * LANES]


def _route(x, mod_l, g, w_r, b_r, tri, batch, seq):
    n, d = x.shape
    tm, xs_rows, _, _ = _moe_layout(n)
    nt = n // tm
    tps = seq // tm
    full = lambda i: (0, 0)
    return pl.pallas_call(
        _route_kernel,
        grid=(nt,),
        in_specs=[pl.BlockSpec((tm, d), lambda i: (i, 0)),
                  pl.BlockSpec((1, 6, d), lambda i: (i // tps, 0, 0)),
                  pl.BlockSpec((1, d), full),
                  pl.BlockSpec((d, LANES), full),
                  pl.BlockSpec((d, LANES), full),
                  pl.BlockSpec((1, LANES), full),
                  pl.BlockSpec((tm, tm), full)],
        out_specs=[pl.BlockSpec((xs_rows, d), lambda i: (i, 0)),
                   pl.BlockSpec((xs_rows, LANES), lambda i: (i, 0)),
                   pl.BlockSpec((1, 1, tm), lambda i: (i, 0, 0)),
                   pl.BlockSpec((1, 8, LANES), lambda i: (i, 0, 0))],
        out_shape=[jax.ShapeDtypeStruct((nt * xs_rows, d), BF16),
                   jax.ShapeDtypeStruct((nt * xs_rows, LANES), F32),
                   jax.ShapeDtypeStruct((nt, 1, tm), I32),
                   jax.ShapeDtypeStruct((nt, 8, LANES), F32)],
        compiler_params=_params("parallel"),
        name="route_sort",
    )(x, mod_l, g, *_split_bf16(w_r), b_r, tri)


def _dispatch_kernel(src_ref, dst_ref, len_ref, fill_ref, xt_ref, gt_ref, xs_hbm, gs_hbm,
                     zx_ref, zg_ref, sem_x, sem_g):
    i = pl.program_id(0)
    nt = src_ref.shape[0]

    def chunk_copies(x_src, g_src, src_row, dst_row):
        src_row = pl.multiple_of(src_row, SEG_ALIGN)
        dst_row = pl.multiple_of(dst_row, SEG_ALIGN)
        return (pltpu.make_async_copy(x_src.at[pl.ds(src_row, SEG_ALIGN)],
                                      xs_hbm.at[pl.ds(dst_row, SEG_ALIGN)], sem_x),
                pltpu.make_async_copy(g_src.at[pl.ds(src_row, SEG_ALIGN)],
                                      gs_hbm.at[pl.ds(dst_row, SEG_ALIGN)], sem_g))

    def run(x_src, g_src, jobs):
        total = jnp.int32(0)
        for src, dst, rows in jobs:
            nch = rows >> SEG_SHIFT

            def issue(k, carry, src=src, dst=dst):
                for prio, cp in enumerate(chunk_copies(x_src, g_src, src + k * SEG_ALIGN,
                                                       dst + k * SEG_ALIGN)):
                    cp.start(priority=prio)
                return carry

            lax.fori_loop(0, nch, issue, 0)
            total = total + nch

        def drain(k, carry):
            for cp in chunk_copies(x_src, g_src, 0, 0):
                cp.wait()
            return carry

        lax.fori_loop(0, total, drain, 0)

    @pl.when(i < nt)
    def _():
        t = jnp.minimum(i, nt - 1)
        run(xt_ref, gt_ref, [(src_ref[t, g], dst_ref[t, g], len_ref[t, g]) for g in range(N_GROUPS)])

    @pl.when(i == nt)
    def _():
        zx_ref[...] = jnp.zeros_like(zx_ref)
        zg_ref[...] = jnp.zeros_like(zg_ref)
        for g in range(N_GROUPS + 1):
            start = fill_ref[0, g]
            nch = fill_ref[1, g] >> SEG_SHIFT

            def issue(k, carry, start=start):
                for prio, cp in enumerate(chunk_copies(zx_ref, zg_ref, 0, start + k * SEG_ALIGN)):
                    cp.start(priority=prio)
                return carry

            lax.fori_loop(0, nch, issue, 0)

            def drain(k, carry):
                for cp in chunk_copies(zx_ref, zg_ref, 0, 0):
                    cp.wait()
                return carry

            lax.fori_loop(0, nch, drain, 0)


def _dispatch(xt, gt, src_off, dst_off, seg_len, fill, rows):
    d = xt.shape[1]
    nt = src_off.shape[0]
    xs_rows = xt.shape[0] // nt
    tile = lambda i, *_: (jnp.minimum(i, nt - 1), 0)
    return pl.pallas_call(
        _dispatch_kernel,
        grid_spec=pltpu.PrefetchScalarGridSpec(
            num_scalar_prefetch=4,
            grid=(nt + 1,),
            in_specs=[pl.BlockSpec((xs_rows, d), tile), pl.BlockSpec((xs_rows, LANES), tile)],
            out_specs=[pl.BlockSpec(memory_space=pl.ANY), pl.BlockSpec(memory_space=pl.ANY)],
            scratch_shapes=[pltpu.VMEM((SEG_ALIGN, d), BF16), pltpu.VMEM((SEG_ALIGN, LANES), F32),
                            pltpu.SemaphoreType.DMA, pltpu.SemaphoreType.DMA]),
        out_shape=[jax.ShapeDtypeStruct((rows, d), BF16),
                   jax.ShapeDtypeStruct((rows, LANES), F32)],
        compiler_params=_params("arbitrary"),
        name="dispatch",
    )(src_off, dst_off, seg_len, fill, xt, gt)


def _experts_kernel(grp_ref, valid_ref, x_ref, gs_ref, wg_ref, wu_ref, wd_ref, o_ref):
    j = pl.program_id(0)
    e = pl.program_id(1)

    @pl.when(e == 0)
    def _():
        o_ref[...] = jnp.zeros_like(o_ref)

    @pl.when(valid_ref[j] > 0)
    def _():
        x = x_ref[...]
        es, f, d = wd_ref.shape[1:]
        lane = lax.broadcasted_iota(I32, gs_ref.shape, 1)
        acts = []
        for k in range(es):
            hg = jnp.dot(x, wg_ref[0, k].astype(BF16), preferred_element_type=F32)
            hu = jnp.dot(x, wu_ref[0, k].astype(BF16), preferred_element_type=F32)
            gate = jnp.sum(jnp.where(lane == e * es + k, gs_ref[...], 0.0), axis=-1, keepdims=True)
            acts.append(((hg * jax.nn.sigmoid(hg)) * hu * gate).astype(BF16))
        wd = wd_ref[0].astype(BF16).reshape(es * f, d)
        o_ref[...] += jnp.dot(jnp.concatenate(acts, axis=1), wd, preferred_element_type=F32)


def _experts(xs, gs, tile_grp, tile_valid, wg, wu, wd, layer, e_tm):
    rows, d = xs.shape
    f = wg.shape[-1]
    ntile = tile_grp.shape[0]
    row = lambda j, e, grp, valid: (j, 0)
    steps = EXPERTS_PER_GROUP // EXPERT_STEP
    wsel = lambda j, e, grp, valid: (
        layer, grp[j] * steps + jnp.where(valid[j] > 0, e, steps - 1), 0, 0)
    return pl.pallas_call(
        _experts_kernel,
        grid_spec=pltpu.PrefetchScalarGridSpec(
            num_scalar_prefetch=2,
            grid=(ntile, steps),
            in_specs=[pl.BlockSpec((e_tm, d), row),
                      pl.BlockSpec((e_tm, LANES), row),
                      pl.BlockSpec((1, EXPERT_STEP, d, f), wsel),
                      pl.BlockSpec((1, EXPERT_STEP, d, f), wsel),
                      pl.BlockSpec((1, EXPERT_STEP, f, d), wsel)],
            out_specs=pl.BlockSpec((e_tm, d), row)),
        out_shape=jax.ShapeDtypeStruct((rows, d), F32),
        compiler_params=_params("arbitrary", "arbitrary"),
        name="experts",
    )(tile_grp, tile_valid, xs, gs, wg, wu, wd)


def _combine_kernel(meta_ref, x_ref, d_ref, mod_ref, gout_ref, ys_hbm, o_ref, yseg, sem, *, final_norm):
    i = pl.program_id(0)
    nt = pl.num_programs(0)
    xs_rows = yseg.shape[1]

    def seg_copy(slot, src_row, dst_row):
        src_row = pl.multiple_of(src_row, SEG_ALIGN)
        dst_row = pl.multiple_of(dst_row, SEG_ALIGN)
        return pltpu.make_async_copy(ys_hbm.at[pl.ds(src_row, SEG_ALIGN)],
                                     yseg.at[slot, pl.ds(dst_row, SEG_ALIGN)], sem.at[slot])

    def fetch(tile, slot):
        yseg[slot] = jnp.zeros(yseg.shape[1:], F32)
        seg = jnp.int32(0)
        for g in range(N_GROUPS):
            base = meta_ref[tile, g]
            c16 = meta_ref[tile, N_GROUPS + g]

            def issue(k, carry, seg=seg, base=base):
                seg_copy(slot, base + k * SEG_ALIGN, seg + k * SEG_ALIGN).start()
                return carry

            lax.fori_loop(0, c16 >> SEG_SHIFT, issue, 0)
            seg = seg + c16

    @pl.when(i == 0)
    def _():
        fetch(0, 0)

    @pl.when(i + 1 < nt)
    def _():
        fetch(jnp.minimum(i + 1, nt - 1), (i + 1) % 2)

    slot = i % 2
    total_chunks = jnp.int32(0)
    for g in range(N_GROUPS):
        total_chunks = total_chunks + (meta_ref[i, N_GROUPS + g] >> SEG_SHIFT)

    def drain(k, carry):
        seg_copy(slot, 0, 0).wait()
        return carry

    lax.fori_loop(0, total_chunks, drain, 0)

    perm = _perm_matrix(d_ref[0], xs_rows)
    y_hi, y_lo = _split_bf16(yseg[slot])
    y = (lax.dot_general(perm, y_hi, _TN, preferred_element_type=F32)
         + lax.dot_general(perm, y_lo, _TN, preferred_element_type=F32))
    out = x_ref[...] + mod_ref[0, 5:6, :] * y
    if final_norm:
        out = out * lax.rsqrt(jnp.mean(out * out, axis=-1, keepdims=True) + EPS) * gout_ref[...]
    o_ref[...] = out


def _combine(x, d_rows, meta, ys, mod_l, g_out, batch, seq, final_norm):
    n, d = x.shape
    tm, xs_rows, _, _ = _moe_layout(n)
    tps = seq // tm
    return pl.pallas_call(
        functools.partial(_combine_kernel, final_norm=final_norm),
        grid_spec=pltpu.PrefetchScalarGridSpec(
            num_scalar_prefetch=1,
            grid=(n // tm,),
            in_specs=[pl.BlockSpec((tm, d), lambda i, meta: (i, 0)),
                      pl.BlockSpec((1, 1, tm), lambda i, meta: (i, 0, 0)),
                      pl.BlockSpec((1, 6, d), lambda i, meta: (i // tps, 0, 0)),
                      pl.BlockSpec((1, d), lambda i, meta: (0, 0)),
                      pl.BlockSpec(memory_space=pl.ANY)],
            out_specs=pl.BlockSpec((tm, d), lambda i, meta: (i, 0)),
            scratch_shapes=[pltpu.VMEM((2, xs_rows, d), F32), pltpu.SemaphoreType.DMA((2,))]),
        out_shape=jax.ShapeDtypeStruct((n, d), F32),
        compiler_params=_params("arbitrary"),
        name="combine",
    )(meta, x, d_rows, mod_l, g_out, ys)


def _moe_plan(cnt, n):
    _, _, e_tm, rows = _moe_layout(n)
    seg_len = ((cnt + (SEG_ALIGN - 1)) // SEG_ALIGN) * SEG_ALIGN
    totals = jnp.sum(seg_len, axis=0)
    region = ((totals + (e_tm - 1)) // e_tm) * e_tm
    region_end = jnp.cumsum(region)
    region_start = region_end - region
    src_off = jnp.cumsum(seg_len, axis=1) - seg_len
    dst_off = region_start[None, :] + jnp.cumsum(seg_len, axis=0) - seg_len
    fill_start = jnp.concatenate([region_start + totals, region_end[-1:]])
    fill_len = jnp.concatenate([region - totals, rows - region_end[-1:]])
    fill = jnp.stack([fill_start, fill_len]).astype(I32)
    j = jnp.arange(rows // e_tm, dtype=I32) * e_tm
    valid = j < region_end[-1]
    grp = jnp.minimum(jnp.sum(j[:, None] >= region_end[None, :], axis=1), N_GROUPS - 1)
    last_grp = jnp.max(jnp.where(region > 0, jnp.arange(N_GROUPS), 0))
    grp = jnp.where(valid, grp, last_grp).astype(I32)
    return (src_off.astype(I32), dst_off.astype(I32), seg_len.astype(I32), fill, grp,
            valid.astype(I32))


def _moe(x, mod_l, g_ffn, w_r, b_r, tri, wg, wu, wd, layer, g_out, batch, seq, final_norm):
    n = x.shape[0]
    _, _, e_tm, rows = _moe_layout(n)
    xt, gt, d_rows, cnt = _route(x, mod_l, g_ffn, w_r, b_r, tri, batch, seq)
    cnt = cnt[:, 0:N_GROUPS, 0].astype(I32)
    src_off, dst_off, seg_len, fill, grp, valid = _moe_plan(cnt, n)
    xs, gs = _dispatch(xt, gt, src_off, dst_off, seg_len, fill, rows)
    ys = _experts(xs, gs, grp, valid, wg, wu, wd, layer, e_tm)
    meta = jnp.concatenate([dst_off, seg_len], axis=1)
    return _combine(x, d_rows, meta, ys, mod_l, g_out, batch, seq, final_norm)


def kernel(x, c, w_mod, b_mod, norm_mix, norm_ffn, w_in, kv_norm, w_kv_up, sgu_norm, w_s, b_s, w_o,
           w_pool, pool_scale, w_gr, b_gr, w_er, b_er, w_gate, w_up, w_down, norm_out):
    batch, seq, d = x.shape
    depth = w_mod.shape[0]
    n = batch * seq
    xs = x.reshape(n, d)
    mod = _modulation(c, w_mod, b_mod).reshape(depth, batch, 6, d)

    pos = np.arange(SGU_CHUNK)
    sgu_mask = jnp.asarray((pos[None, :] // CHUNK) <= (pos[:, None] // CHUNK))
    sizes = (A_WIDTH, KV_LATENT, IDX_HEADS * IDX_DIM, IDX_DIM, IDX_HEADS, B_WIDTH, B_WIDTH)
    offs = np.concatenate([[0], np.cumsum(sizes)])
    pad = LANES - IDX_DIM - IDX_HEADS
    rt_tm = _moe_layout(n)[0]
    tri = jnp.asarray(np.triu(np.ones((rt_tm, rt_tm), np.float32), 1), BF16)

    for i in range(depth):
        j = i // 2
        mod_l = mod[i]
        g_mix = norm_mix[i].reshape(1, d)
        if i % 2 == 0:
            wi = w_in[j]
            w_in_r = jnp.concatenate(
                [wi[:, offs[0]:offs[3]], wi[:, offs[3]:offs[5]], jnp.zeros((d, pad), wi.dtype),
                 wi[:, offs[5]:offs[7]]], axis=1).astype(BF16)
            w_s_m = jnp.where(sgu_mask[None], w_s[j], 0.0).astype(BF16)
            b_s_b = jnp.broadcast_to(b_s[j][:, :, None], (B_GROUPS, SGU_CHUNK, B_WIDTH // B_GROUPS))
            q, kv, qi, misc, bo = _in_proj(
                xs, mod_l, g_mix, w_in_r, kv_norm[j].reshape(1, -1), w_kv_up[j].astype(BF16),
                sgu_norm[j].reshape(1, -1), w_s_m, b_s_b, batch, seq)
            a = _attention(q, qi, misc, kv, batch, seq)
            xs = _out_proj(xs, a, bo, w_o[j].astype(BF16), mod_l, batch, seq)
        else:
            xs = _pool_mixer(xs, mod_l, g_mix, w_pool[j].astype(BF16), pool_scale[j].reshape(1, d),
                             batch, seq)
        w_r = jnp.concatenate([w_gr[i], w_er[i],
                               jnp.zeros((d, LANES - N_GROUPS - N_EXPERTS), F32)], axis=1)
        b_r = jnp.concatenate([b_gr[i], b_er[i],
                               jnp.zeros((LANES - N_GROUPS - N_EXPERTS,), F32)]).reshape(1, LANES)
        xs = _moe(xs, mod_l, norm_ffn[i].reshape(1, d), w_r, b_r, tri, w_gate, w_up, w_down, i,
                  norm_out.reshape(1, d), batch, seq, final_norm=(i == depth - 1))
    return xs.reshape(batch, seq, d)
```
